```python
import math
import jax, jax.numpy as jnp
from jax import lax
import numpy as np

D_MODEL = 1024
BATCH = 2
SEQ = 8192
DEPTH = 1
DEC_BATCH = 128
DEC_SEQ = 1
PAST_LEN = 2048
PAGE_SIZE = 128

HEAD_DIM_A = 64
N_HEADS_A = 6
D_A = N_HEADS_A * HEAD_DIM_A
D_B = D_MODEL - D_A
N_HEADS_B = 4
HEAD_DIM_B = D_B // N_HEADS_B
DILATED_BRANCHES = ((128, 1), (512, 4), (2048, 16))
WINDOW_MAX = 2048
N_BUCKETS = 32
REL_MAX_DIST = 2048
CONV_W = 4
MLSTM_CHUNK = 128
D_FF = 4 * D_MODEL
N_GATES = 2 * N_HEADS_B
D_IN = 3 * D_A + 2 * D_B + N_GATES
SPLITS = [D_A, 2 * D_A, 3 * D_A, 3 * D_A + D_B, 3 * D_A + 2 * D_B]
EPS = 1e-6
NEG = -1e30

kernel_name = "hybrid_dilated_attn_mlstm_decode_step"


def rmsnorm(x, g):
    xf = x.astype(jnp.float32)
    y = xf * lax.rsqrt(jnp.mean(xf * xf, axis=-1, keepdims=True) + EPS)
    return (y * g.astype(jnp.float32)).astype(x.dtype)


def t5_bucket(dist):
    max_exact = N_BUCKETS // 2
    df = jnp.maximum(dist, 1).astype(jnp.float32)
    large = max_exact + (jnp.log(df / max_exact) / math.log(REL_MAX_DIST / max_exact)
                         * (N_BUCKETS - max_exact)).astype(jnp.int32)
    large = jnp.minimum(large, N_BUCKETS - 1)
    return jnp.where(dist < max_exact, dist, large)


def dilated_prompt(q, k, v, rel_bias, window, dil):
    B, S, H, E = q.shape
    nk = window // dil
    blk = nk
    unit = blk * dil
    s_pad = -(-S // unit) * unit
    nb = s_pad // unit
    padw = ((0, 0), (0, s_pad - S), (0, 0), (0, 0))

    def to_blocks(t):
        return jnp.pad(t, padw).reshape(B, nb, blk, dil, H, E)

    def with_prev(t):
        prev = jnp.pad(t, ((0, 0), (1, 0), (0, 0), (0, 0), (0, 0), (0, 0)))[:, :-1]
        return jnp.concatenate([prev, t], axis=2)

    qb = to_blocks(q)
    kc = with_prev(to_blocks(k))
    vc = with_prev(to_blocks(v))
    qi = jnp.arange(blk)[:, None]
    ki = jnp.arange(2 * blk)[None, :]
    j = qi + blk - ki
    band = (j >= 0) & (j <= nk)
    blk_idx = jnp.arange(nb)[:, None, None]
    mask = band[None] & (blk_idx * blk + ki[None] - blk >= 0)
    bias = rel_bias[t5_bucket(jnp.clip(j, 0, None) * dil)].astype(jnp.float32)
    bias = bias.transpose(2, 0, 1)
    scale = 1.0 / math.sqrt(E)
    s = jnp.einsum('bnqrhe,bnkrhe->bnrhqk', qb, kc).astype(jnp.float32) * scale + bias
    s = jnp.where(mask[None, :, None, None], s, NEG)
    lse = jax.nn.logsumexp(s, axis=-1)
    p = jnp.exp(s - lse[..., None]).astype(v.dtype)
    o = jnp.einsum('bnrhqk,bnkrhe->bnqrhe', p, vc).reshape(B, s_pad, H, E)[:, :S]
    lse = lse.transpose(0, 1, 4, 2, 3).reshape(B, s_pad, H)[:, :S]
    return o, lse


def dilated_step(q, k_all, v_all, rel_bias, window, dil):
    B, T, H, E = q.shape
    P = k_all.shape[1] - T
    nk = window // dil
    j = jnp.arange(nk + 1)
    idx = P + jnp.arange(T)[:, None] - j[None, :] * dil
    valid = idx >= 0
    idx = jnp.maximum(idx, 0)
    kg = k_all[:, idx]
    vg = v_all[:, idx]
    bias = rel_bias[t5_bucket(j * dil)].astype(jnp.float32).T
    scale = 1.0 / math.sqrt(E)
    s = jnp.einsum('bthe,btjhe->bthj', q, kg).astype(jnp.float32) * scale + bias
    s = jnp.where(valid[None, :, None, :], s, NEG)
    lse = jax.nn.logsumexp(s, axis=-1)
    p = jnp.exp(s - lse[..., None]).astype(v_all.dtype)
    o = jnp.einsum('bthj,btjhe->bthe', p, vg)
    return o, lse


def combine_branches(branches):
    o = jnp.stack([b[0] for b in branches])
    lse = jnp.stack([b[1] for b in branches])
    w = jax.nn.softmax(lse, axis=0).astype(o.dtype)
    return jnp.einsum('gbsh,gbshe->bshe', w, o)


def causal_conv(x, buf, w, b):
    S = x.shape[1]
    xp = jnp.concatenate([buf.astype(x.dtype), x], axis=1)
    y = b + sum(xp[:, i:i + S] * w[i] for i in range(CONV_W))
    return y, xp[:, -(CONV_W - 1):]


def mlstm_cell(q, k, v, i_pre, f_pre, C0, n0, m0):
    B, S, H, E = q.shape
    L = min(MLSTM_CHUNK, S)
    s_pad = -(-S // L) * L
    nc = s_pad // L
    pad = s_pad - S
    f32 = jnp.float32

    def chunks4(t):
        t = jnp.pad(t.astype(f32), ((0, 0), (0, pad), (0, 0), (0, 0)))
        return t.reshape(B, nc, L, H, E).transpose(1, 0, 3, 2, 4)

    def chunks3(t, fill):
        t = jnp.pad(t.astype(f32), ((0, 0), (0, pad), (0, 0)), constant_values=fill)
        return t.reshape(B, nc, L, H).transpose(1, 0, 3, 2)

    logf = jax.nn.log_sigmoid(f_pre.astype(f32))
    xs = (chunks4(q), chunks4(k), chunks4(v), chunks3(i_pre, NEG), chunks3(logf, 0.0))
    causal = jnp.tril(jnp.ones((L, L), dtype=bool))

    def step(carry, inp):
        C, n, m = carry
        qc, kc, vc, ic, fc = inp
        a = jnp.cumsum(fc, axis=-1)
        g = a + m[..., None]
        D = jnp.where(causal, a[..., :, None] - a[..., None, :] + ic[..., None, :], NEG)
        m_t = jnp.maximum(g, jnp.max(D, axis=-1))
        w_state = jnp.exp(g - m_t)
        A = jnp.einsum('bhte,bhse->bhts', qc, kc) * jnp.exp(D - m_t[..., None])
        num = (w_state[..., None] * jnp.einsum('bhvk,bhtk->bhtv', C, qc)
               + jnp.einsum('bhts,bhsv->bhtv', A, vc))
        den = w_state * jnp.einsum('bhk,bhtk->bht', n, qc) + jnp.sum(A, axis=-1)
        h = num / jnp.maximum(jnp.abs(den), jnp.exp(-m_t))[..., None]
        b_tot = a[..., -1]
        wl = b_tot[..., None] - a + ic
        m_new = jnp.maximum(b_tot + m, jnp.max(wl, axis=-1))
        wk = jnp.exp(wl - m_new[..., None])
        decay = jnp.exp(b_tot + m - m_new)
        C_new = decay[..., None, None] * C + jnp.einsum('bhs,bhsv,bhsk->bhvk', wk, vc, kc)
        n_new = decay[..., None] * n + jnp.einsum('bhs,bhsk->bhk', wk, kc)
        return (C_new, n_new, m_new), h

    (C, n, m), hs = lax.scan(step, (C0.astype(f32), n0.astype(f32), m0.astype(f32)), xs)
    h = hs.transpose(1, 0, 3, 2, 4).reshape(B, s_pad, H, E)[:, :S]
    return h, C, n, m


def hybrid_layer(x, win_k, win_v, conv_buf, C0, n0, m0, rel_bias,
                 norm1_g, w_in, gate_bias, conv_w, conv_b, wq_head, wk_head,
                 attn_out_g, mh_norm_g, skip, w_out, norm2_g, w_ff1, w_ff2):
    B, S, _ = x.shape
    h = rmsnorm(x, norm1_g)
    z = h @ w_in
    qa, ka, va, xb, ob, gates = jnp.split(z, SPLITS, axis=-1)
    qa = qa.reshape(B, S, N_HEADS_A, HEAD_DIM_A)
    ka = ka.reshape(B, S, N_HEADS_A, HEAD_DIM_A)
    va = va.reshape(B, S, N_HEADS_A, HEAD_DIM_A)

    if win_k is None:
        branches = [dilated_prompt(qa, ka, va, rel_bias, w, d) for (w, d) in DILATED_BRANCHES]
        P = min(WINDOW_MAX, S)
        new_k = ka[:, S - P:]
        new_v = va[:, S - P:]
    else:
        P = win_k.shape[1]
        k_all = jnp.concatenate([win_k.astype(ka.dtype), ka], axis=1)
        v_all = jnp.concatenate([win_v.astype(va.dtype), va], axis=1)
        branches = [dilated_step(qa, k_all, v_all, rel_bias, w, d) for (w, d) in DILATED_BRANCHES]
        new_k = k_all[:, -P:]
        new_v = v_all[:, -P:]
    out_a = rmsnorm(combine_branches(branches).reshape(B, S, D_A), attn_out_g)

    c, new_conv = causal_conv(xb, conv_buf, conv_w, conv_b)
    c_act = jax.nn.silu(c)
    c_h = c_act.reshape(B, S, N_HEADS_B, HEAD_DIM_B)
    qb = jnp.einsum('bshd,hde->bshe', c_h, wq_head)
    kb = jnp.einsum('bshd,hde->bshe', c_h, wk_head) * (1.0 / math.sqrt(HEAD_DIM_B))
    vb = xb.reshape(B, S, N_HEADS_B, HEAD_DIM_B)
    gates = gates + gate_bias
    i_pre, f_pre = gates[..., :N_HEADS_B], gates[..., N_HEADS_B:]
    hb, C, n, m = mlstm_cell(qb, kb, vb, i_pre, f_pre, C0, n0, m0)
    hb = hb * lax.rsqrt(jnp.mean(hb * hb, axis=-1, keepdims=True) + EPS)
    hb = hb.reshape(B, S, D_B) * mh_norm_g.astype(jnp.float32)
    out_b = jax.nn.sigmoid(ob) * (hb.astype(x.dtype) + skip * c_act)

    x = x + jnp.concatenate([out_a, out_b], axis=-1) @ w_out
    h2 = rmsnorm(x, norm2_g)
    x = x + jnp.square(jax.nn.relu(h2 @ w_ff1)) @ w_ff2
    return x, (new_k, new_v, new_conv, C, n, m)


def setup_inputs(seed: int = 0) -> dict:
    key = jax.random.key(seed)
    ks = jax.random.split(key, 32)
    nrm = jax.random.normal
    f32 = jnp.float32
    win_buf = min(WINDOW_MAX, PAST_LEN)
    gate_bias = jnp.concatenate([
        0.1 * nrm(ks[20], (DEPTH, N_HEADS_B), f32),
        jax.random.uniform(ks[21], (DEPTH, N_HEADS_B), f32, 3.0, 6.0)], axis=-1)
    return {
        "x_prompt": nrm(ks[0], (BATCH, SEQ, D_MODEL), f32),
        "x_sample": nrm(ks[1], (DEC_BATCH, DEC_SEQ, D_MODEL), f32),
        "cache_win_k": nrm(ks[2], (DEPTH, DEC_BATCH, win_buf, N_HEADS_A, HEAD_DIM_A), f32),
        "cache_win_v": nrm(ks[3], (DEPTH, DEC_BATCH, win_buf, N_HEADS_A, HEAD_DIM_A), f32),
        "state_conv": nrm(ks[4], (DEPTH, DEC_BATCH, CONV_W - 1, D_B), f32),
        "state_C": 0.1 * nrm(ks[5], (DEPTH, DEC_BATCH, N_HEADS_B, HEAD_DIM_B, HEAD_DIM_B), f32),
        "state_n": 0.5 * nrm(ks[6], (DEPTH, DEC_BATCH, N_HEADS_B, HEAD_DIM_B), f32),
        "state_m": nrm(ks[7], (DEPTH, DEC_BATCH, N_HEADS_B), f32),
        "rel_bias": 0.5 * nrm(ks[8], (N_BUCKETS, N_HEADS_A), f32),
        "norm1_g": 1.0 + 0.02 * nrm(ks[9], (DEPTH, D_MODEL), f32),
        "w_in": nrm(ks[10], (DEPTH, D_MODEL, D_IN), f32) * D_MODEL ** -0.5,
        "gate_bias": gate_bias,
        "conv_w": 0.5 * nrm(ks[11], (DEPTH, CONV_W, D_B), f32),
        "conv_b": 0.02 * nrm(ks[12], (DEPTH, D_B), f32),
        "wq_head": nrm(ks[13], (DEPTH, N_HEADS_B, HEAD_DIM_B, HEAD_DIM_B), f32) * HEAD_DIM_B ** -0.5,
        "wk_head": nrm(ks[14], (DEPTH, N_HEADS_B, HEAD_DIM_B, HEAD_DIM_B), f32) * HEAD_DIM_B ** -0.5,
        "attn_out_g": 1.0 + 0.02 * nrm(ks[15], (DEPTH, D_A), f32),
        "mh_norm_g": 1.0 + 0.02 * nrm(ks[16], (DEPTH, D_B), f32),
        "skip": 1.0 + 0.02 * nrm(ks[17], (DEPTH, D_B), f32),
        "w_out": nrm(ks[18], (DEPTH, D_MODEL, D_MODEL), f32) * D_MODEL ** -0.5,
        "norm2_g": 1.0 + 0.02 * nrm(ks[19], (DEPTH, D_MODEL), f32),
        "w_ff1": nrm(ks[22], (DEPTH, D_MODEL, D_FF), f32) * D_MODEL ** -0.5,
        "w_ff2": nrm(ks[23], (DEPTH, D_FF, D_MODEL), f32) * D_FF ** -0.5,
        "final_g": 1.0 + 0.02 * nrm(ks[24], (D_MODEL,), f32),
    }


def reference(x_prompt, x_sample, cache_win_k, cache_win_v, state_conv, state_C, state_n, state_m,
              rel_bias, norm1_g, w_in, gate_bias, conv_w, conv_b, wq_head, wk_head,
              attn_out_g, mh_norm_g, skip, w_out, norm2_g, w_ff1, w_ff2, final_g):
    xp, xs = x_prompt, x_sample
    Bp = x_prompt.shape[0]
    sts_p, sts_s = [], []
    for l in range(DEPTH):
        lp = (norm1_g[l], w_in[l], gate_bias[l], conv_w[l], conv_b[l], wq_head[l], wk_head[l],
              attn_out_g[l], mh_norm_g[l], skip[l], w_out[l], norm2_g[l], w_ff1[l], w_ff2[l])
        xp, st_p = hybrid_layer(
            xp, None, None,
            jnp.zeros((Bp, CONV_W - 1, D_B), xp.dtype),
            jnp.zeros((Bp, N_HEADS_B, HEAD_DIM_B, HEAD_DIM_B), jnp.float32),
            jnp.zeros((Bp, N_HEADS_B, HEAD_DIM_B), jnp.float32),
            jnp.zeros((Bp, N_HEADS_B), jnp.float32),
            rel_bias, *lp)
        xs, st_s = hybrid_layer(
            xs, cache_win_k[l], cache_win_v[l], state_conv[l], state_C[l], state_n[l], state_m[l],
            rel_bias, *lp)
        sts_p.append(st_p)
        sts_s.append(st_s)
    y_prompt = rmsnorm(xp, final_g)
    y_sample = rmsnorm(xs, final_g)
    p_k = jnp.stack([s[0] for s in sts_p])
    p_v = jnp.stack([s[1] for s in sts_p])
    p_conv = jnp.stack([s[2] for s in sts_p])
    p_C = jnp.stack([s[3] for s in sts_p])
    p_n = jnp.stack([s[4] for s in sts_p])
    p_m = jnp.stack([s[5] for s in sts_p])
    s_k = jnp.stack([s[0] for s in sts_s])
    s_v = jnp.stack([s[1] for s in sts_s])
    s_conv = jnp.stack([s[2] for s in sts_s])
    s_C = jnp.stack([s[3] for s in sts_s])
    s_n = jnp.stack([s[4] for s in sts_s])
    s_m = jnp.stack([s[5] for s in sts_s])
    return (y_prompt, y_sample, p_k, p_v, p_conv, p_C, p_n, p_m, s_k, s_v, s_conv, s_C, s_n, s_m)
```

```python
import functools
import math

import jax
import jax.numpy as jnp
from jax import lax
from jax.experimental import pallas as pl
from jax.experimental.pallas import tpu as pltpu

D_MODEL = 1024
HEAD_DIM_A = 64
N_HEADS_A = 6
D_A = N_HEADS_A * HEAD_DIM_A
D_B = D_MODEL - D_A
N_HEADS_B = 4
HEAD_DIM_B = D_B // N_HEADS_B
DILATED_BRANCHES = ((128, 1), (512, 4), (2048, 16))
WINDOW_MAX = 2048
N_BUCKETS = 32
REL_MAX_DIST = 2048
CONV_W = 4
MLSTM_CHUNK = 128
D_FF = 4 * D_MODEL
N_GATES = 2 * N_HEADS_B
D_IN = 3 * D_A + 2 * D_B + N_GATES
SPLITS = [D_A, 2 * D_A, 3 * D_A, 3 * D_A + D_B, 3 * D_A + 2 * D_B]
EPS = 1e-6
NEG = -1e30

LANES = 128
SUBLANES = 8
VMEM_LIMIT_BYTES = 56 * 1024 * 1024


def _round_up(n, m):
    return -(-n // m) * m


def _rms(xf, g):
    return xf * lax.rsqrt(jnp.mean(xf * xf, axis=-1, keepdims=True) + EPS) * g


def _norm_matmul_kernel(x_ref, g_ref, w_ref, z_ref):
    h = _rms(x_ref[...], g_ref[...])
    z_ref[...] = jnp.dot(h.astype(jnp.bfloat16), w_ref[...],
                         preferred_element_type=jnp.float32)


def _norm_matmul(x, g, w_bf16, tm):
    m, d = x.shape
    n = w_bf16.shape[1]
    assert m % tm == 0 and n % LANES == 0
    return pl.pallas_call(
        _norm_matmul_kernel,
        out_shape=jax.ShapeDtypeStruct((m, n), jnp.float32),
        grid=(m // tm,),
        in_specs=[
            pl.BlockSpec((tm, d), lambda i: (i, 0)),
            pl.BlockSpec((1, d), lambda i: (0, 0)),
            pl.BlockSpec((d, n), lambda i: (0, 0)),
        ],
        out_specs=pl.BlockSpec((tm, n), lambda i: (i, 0)),
        compiler_params=pltpu.CompilerParams(
            dimension_semantics=("arbitrary",),
            vmem_limit_bytes=VMEM_LIMIT_BYTES),
        name="norm_inproj",
    )(x, g, w_bf16)


def _out_ffn_kernel(x_ref, o_ref, wo_ref, g2_ref, w1_ref, w2_ref, gf_ref, y_ref, *, ff_chunk):
    x1 = x_ref[...] + jnp.dot(o_ref[...].astype(jnp.bfloat16), wo_ref[...],
                              preferred_element_type=jnp.float32)
    h2 = _rms(x1, g2_ref[...]).astype(jnp.bfloat16)
    acc = x1
    for c in range(D_FF // ff_chunk):
        u = jnp.dot(h2, w1_ref[:, c * ff_chunk:(c + 1) * ff_chunk],
                    preferred_element_type=jnp.float32)
        u = jnp.square(jnp.maximum(u, 0.0)).astype(jnp.bfloat16)
        acc = acc + jnp.dot(u, w2_ref[c * ff_chunk:(c + 1) * ff_chunk, :],
                            preferred_element_type=jnp.float32)
    y_ref[...] = _rms(acc, gf_ref[...])


def _out_ffn(x, o, wo, g2, w1, w2, gf, tm, ff_chunk=1024):
    m, d = x.shape
    ko = o.shape[1]
    const = lambda i: (0, 0)
    single = dict(pipeline_mode=pl.Buffered(1))
    return pl.pallas_call(
        functools.partial(_out_ffn_kernel, ff_chunk=ff_chunk),
        out_shape=jax.ShapeDtypeStruct((m, d), jnp.float32),
        grid=(m // tm,),
        in_specs=[
            pl.BlockSpec((tm, d), lambda i: (i, 0)),
            pl.BlockSpec((tm, ko), lambda i: (i, 0)),
            pl.BlockSpec((ko, d), const, **single),
            pl.BlockSpec((1, d), const),
            pl.BlockSpec((d, D_FF), const, **single),
            pl.BlockSpec((D_FF, d), const, **single),
            pl.BlockSpec((1, d), const),
        ],
        out_specs=pl.BlockSpec((tm, d), lambda i: (i, 0)),
        compiler_params=pltpu.CompilerParams(
            dimension_semantics=("arbitrary",),
            vmem_limit_bytes=VMEM_LIMIT_BYTES),
        name="outproj_ffn",
    )(x, o, wo, g2, w1, w2, gf)


def _rmsnorm(x, g):
    xf = x.astype(jnp.float32)
    y = xf * lax.rsqrt(jnp.mean(xf * xf, axis=-1, keepdims=True) + EPS)
    return (y * g.astype(jnp.float32)).astype(x.dtype)


def _t5_bucket(dist):
    max_exact = N_BUCKETS // 2
    df = jnp.maximum(dist, 1).astype(jnp.float32)
    large = max_exact + (jnp.log(df / max_exact) / math.log(REL_MAX_DIST / max_exact)
                         * (N_BUCKETS - max_exact)).astype(jnp.int32)
    large = jnp.minimum(large, N_BUCKETS - 1)
    return jnp.where(dist < max_exact, dist, large)


def _dilated_prompt(q, k, v, rel_bias, window, dil):
    B, S, H, E = q.shape
    nk = window // dil
    blk = nk
    unit = blk * dil
    s_pad = -(-S // unit) * unit
    nb = s_pad // unit
    padw = ((0, 0), (0, s_pad - S), (0, 0), (0, 0))

    def to_blocks(t):
        return jnp.pad(t, padw).reshape(B, nb, blk, dil, H, E)

    def with_prev(t):
        prev = jnp.pad(t, ((0, 0), (1, 0), (0, 0), (0, 0), (0, 0), (0, 0)))[:, :-1]
        return jnp.concatenate([prev, t], axis=2)

    qb = to_blocks(q)
    kc = with_prev(to_blocks(k))
    vc = with_prev(to_blocks(v))
    qi = jnp.arange(blk)[:, None]
    ki = jnp.arange(2 * blk)[None, :]
    j = qi + blk - ki
    band = (j >= 0) & (j <= nk)
    blk_idx = jnp.arange(nb)[:, None, None]
    mask = band[None] & (blk_idx * blk + ki[None] - blk >= 0)
    bias = rel_bias[_t5_bucket(jnp.clip(j, 0, None) * dil)].astype(jnp.float32)
    bias = bias.transpose(2, 0, 1)
    scale = 1.0 / math.sqrt(E)
    s = jnp.einsum('bnqrhe,bnkrhe->bnrhqk', qb, kc).astype(jnp.float32) * scale + bias
    s = jnp.where(mask[None, :, None, None], s, NEG)
    lse = jax.nn.logsumexp(s, axis=-1)
    p = jnp.exp(s - lse[..., None]).astype(v.dtype)
    o = jnp.einsum('bnrhqk,bnkrhe->bnqrhe', p, vc).reshape(B, s_pad, H, E)[:, :S]
    lse = lse.transpose(0, 1, 4, 2, 3).reshape(B, s_pad, H)[:, :S]
    return o, lse


def _dilated_step(q, k_all, v_all, rel_bias, window, dil):
    B, T, H, E = q.shape
    P = k_all.shape[1] - T
    nk = window // dil
    j = jnp.arange(nk + 1)
    idx = P + jnp.arange(T)[:, None] - j[None, :] * dil
    valid = idx >= 0
    idx = jnp.maximum(idx, 0)
    kg = k_all[:, idx]
    vg = v_all[:, idx]
    bias = rel_bias[_t5_bucket(j * dil)].astype(jnp.float32).T
    scale = 1.0 / math.sqrt(E)
    s = jnp.einsum('bthe,btjhe->bthj', q, kg).astype(jnp.float32) * scale + bias
    s = jnp.where(valid[None, :, None, :], s, NEG)
    lse = jax.nn.logsumexp(s, axis=-1)
    p = jnp.exp(s - lse[..., None]).astype(v_all.dtype)
    o = jnp.einsum('bthj,btjhe->bthe', p, vg)
    return o, lse


def _combine_branches(branches):
    o = jnp.stack([b[0] for b in branches])
    lse = jnp.stack([b[1] for b in branches])
    w = jax.nn.softmax(lse, axis=0).astype(o.dtype)
    return jnp.einsum('gbsh,gbshe->bshe', w, o)


def _causal_conv(x, buf, w, b):
    S = x.shape[1]
    xp = jnp.concatenate([buf.astype(x.dtype), x], axis=1)
    y = b + sum(xp[:, i:i + S] * w[i] for i in range(CONV_W))
    return y, xp[:, -(CONV_W - 1):]


def _mlstm_cell(q, k, v, i_pre, f_pre, C0, n0, m0):
    B, S, H, E = q.shape
    L = min(MLSTM_CHUNK, S)
    s_pad = -(-S // L) * L
    nc = s_pad // L
    pad = s_pad - S
    f32 = jnp.float32

    def chunks4(t):
        t = jnp.pad(t.astype(f32), ((0, 0), (0, pad), (0, 0), (0, 0)))
        return t.reshape(B, nc, L, H, E).transpose(1, 0, 3, 2, 4)

    def chunks3(t, fill):
        t = jnp.pad(t.astype(f32), ((0, 0), (0, pad), (0, 0)), constant_values=fill)
        return t.reshape(B, nc, L, H).transpose(1, 0, 3, 2)

    logf = jax.nn.log_sigmoid(f_pre.astype(f32))
    xs = (chunks4(q), chunks4(k), chunks4(v), chunks3(i_pre, NEG), chunks3(logf, 0.0))
    causal = jnp.tril(jnp.ones((L, L), dtype=bool))

    def step(carry, inp):
        C, n, m = carry
        qc, kc, vc, ic, fc = inp
        a = jnp.cumsum(fc, axis=-1)
        g = a + m[..., None]
        D = jnp.where(causal, a[..., :, None] - a[..., None, :] + ic[..., None, :], NEG)
        m_t = jnp.maximum(g, jnp.max(D, axis=-1))
        w_state = jnp.exp(g - m_t)
        A = jnp.einsum('bhte,bhse->bhts', qc, kc) * jnp.exp(D - m_t[..., None])
        num = (w_state[..., None] * jnp.einsum('bhvk,bhtk->bhtv', C, qc)
               + jnp.einsum('bhts,bhsv->bhtv', A, vc))
        den = w_state * jnp.einsum('bhk,bhtk->bht', n, qc) + jnp.sum(A, axis=-1)
        h = num / jnp.maximum(jnp.abs(den), jnp.exp(-m_t))[..., None]
        b_tot = a[..., -1]
        wl = b_tot[..., None] - a + ic
        m_new = jnp.maximum(b_tot + m, jnp.max(wl, axis=-1))
        wk = jnp.exp(wl - m_new[..., None])
        decay = jnp.exp(b_tot + m - m_new)
        C_new = decay[..., None, None] * C + jnp.einsum('bhs,bhsv,bhsk->bhvk', wk, vc, kc)
        n_new = decay[..., None] * n + jnp.einsum('bhs,bhsk->bhk', wk, kc)
        return (C_new, n_new, m_new), h

    (C, n, m), hs = lax.scan(step, (C0.astype(f32), n0.astype(f32), m0.astype(f32)), xs)
    h = hs.transpose(1, 0, 3, 2, 4).reshape(B, s_pad, H, E)[:, :S]
    return h, C, n, m


def _mixers(x, z, win_k, win_v, conv_buf, C0, n0, m0, rel_bias, gate_bias, conv_w, conv_b,
            wq_head, wk_head, attn_out_g, mh_norm_g, skip):
    B, S, _ = x.shape
    qa, ka, va, xb, ob, gates = jnp.split(z, SPLITS, axis=-1)
    qa = qa.reshape(B, S, N_HEADS_A, HEAD_DIM_A)
    ka = ka.reshape(B, S, N_HEADS_A, HEAD_DIM_A)
    va = va.reshape(B, S, N_HEADS_A, HEAD_DIM_A)
    if win_k is None:
        branches = [_dilated_prompt(qa, ka, va, rel_bias, w, d) for (w, d) in DILATED_BRANCHES]
        P = min(WINDOW_MAX, S)
        new_k = ka[:, S - P:]
        new_v = va[:, S - P:]
    else:
        P = win_k.shape[1]
        k_all = jnp.concatenate([win_k.astype(ka.dtype), ka], axis=1)
        v_all = jnp.concatenate([win_v.astype(va.dtype), va], axis=1)
        branches = [_dilated_step(qa, k_all, v_all, rel_bias, w, d) for (w, d) in DILATED_BRANCHES]
        new_k = k_all[:, -P:]
        new_v = v_all[:, -P:]
    out_a = _rmsnorm(_combine_branches(branches).reshape(B, S, D_A), attn_out_g)

    c, new_conv = _causal_conv(xb, conv_buf, conv_w, conv_b)
    c_act = jax.nn.silu(c)
    c_h = c_act.reshape(B, S, N_HEADS_B, HEAD_DIM_B)
    qb = jnp.einsum('bshd,hde->bshe', c_h, wq_head)
    kb = jnp.einsum('bshd,hde->bshe', c_h, wk_head) * (1.0 / math.sqrt(HEAD_DIM_B))
    vb = xb.reshape(B, S, N_HEADS_B, HEAD_DIM_B)
    gates = gates + gate_bias
    i_pre, f_pre = gates[..., :N_HEADS_B], gates[..., N_HEADS_B:]
    hb, C, n, m = _mlstm_cell(qb, kb, vb, i_pre, f_pre, C0, n0, m0)
    hb = hb * lax.rsqrt(jnp.mean(hb * hb, axis=-1, keepdims=True) + EPS)
    hb = hb.reshape(B, S, D_B) * mh_norm_g.astype(jnp.float32)
    out_b = jax.nn.sigmoid(ob) * (hb.astype(x.dtype) + skip * c_act)
    return jnp.concatenate([out_a, out_b], axis=-1), (new_k, new_v, new_conv, C, n, m)


def _layer(x, win_k, win_v, conv_buf, C0, n0, m0, rel_bias, norm1_g, w_in_bf, gate_bias, conv_w,
           conv_b, wq_head, wk_head, attn_out_g, mh_norm_g, skip, w_out_bf, norm2_g, w_ff1_bf,
           w_ff2_bf, final_g, tm):
    B, S, _ = x.shape
    x2 = x.reshape(B * S, D_MODEL)
    z = _norm_matmul(x2, norm1_g.reshape(1, D_MODEL), w_in_bf, tm)[:, :D_IN]
    o, st = _mixers(x, z.reshape(B, S, D_IN), win_k, win_v, conv_buf, C0, n0, m0, rel_bias,
                    gate_bias, conv_w, conv_b, wq_head, wk_head, attn_out_g, mh_norm_g, skip)
    y = _out_ffn(x2, o.reshape(B * S, D_MODEL), w_out_bf, norm2_g.reshape(1, D_MODEL), w_ff1_bf,
                 w_ff2_bf, final_g.reshape(1, D_MODEL), tm)
    return y.reshape(B, S, D_MODEL), st


def kernel(x_prompt, x_sample, cache_win_k, cache_win_v, state_conv, state_C, state_n, state_m,
           rel_bias, norm1_g, w_in, gate_bias, conv_w, conv_b, wq_head, wk_head, attn_out_g,
           mh_norm_g, skip, w_out, norm2_g, w_ff1, w_ff2, final_g):
    bf16 = jnp.bfloat16
    Bp = x_prompt.shape[0]
    n_in = _round_up(D_IN, LANES)
    w_in_bf = jnp.pad(w_in[0], ((0, 0), (0, n_in - D_IN))).astype(bf16)
    lp = (rel_bias, norm1_g[0], w_in_bf, gate_bias[0], conv_w[0], conv_b[0], wq_head[0], wk_head[0],
          attn_out_g[0], mh_norm_g[0], skip[0], w_out[0].astype(bf16), norm2_g[0],
          w_ff1[0].astype(bf16), w_ff2[0].astype(bf16), final_g)
    y_p, st_p = _layer(
        x_prompt, None, None,
        jnp.zeros((Bp, CONV_W - 1, D_B), x_prompt.dtype),
        jnp.zeros((Bp, N_HEADS_B, HEAD_DIM_B, HEAD_DIM_B), jnp.float32),
        jnp.zeros((Bp, N_HEADS_B, HEAD_DIM_B), jnp.float32),
        jnp.zeros((Bp, N_HEADS_B), jnp.float32),
        *lp, tm=512)
    y_s, st_s = _layer(
        x_sample, cache_win_k[0], cache_win_v[0], state_conv[0], state_C[0], state_n[0], state_m[0],
        *lp, tm=128)
    return (y_p, y_s) + tuple(s[None] for s in st_p) + tuple(s[None] for s in st_s)
```

```python
import functools
import math

import jax
import jax.numpy as jnp
from jax import lax
from jax.experimental import pallas as pl
from jax.experimental.pallas import tpu as pltpu

D_MODEL = 1024
HEAD_DIM_A = 64
N_HEADS_A = 6
D_A = N_HEADS_A * HEAD_DIM_A
D_B = D_MODEL - D_A
N_HEADS_B = 4
HEAD_DIM_B = D_B // N_HEADS_B
DILATED_BRANCHES = ((128, 1), (512, 4), (2048, 16))
WINDOW_MAX = 2048
N_BUCKETS = 32
REL_MAX_DIST = 2048
CONV_W = 4
MLSTM_CHUNK = 128
D_FF = 4 * D_MODEL
N_GATES = 2 * N_HEADS_B
D_IN = 3 * D_A + 2 * D_B + N_GATES
SPLITS = [D_A, 2 * D_A, 3 * D_A, 3 * D_A + D_B, 3 * D_A + 2 * D_B]
EPS = 1e-6
NEG = -1e30

LANES = 128
SUBLANES = 8
VMEM_LIMIT_BYTES = 56 * 1024 * 1024

N_PAIRS = D_A // LANES
N_SLABS = 3 * N_PAIRS
HEAD_PAD = 2 * LANES
D_B_PAD = N_HEADS_B * HEAD_PAD
D_IN_PAD = 3 * D_A + 2 * D_B_PAD + LANES
D_O_PAD = D_A + LANES + D_B_PAD
BLK = 128
Q_SUPER = BLK * DILATED_BRANCHES[-1][1]

f32 = jnp.float32
bf16 = jnp.bfloat16


def _round_up(n, m):
    return -(-n // m) * m


def _rms(xf, g):
    return xf * lax.rsqrt(jnp.mean(xf * xf, axis=-1, keepdims=True) + EPS) * g


def _pad_heads(t):
    t = t.reshape(t.shape[:-1] + (N_HEADS_B, HEAD_DIM_B))
    t = jnp.pad(t, [(0, 0)] * (t.ndim - 1) + [(0, HEAD_PAD - HEAD_DIM_B)])
    return t.reshape(t.shape[:-2] + (D_B_PAD,))


def _unpad_heads(t):
    t = t.reshape(t.shape[:-1] + (N_HEADS_B, HEAD_PAD))[..., :HEAD_DIM_B]
    return t.reshape(t.shape[:-2] + (D_B,))


def _norm_inproj_kernel(x_ref, g_ref, w_ref, slab_ref, xb_ref, ob_ref, gt_ref):
    h = _rms(x_ref[...], g_ref[...]).astype(bf16)
    qkv = jnp.dot(h, w_ref[:, 0:3 * D_A], preferred_element_type=f32)
    for j in range(N_SLABS):
        slab_ref[j] = qkv[:, j * LANES:(j + 1) * LANES]
    o = 3 * D_A
    xb_ref[...] = jnp.dot(h, w_ref[:, o:o + D_B_PAD], preferred_element_type=f32)
    o += D_B_PAD
    ob_ref[...] = jnp.dot(h, w_ref[:, o:o + D_B_PAD], preferred_element_type=f32)
    o += D_B_PAD
    gt_ref[...] = jnp.dot(h, w_ref[:, o:o + LANES], preferred_element_type=f32)


def _norm_inproj(x, g, w_pad_bf16, tm):
    m, d = x.shape
    assert m % tm == 0
    row = lambda w: pl.BlockSpec((tm, w), lambda i: (i, 0))
    return pl.pallas_call(
        _norm_inproj_kernel,
        out_shape=(
            jax.ShapeDtypeStruct((N_SLABS, m, LANES), f32),
            jax.ShapeDtypeStruct((m, D_B_PAD), f32),
            jax.ShapeDtypeStruct((m, D_B_PAD), f32),
            jax.ShapeDtypeStruct((m, LANES), f32),
        ),
        grid=(m // tm,),
        in_specs=[
            row(d),
            pl.BlockSpec((1, d), lambda i: (0, 0)),
            pl.BlockSpec((d, D_IN_PAD), lambda i: (0, 0), pipeline_mode=pl.Buffered(1)),
        ],
        out_specs=(
            pl.BlockSpec((N_SLABS, tm, LANES), lambda i: (0, i, 0)),
            row(D_B_PAD), row(D_B_PAD), row(LANES),
        ),
        compiler_params=pltpu.CompilerParams(
            dimension_semantics=("arbitrary",),
            vmem_limit_bytes=VMEM_LIMIT_BYTES),
        name="norm_inproj",
    )(x, g, w_pad_bf16)


def _pad_w_in(w_in):
    qkv = w_in[:, :3 * D_A]
    xb = _pad_heads(w_in[:, SPLITS[2]:SPLITS[3]])
    ob = _pad_heads(w_in[:, SPLITS[3]:SPLITS[4]])
    gt = jnp.pad(w_in[:, SPLITS[4]:], ((0, 0), (0, LANES - N_GATES)))
    return jnp.concatenate([qkv, xb, ob, gt], axis=1)


def _out_ffn_kernel(x_ref, oa_ref, ob_ref, wo_ref, g2_ref, w1_ref, w2_ref, gf_ref, y_ref, *, ff_chunk):
    o = jnp.concatenate([oa_ref[...].astype(bf16), jnp.zeros((oa_ref.shape[0], LANES), bf16),
                         ob_ref[...].astype(bf16)], axis=1)
    x1 = x_ref[...] + jnp.dot(o, wo_ref[...], preferred_element_type=f32)
    h2 = _rms(x1, g2_ref[...]).astype(bf16)
    acc = x1
    for c in range(D_FF // ff_chunk):
        u = jnp.dot(h2, w1_ref[:, c * ff_chunk:(c + 1) * ff_chunk], preferred_element_type=f32)
        u = jnp.square(jnp.maximum(u, 0.0)).astype(bf16)
        acc = acc + jnp.dot(u, w2_ref[c * ff_chunk:(c + 1) * ff_chunk, :],
                            preferred_element_type=f32)
    y_ref[...] = _rms(acc, gf_ref[...])


def _out_ffn(x, oa, ob, wo, g2, w1, w2, gf, tm, ff_chunk=1024):
    m, d = x.shape
    assert wo.shape[0] == D_O_PAD
    const = lambda i: (0, 0)
    single = dict(pipeline_mode=pl.Buffered(1))
    row = lambda w: pl.BlockSpec((tm, w), lambda i: (i, 0))
    return pl.pallas_call(
        functools.partial(_out_ffn_kernel, ff_chunk=ff_chunk),
        out_shape=jax.ShapeDtypeStruct((m, d), f32),
        grid=(m // tm,),
        in_specs=[
            row(d), row(D_A), row(D_B_PAD),
            pl.BlockSpec((D_O_PAD, d), const, **single),
            pl.BlockSpec((1, d), const),
            pl.BlockSpec((d, D_FF), const, **single),
            pl.BlockSpec((D_FF, d), const, **single),
            pl.BlockSpec((1, d), const),
        ],
        out_specs=row(d),
        compiler_params=pltpu.CompilerParams(
            dimension_semantics=("arbitrary",),
            vmem_limit_bytes=VMEM_LIMIT_BYTES),
        name="outproj_ffn",
    )(x, oa, ob, wo, g2, w1, w2, gf)


def _t5_bucket(dist):
    max_exact = N_BUCKETS // 2
    df = jnp.maximum(dist, 1).astype(jnp.float32)
    large = max_exact + (jnp.log(df / max_exact) / math.log(REL_MAX_DIST / max_exact)
                         * (N_BUCKETS - max_exact)).astype(jnp.int32)
    large = jnp.minimum(large, N_BUCKETS - 1)
    return jnp.where(dist < max_exact, dist, large)


def _prompt_bias_table(rel_bias):
    qi = jnp.arange(BLK)[:, None]
    ki = jnp.arange(2 * BLK)[None, :]
    j = qi + BLK - ki
    band = (j >= 0) & (j <= BLK)
    tabs = []
    for (_, d) in DILATED_BRANCHES:
        bias = rel_bias[_t5_bucket(jnp.clip(j, 0, None) * d)].astype(f32)
        bias = jnp.where(band[:, :, None], bias, NEG).transpose(2, 0, 1)
        tabs.append(bias.reshape(N_PAIRS, 2 * BLK, 2 * BLK))
    return jnp.stack(tabs)


def _attn_prompt_kernel(bias_ref, q_ref, kp_ref, kc_ref, vp_ref, vc_ref, g_ref, o_ref,
                        m_scr, l_scr, acc_scr):
    n = pl.program_id(1)
    lane = lax.broadcasted_iota(jnp.int32, (BLK, LANES), 1)
    low = lane < HEAD_DIM_A
    kcol = lax.broadcasted_iota(jnp.int32, (2 * BLK, 2 * BLK), 1)
    first_extra = jnp.where((kcol < BLK) & (n == 0), NEG, 0.0).astype(f32)

    def job(g, d, ka_ref, va_ref, row_a, row_b, q_start, extra):
        def ld(ref, p, start):
            if d == 1:
                return ref[p, pl.ds(start, BLK), :]
            return ref.at[p][pl.ds(start, BLK, stride=d), :]

        def st(ref, p, val):
            if d == 1:
                ref[p, pl.ds(q_start, BLK), :] = val
            else:
                ref.at[p][pl.ds(q_start, BLK, stride=d), :] = val

        for p in range(N_PAIRS):
            q2 = ld(q_ref, p, q_start) * (1.0 / math.sqrt(HEAD_DIM_A))
            qcat = jnp.concatenate([jnp.where(low, q2, 0.0), jnp.where(low, 0.0, q2)], axis=0)
            kk = jnp.concatenate([ld(ka_ref, p, row_a), ld(kc_ref, p, row_b)], axis=0)
            vv = jnp.concatenate([ld(va_ref, p, row_a), ld(vc_ref, p, row_b)], axis=0)
            s = lax.dot_general(qcat.astype(bf16), kk.astype(bf16), (((1,), (1,)), ((), ())),
                                preferred_element_type=f32)
            s = s + bias_ref[g, p]
            if extra is not None:
                s = s + extra
            m = jnp.max(s, axis=-1, keepdims=True)
            e = jnp.exp(s - m)
            l = jnp.sum(e, axis=-1, keepdims=True)
            o = jnp.dot(e.astype(bf16), vv.astype(bf16), preferred_element_type=f32)
            o_t = jnp.where(low, o[:BLK], o[BLK:])
            m_t = jnp.where(low, m[:BLK], m[BLK:])
            l_t = jnp.where(low, l[:BLK], l[BLK:])
            if g == 0:
                st(acc_scr, p, o_t)
                st(m_scr, p, m_t)
                st(l_scr, p, l_t)
            else:
                m_o = ld(m_scr, p, q_start)
                m_n = jnp.maximum(m_o, m_t)
                al = jnp.exp(m_o - m_n)
                be = jnp.exp(m_t - m_n)
                st(acc_scr, p, al * ld(acc_scr, p, q_start) + be * o_t)
                st(l_scr, p, al * ld(l_scr, p, q_start) + be * l_t)
                st(m_scr, p, m_n)

    for g, (_, d) in enumerate(DILATED_BRANCHES):
        unit = BLK * d
        nu = Q_SUPER // unit
        log_d = d.bit_length() - 1

        def first(r, carry, g=g, d=d, unit=unit, nu=nu):
            job(g, d, kp_ref, vp_ref, (nu - 1) * unit + r, r, r, first_extra)
            return carry

        def rest(idx, carry, g=g, d=d, unit=unit, log_d=log_d):
            u = 1 + lax.shift_right_logical(idx, log_d)
            r = jnp.bitwise_and(idx, d - 1)
            row_b = u * unit + r
            job(g, d, kc_ref, vc_ref, row_b - unit, row_b, row_b, None)
            return carry

        if d == 1:
            first(0, 0)
        else:
            lax.fori_loop(0, d, first, 0)
        if nu > 1:
            lax.fori_loop(0, (nu - 1) * d, rest, 0)

    rc = 2 * BLK

    def fin(c, carry):
        rows = pl.ds(pl.multiple_of(c * rc, rc), rc)
        os_ = [acc_scr[p, rows, :] / l_scr[p, rows, :] for p in range(N_PAIRS)]
        ss = sum(jnp.sum(o * o, axis=-1, keepdims=True) for o in os_)
        sc = lax.rsqrt(ss * (1.0 / D_A) + EPS)
        for p in range(N_PAIRS):
            o_ref[rows, p * LANES:(p + 1) * LANES] = os_[p] * sc * g_ref[:, p * LANES:(p + 1) * LANES]
        return carry

    lax.fori_loop(0, Q_SUPER // rc, fin, 0)


def _attn_prompt(slabs, rel_bias, attn_out_g, batch, seq):
    assert seq % Q_SUPER == 0
    nsb = seq // Q_SUPER
    nbr = len(DILATED_BRANCHES)

    def cur(bi, ni):
        return bi * nsb + ni

    def prev(bi, ni):
        return bi * nsb + jnp.maximum(ni - 1, 0)

    slab = lambda grp, rowf: pl.BlockSpec((N_PAIRS, Q_SUPER, LANES),
                                          lambda bi, ni: (grp, rowf(bi, ni), 0))
    return pl.pallas_call(
        _attn_prompt_kernel,
        out_shape=jax.ShapeDtypeStruct((batch * seq, D_A), f32),
        grid=(batch, nsb),
        in_specs=[
            pl.BlockSpec((nbr, N_PAIRS, 2 * BLK, 2 * BLK), lambda bi, ni: (0, 0, 0, 0),
                         pipeline_mode=pl.Buffered(1)),
            slab(0, cur), slab(1, prev), slab(1, cur), slab(2, prev), slab(2, cur),
            pl.BlockSpec((1, D_A), lambda bi, ni: (0, 0)),
        ],
        out_specs=pl.BlockSpec((Q_SUPER, D_A), lambda bi, ni: (cur(bi, ni), 0)),
        scratch_shapes=[
            pltpu.VMEM((N_PAIRS, Q_SUPER, LANES), f32),
            pltpu.VMEM((N_PAIRS, Q_SUPER, LANES), f32),
            pltpu.VMEM((N_PAIRS, Q_SUPER, LANES), f32),
        ],
        compiler_params=pltpu.CompilerParams(
            dimension_semantics=("arbitrary", "arbitrary"),
            vmem_limit_bytes=VMEM_LIMIT_BYTES),
        name="attn_prompt",
    )(_prompt_bias_table(rel_bias), slabs, slabs, slabs, slabs, slabs, attn_out_g.reshape(1, D_A))


def _log_sigmoid(x):
    return -(jnp.maximum(-x, 0.0) + jnp.log1p(jnp.exp(-jnp.abs(x))))


def _mlstm_prompt_kernel(xb_ref, ob_ref, gt_ref, gbias_ref, cw_ref, cb_ref, wq_ref, wk_ref, mg_ref,
                         skip_ref, out_ref, c_out_ref, n_out_ref, m_out_ref, conv_out_ref,
                         conv_scr, c_scr, n_scr, m_scr):
    c_idx = pl.program_id(1)
    L = MLSTM_CHUNK

    @pl.when(c_idx == 0)
    def _init():
        conv_scr[0:SUBLANES, :] = jnp.zeros((SUBLANES, D_B_PAD), f32)
        c_scr[...] = jnp.zeros_like(c_scr)
        n_scr[...] = jnp.zeros_like(n_scr)
        m_scr[...] = jnp.zeros_like(m_scr)

    x = xb_ref[...]
    conv_scr[SUBLANES:SUBLANES + L, :] = x
    c = cb_ref[...] + x * cw_ref[CONV_W - 1:CONV_W, :]
    for i in range(CONV_W - 1):
        sh = CONV_W - 1 - i
        c = c + conv_scr[SUBLANES - sh:SUBLANES - sh + L, :] * cw_ref[i:i + 1, :]
    conv_scr[0:SUBLANES, :] = x[L - SUBLANES:, :]
    c_act = c * jax.nn.sigmoid(c)

    gts = gt_ref[...] + gbias_ref[...]
    logf = _log_sigmoid(gts)
    row = lax.broadcasted_iota(jnp.int32, (L, L), 0)
    col = lax.broadcasted_iota(jnp.int32, (L, L), 1)
    causal = row >= col
    a_all = jnp.dot(causal.astype(f32), logf, precision=lax.Precision.HIGHEST,
                    preferred_element_type=f32)
    gts_t = gts.T
    a_t = a_all.T

    for h in range(N_HEADS_B):
        sl = slice(h * HEAD_PAD, (h + 1) * HEAD_PAD)
        ch = c_act[:, sl]
        ch_bf = ch.astype(bf16)
        q = jnp.dot(ch_bf, wq_ref[h], preferred_element_type=f32)
        k = jnp.dot(ch_bf, wk_ref[h], preferred_element_type=f32) * (1.0 / math.sqrt(HEAD_DIM_B))
        v = x[:, sl]
        q_bf, k_bf = q.astype(bf16), k.astype(bf16)
        a_col = a_all[:, N_HEADS_B + h:N_HEADS_B + h + 1]
        i_col = gts[:, h:h + 1]
        a_row = a_t[N_HEADS_B + h:N_HEADS_B + h + 1, :]
        i_row = gts_t[h:h + 1, :]
        m_prev = m_scr[h]
        dmat = jnp.where(causal, a_col - a_row + i_row, NEG)
        g = a_col + m_prev
        m_t = jnp.maximum(g, jnp.max(dmat, axis=-1, keepdims=True))
        w_state = jnp.exp(g - m_t)
        s = lax.dot_general(q_bf, k_bf, (((1,), (1,)), ((), ())), preferred_element_type=f32)
        amat = s * jnp.exp(dmat - m_t)
        c_state = c_scr[h]
        inter = lax.dot_general(q_bf, c_state.astype(bf16), (((1,), (1,)), ((), ())),
                                preferred_element_type=f32)
        num = w_state * inter + jnp.dot(amat.astype(bf16), v.astype(bf16),
                                        preferred_element_type=f32)
        n_row = n_scr[h]
        den = (w_state * jnp.sum(q * n_row, axis=-1, keepdims=True)
               + jnp.sum(amat, axis=-1, keepdims=True))
        hh = num / jnp.maximum(jnp.abs(den), jnp.exp(-m_t))
        hn = hh * lax.rsqrt(jnp.sum(hh * hh, axis=-1, keepdims=True) * (1.0 / HEAD_DIM_B) + EPS)
        hn = hn * mg_ref[:, sl]
        out_ref[:, sl] = jax.nn.sigmoid(ob_ref[:, sl]) * (hn + skip_ref[:, sl] * ch)

        b_tot = a_col[L - 1:L, :]
        wl = b_tot - a_col + i_col
        m_new = jnp.maximum(b_tot + m_prev, jnp.max(wl, axis=0, keepdims=True))
        wk = jnp.exp(wl - m_new)
        decay = jnp.exp(b_tot + m_prev - m_new)
        upd = lax.dot_general((wk * v).astype(bf16), k_bf, (((0,), (0,)), ((), ())),
                              preferred_element_type=f32)
        c_scr[h] = decay * c_state + upd
        n_scr[h] = decay * n_row + jnp.sum(wk * k, axis=0, keepdims=True)
        m_scr[h] = m_new

    @pl.when(c_idx == pl.num_programs(1) - 1)
    def _final():
        c_out_ref[0] = c_scr[...]
        n_out_ref[0] = n_scr[...]
        m_out_ref[0] = m_scr[...]
        conv_out_ref[0] = x[L - SUBLANES:, :]


def _mlstm_prompt(xb, ob, gates, gbias, cw, cb, wq, wk, mg, skip, batch, seq):
    L = MLSTM_CHUNK
    assert seq % L == 0
    nc = seq // L
    rows = lambda w: pl.BlockSpec((L, w), lambda bi, ci: (bi * nc + ci, 0))
    const2 = lambda shape: pl.BlockSpec(shape, lambda bi, ci: (0, 0))
    const3 = lambda shape: pl.BlockSpec(shape, lambda bi, ci: (0, 0, 0))
    state = lambda shape: pl.BlockSpec((1,) + shape, lambda bi, ci: (bi,) + (0,) * len(shape))
    return pl.pallas_call(
        _mlstm_prompt_kernel,
        out_shape=(
            jax.ShapeDtypeStruct((batch * seq, D_B_PAD), f32),
            jax.ShapeDtypeStruct((batch, N_HEADS_B, HEAD_PAD, HEAD_PAD), f32),
            jax.ShapeDtypeStruct((batch, N_HEADS_B, 1, HEAD_PAD), f32),
            jax.ShapeDtypeStruct((batch, N_HEADS_B, 1, 1), f32),
            jax.ShapeDtypeStruct((batch, SUBLANES, D_B_PAD), f32),
        ),
        grid=(batch, nc),
        in_specs=[rows(D_B_PAD), rows(D_B_PAD), rows(LANES), const2((1, LANES)),
                  const2((CONV_W, D_B_PAD)), const2((1, D_B_PAD)),
                  const3((N_HEADS_B, HEAD_PAD, HEAD_PAD)), const3((N_HEADS_B, HEAD_PAD, HEAD_PAD)),
                  const2((1, D_B_PAD)), const2((1, D_B_PAD))],
        out_specs=(rows(D_B_PAD), state((N_HEADS_B, HEAD_PAD, HEAD_PAD)),
                   state((N_HEADS_B, 1, HEAD_PAD)), state((N_HEADS_B, 1, 1)),
                   state((SUBLANES, D_B_PAD))),
        scratch_shapes=[
            pltpu.VMEM((SUBLANES + L, D_B_PAD), f32),
            pltpu.VMEM((N_HEADS_B, HEAD_PAD, HEAD_PAD), f32),
            pltpu.VMEM((N_HEADS_B, 1, HEAD_PAD), f32),
            pltpu.VMEM((N_HEADS_B, 1, 1), f32),
        ],
        compiler_params=pltpu.CompilerParams(
            dimension_semantics=("arbitrary", "arbitrary"),
            vmem_limit_bytes=VMEM_LIMIT_BYTES),
        name="mlstm_prompt",
    )(xb, ob, gates, gbias, cw, cb, wq, wk, mg, skip)


ATT_BT = 4
ROLL_CHUNKS = 8


def _step_bias_table(rel_bias):
    dist = jnp.concatenate([BLK - jnp.arange(BLK), jnp.zeros((1,), jnp.int32)])
    t = jnp.stack([rel_bias[_t5_bucket(dist * d)].astype(f32) for (_, d) in DILATED_BRANCHES])
    return t[..., None]


def _roll_copies(ck_any, cv_any, kn_any, vn_any, nk_any, nv_any, sem):
    nb, p = ck_any.shape[0], ck_any.shape[1]
    per = nb // ROLL_CHUNKS
    copies = []
    for t, (src, new, dst) in enumerate(((ck_any, kn_any, nk_any), (cv_any, vn_any, nv_any))):
        for c in range(ROLL_CHUNKS):
            bs = pl.ds(c * per, per)
            copies.append(pltpu.make_async_copy(src.at[bs, pl.ds(1, p - 1)],
                                                dst.at[0, bs, pl.ds(0, p - 1)], sem.at[t, c]))
        copies.append(pltpu.make_async_copy(new, dst.at[0, :, pl.ds(p - 1, 1)], sem.at[t, ROLL_CHUNKS]))
    return copies


def _attn_step_kernel(bias_ref, q_ref, kn_ref, vn_ref, g_ref,
                      k1_ref, k4_ref, k16_ref, v1_ref, v4_ref, v16_ref,
                      ck_any, cv_any, kn_any, vn_any,
                      o_ref, nk_any, nv_any, sem):
    i = pl.program_id(0)
    copies = _roll_copies(ck_any, cv_any, kn_any, vn_any, nk_any, nv_any, sem)

    @pl.when(i == 0)
    def _start():
        for cp in copies:
            cp.start()

    scale = 1.0 / math.sqrt(HEAD_DIM_A)
    k_refs = (k1_ref, k4_ref, k16_ref)
    v_refs = (v1_ref, v4_ref, v16_ref)
    nbr = len(DILATED_BRANCHES)
    for bi in range(ATT_BT):
        q = q_ref[bi]
        s_new = jnp.sum(kn_ref[bi] * q, axis=-1, keepdims=True) * scale
        ss, s0s = [], []
        for g in range(nbr):
            kk = k_refs[g][bi]
            s = jnp.sum(kk * q[None], axis=-1, keepdims=True) * scale
            ss.append(s + bias_ref[g, 0:BLK])
            s0s.append(s_new + bias_ref[g, BLK])
        m = functools.reduce(jnp.maximum, [jnp.max(s, axis=0) for s in ss] + s0s)
        l = jnp.zeros((N_HEADS_A, 1), f32)
        acc = jnp.zeros((N_HEADS_A, HEAD_DIM_A), f32)
        vn = vn_ref[bi]
        for g in range(nbr):
            e = jnp.exp(ss[g] - m[None])
            e0 = jnp.exp(s0s[g] - m)
            l = l + jnp.sum(e, axis=0) + e0
            acc = acc + jnp.sum(e * v_refs[g][bi], axis=0) + e0 * vn
        o = acc / l
        ssq = jnp.sum(jnp.sum(o * o, axis=-1, keepdims=True), axis=0, keepdims=True)
        o_ref[bi] = o * lax.rsqrt(ssq * (1.0 / D_A) + EPS) * g_ref[...]

    @pl.when(i == pl.num_programs(0) - 1)
    def _finish():
        for cp in copies:
            cp.wait()


def _attn_step(q3, kn3, vn3, cache_k, cache_v, rel_bias, attn_out_g):
    nb, p = cache_k.shape[0], cache_k.shape[1]
    assert nb % ATT_BT == 0 and nb % ROLL_CHUNKS == 0
    nbr = len(DILATED_BRANCHES)
    tile3 = pl.BlockSpec((ATT_BT, N_HEADS_A, HEAD_DIM_A), lambda i: (i, 0, 0))
    views, specs = [], []
    for cache in (cache_k, cache_v):
        for (w, d) in DILATED_BRANCHES:
            assert p % d == 0 and (p // d) % BLK == 0 and w // d == BLK
            views.append(cache.reshape(nb, p // d, d, N_HEADS_A, HEAD_DIM_A))
            last = (p // d) // BLK - 1
            specs.append(pl.BlockSpec((ATT_BT, BLK, None, N_HEADS_A, HEAD_DIM_A),
                                      lambda i, last=last: (i, last, 0, 0, 0)))
    any_spec = pl.BlockSpec(memory_space=pl.ANY)
    kn4 = kn3.reshape(nb, 1, N_HEADS_A, HEAD_DIM_A)
    vn4 = vn3.reshape(nb, 1, N_HEADS_A, HEAD_DIM_A)
    buf = jax.ShapeDtypeStruct((1, nb, p, N_HEADS_A, HEAD_DIM_A), f32)
    return pl.pallas_call(
        _attn_step_kernel,
        out_shape=(jax.ShapeDtypeStruct((nb, N_HEADS_A, HEAD_DIM_A), f32), buf, buf),
        grid=(nb // ATT_BT,),
        in_specs=[
            pl.BlockSpec((nbr, BLK + 1, N_HEADS_A, 1), lambda i: (0, 0, 0, 0)),
            tile3, tile3, tile3,
            pl.BlockSpec((N_HEADS_A, HEAD_DIM_A), lambda i: (0, 0)),
            *specs,
            any_spec, any_spec, any_spec, any_spec,
        ],
        out_specs=(tile3, any_spec, any_spec),
        scratch_shapes=[pltpu.SemaphoreType.DMA((2, ROLL_CHUNKS + 1))],
        compiler_params=pltpu.CompilerParams(
            dimension_semantics=("arbitrary",),
            vmem_limit_bytes=VMEM_LIMIT_BYTES),
        name="attn_step",
    )(_step_bias_table(rel_bias), q3, kn3, vn3,
      attn_out_g.reshape(N_HEADS_A, HEAD_DIM_A), *views, cache_k, cache_v, kn4, vn4)


N_SCAL = 5 * N_HEADS_B
SCAL_W, SCAL_A, SCAL_I, SCAL_R, SCAL_M = (k * N_HEADS_B for k in range(5))


def _mlstm_step_pre_kernel(xb_ref, gt_ref, gbias_ref, sc_ref, cw_ref, cb_ref, wq_ref, wk_ref, n_ref, m_ref,
                           q_out, k_out, cact_out, n_out, scal_out):
    x = xb_ref[...]
    c = cb_ref[...] + x * cw_ref[CONV_W - 1:CONV_W, :]
    for t in range(CONV_W - 1):
        c = c + sc_ref[t] * cw_ref[t:t + 1, :]
    c_act = c * jax.nn.sigmoid(c)
    cact_out[...] = c_act
    gts = gt_ref[...] + gbias_ref[...]
    logf = _log_sigmoid(gts)
    scal_out[...] = jnp.zeros_like(scal_out)
    for h in range(N_HEADS_B):
        sl = slice(h * HEAD_PAD, (h + 1) * HEAD_PAD)
        ch_bf = c_act[:, sl].astype(bf16)
        q = jnp.dot(ch_bf, wq_ref[h], preferred_element_type=f32)
        k = jnp.dot(ch_bf, wk_ref[h], preferred_element_type=f32) * (1.0 / math.sqrt(HEAD_DIM_B))
        q_out[:, sl] = q
        k_out[:, sl] = k
        i_pre = gts[:, h:h + 1]
        a = logf[:, N_HEADS_B + h:N_HEADS_B + h + 1]
        m_old = m_ref[:, h:h + 1]
        m_t = jnp.maximum(a + m_old, i_pre)
        w_state = jnp.exp(a + m_old - m_t)
        w_in = jnp.exp(i_pre - m_t)
        amat = jnp.sum(q * k, axis=-1, keepdims=True) * w_in
        n_old = n_ref[:, sl]
        den = w_state * jnp.sum(n_old * q, axis=-1, keepdims=True) + amat
        n_out[:, sl] = w_state * n_old + w_in * k
        for base, val in ((SCAL_W, w_state), (SCAL_A, amat), (SCAL_I, w_in),
                          (SCAL_R, 1.0 / jnp.maximum(jnp.abs(den), jnp.exp(-m_t))), (SCAL_M, m_t)):
            scal_out[:, base + h:base + h + 1] = val


def _mlstm_step_pre(xb, gts, gbias, sc_t, cw, cb, wq, wk, n_pad, m_pad):
    nb = xb.shape[0]
    wide = jax.ShapeDtypeStruct((nb, D_B_PAD), f32)
    return pl.pallas_call(
        _mlstm_step_pre_kernel,
        out_shape=(wide, wide, wide, wide, jax.ShapeDtypeStruct((nb, LANES), f32)),
        compiler_params=pltpu.CompilerParams(vmem_limit_bytes=VMEM_LIMIT_BYTES),
        name="mlstm_step_pre",
    )(xb, gts, gbias, sc_t, cw, cb, wq, wk, n_pad, m_pad)


ST_BT = 8


def _mlstm_step_state_kernel(scal_ref, c_ref, q_ref, k_ref, v_ref, ob_ref, cact_ref, mg_ref, skip_ref,
                             c_out, out_ref, h_scr):
    i = pl.program_id(0)

    @pl.when(i == 0)
    def _init():
        h_scr[...] = jnp.zeros_like(h_scr)

    E = HEAD_DIM_B
    eye = (lax.broadcasted_iota(jnp.int32, (E, E), 0) == lax.broadcasted_iota(jnp.int32, (E, E), 1))

    def per_seq(bi, carry):
        b = i * ST_BT + bi
        for h in range(N_HEADS_B):
            lanes = slice(h * HEAD_PAD, h * HEAD_PAD + E)
            c_old = c_ref[bi, h]
            q_row = q_ref[pl.ds(b, 1), lanes]
            k_row = k_ref[pl.ds(b, 1), lanes]
            v_row = v_ref[pl.ds(b, 1), lanes]
            w_state = scal_ref[b * N_SCAL + SCAL_W + h]
            amat = scal_ref[b * N_SCAL + SCAL_A + h]
            w_in = scal_ref[b * N_SCAL + SCAL_I + h]
            rden = scal_ref[b * N_SCAL + SCAL_R + h]
            cq = jnp.sum(c_old * q_row, axis=-1, keepdims=True)
            v_col = jnp.sum(jnp.where(eye, v_row, 0.0), axis=-1, keepdims=True)
            h_col = (w_state * cq + amat * v_col) * rden
            c_out[bi, h] = w_state * c_old + (w_in * v_col) * k_row
            h_scr[pl.ds(b, 1), lanes] = jnp.sum(jnp.where(eye, h_col, 0.0), axis=0, keepdims=True)
        return carry

    lax.fori_loop(0, ST_BT, per_seq, 0)

    @pl.when(i == pl.num_programs(0) - 1)
    def _gate():
        for h in range(N_HEADS_B):
            sl = slice(h * HEAD_PAD, (h + 1) * HEAD_PAD)
            hh = h_scr[:, sl]
            hn = hh * lax.rsqrt(jnp.sum(hh * hh, axis=-1, keepdims=True) * (1.0 / E) + EPS) * mg_ref[:, sl]
            out_ref[:, sl] = jax.nn.sigmoid(ob_ref[:, sl]) * (hn + skip_ref[:, sl] * cact_ref[:, sl])


def _mlstm_step_state(scal_flat, c_state, q, k, v, ob, c_act, mg, skip):
    nb = c_state.shape[0]
    assert nb % ST_BT == 0
    whole = lambda w: pl.BlockSpec((nb, w), lambda i: (0, 0))
    cblk = pl.BlockSpec((ST_BT, N_HEADS_B, HEAD_DIM_B, HEAD_DIM_B), lambda i: (i, 0, 0, 0))
    return pl.pallas_call(
        _mlstm_step_state_kernel,
        out_shape=(jax.ShapeDtypeStruct(c_state.shape, f32), jax.ShapeDtypeStruct((nb, D_B_PAD), f32)),
        grid=(nb // ST_BT,),
        in_specs=[pl.BlockSpec(memory_space=pltpu.SMEM), cblk,
                  whole(D_B_PAD), whole(D_B_PAD), whole(D_B_PAD), whole(D_B_PAD), whole(D_B_PAD),
                  pl.BlockSpec((1, D_B_PAD), lambda i: (0, 0)), pl.BlockSpec((1, D_B_PAD), lambda i: (0, 0))],
        out_specs=(cblk, whole(D_B_PAD)),
        scratch_shapes=[pltpu.VMEM((nb, D_B_PAD), f32)],
        compiler_params=pltpu.CompilerParams(
            dimension_semantics=("arbitrary",),
            vmem_limit_bytes=VMEM_LIMIT_BYTES),
        name="mlstm_step_state",
    )(scal_flat, c_state, q, k, v, ob, c_act, mg, skip)


def kernel(x_prompt, x_sample, cache_win_k, cache_win_v, state_conv, state_C, state_n, state_m,
           rel_bias, norm1_g, w_in, gate_bias, conv_w, conv_b, wq_head, wk_head, attn_out_g,
           mh_norm_g, skip, w_out, norm2_g, w_ff1, w_ff2, final_g):
    Bp, Sp, _ = x_prompt.shape
    Bs, Ss, _ = x_sample.shape
    assert Ss == 1
    g1 = norm1_g[0].reshape(1, D_MODEL)
    g2 = norm2_g[0].reshape(1, D_MODEL)
    gf = final_g.reshape(1, D_MODEL)
    w_in_pad = _pad_w_in(w_in[0]).astype(bf16)
    wo = jnp.concatenate([w_out[0][:D_A], jnp.zeros((LANES, D_MODEL), f32),
                          _pad_heads(w_out[0][D_A:].T).T], axis=0).astype(bf16)
    w1 = w_ff1[0].astype(bf16)
    w2 = w_ff2[0].astype(bf16)
    hpad = HEAD_PAD - HEAD_DIM_B
    wq_p = jnp.pad(wq_head[0], ((0, 0), (0, hpad), (0, hpad))).astype(bf16)
    wk_p = jnp.pad(wk_head[0], ((0, 0), (0, hpad), (0, hpad))).astype(bf16)
    gbias = jnp.pad(gate_bias[0], (0, LANES - N_GATES)).reshape(1, LANES)
    cw_p = _pad_heads(conv_w[0])
    cb_p = _pad_heads(conv_b[0]).reshape(1, D_B_PAD)
    mg_p = _pad_heads(mh_norm_g[0]).reshape(1, D_B_PAD)
    skip_p = _pad_heads(skip[0]).reshape(1, D_B_PAD)

    xp2 = x_prompt.reshape(Bp * Sp, D_MODEL)
    slabs, xb, ob, gts = _norm_inproj(xp2, g1, w_in_pad, tm=512)
    out_a = _attn_prompt(slabs, rel_bias, attn_out_g[0], Bp, Sp)
    out_b, c_p, n_p, m_p, tail_p = _mlstm_prompt(xb, ob, gts, gbias, cw_p, cb_p, wq_p, wk_p, mg_p,
                                                 skip_p, Bp, Sp)
    y_p = _out_ffn(xp2, out_a, out_b, wo, g2, w1, w2, gf, tm=512).reshape(Bp, Sp, D_MODEL)
    P = min(WINDOW_MAX, Sp)
    kv = slabs.reshape(3, N_PAIRS, Bp, Sp, LANES)[1:, :, :, Sp - P:]
    kv = kv.transpose(0, 2, 3, 1, 4).reshape(2, Bp, P, N_HEADS_A, HEAD_DIM_A)
    st_p = (kv[0][None], kv[1][None], _unpad_heads(tail_p[:, SUBLANES - (CONV_W - 1):])[None],
            c_p[:, :, :HEAD_DIM_B, :HEAD_DIM_B][None], n_p[:, :, 0, :HEAD_DIM_B][None],
            m_p[:, :, 0, 0][None])

    xs2 = x_sample.reshape(Bs, D_MODEL)
    slabs_s, xb_s, ob_s, gts_s = _norm_inproj(xs2, g1, w_in_pad, tm=Bs)
    qkv_s = slabs_s.reshape(3, N_PAIRS, Bs, LANES).transpose(0, 2, 1, 3)
    qkv_s = qkv_s.reshape(3, Bs, N_HEADS_A, HEAD_DIM_A)
    oa_s, new_k, new_v = _attn_step(qkv_s[0], qkv_s[1], qkv_s[2], cache_win_k[0], cache_win_v[0],
                                    rel_bias, attn_out_g[0])
    sc_pad = _pad_heads(state_conv[0])
    q_s, k_s, cact_s, n_s, scal = _mlstm_step_pre(
        xb_s, gts_s, gbias, sc_pad.transpose(1, 0, 2), cw_p, cb_p, wq_p, wk_p,
        _pad_heads(state_n[0].reshape(Bs, D_B)), jnp.pad(state_m[0], ((0, 0), (0, LANES - N_HEADS_B))))
    c_s, outb_s = _mlstm_step_state(scal[:, :N_SCAL].reshape(-1), state_C[0], q_s, k_s, xb_s, ob_s,
                                    cact_s, mg_p, skip_p)
    y_s = _out_ffn(xs2, oa_s.reshape(Bs, D_A), outb_s, wo, g2, w1, w2, gf,
                   tm=Bs).reshape(Bs, Ss, D_MODEL)
    new_conv = jnp.concatenate([state_conv[0][:, 1:], _unpad_heads(xb_s)[:, None]], axis=1)
    st_s = (new_k, new_v, new_conv[None], c_s[None],
            _unpad_heads(n_s).reshape(Bs, N_HEADS_B, HEAD_DIM_B)[None],
            scal[:, SCAL_M:SCAL_M + N_HEADS_B][None])
    return (y_p, y_s) + st_p + st_s
```

```python
import functools
import math

import jax
import jax.numpy as jnp
from jax import lax
from jax.experimental import pallas as pl
from jax.experimental.pallas import tpu as pltpu

D_MODEL = 1024
HEAD_DIM_A = 64
N_HEADS_A = 6
D_A = N_HEADS_A * HEAD_DIM_A
D_B = D_MODEL - D_A
N_HEADS_B = 4
HEAD_DIM_B = D_B // N_HEADS_B
DILATED_BRANCHES = ((128, 1), (512, 4), (2048, 16))
WINDOW_MAX = 2048
N_BUCKETS = 32
REL_MAX_DIST = 2048
CONV_W = 4
MLSTM_CHUNK = 128
D_FF = 4 * D_MODEL
N_GATES = 2 * N_HEADS_B
D_IN = 3 * D_A + 2 * D_B + N_GATES
SPLITS = [D_A, 2 * D_A, 3 * D_A, 3 * D_A + D_B, 3 * D_A + 2 * D_B]
EPS = 1e-6
NEG = -1e30

LANES = 128
SUBLANES = 8
VMEM_LIMIT_BYTES = 56 * 1024 * 1024

N_PAIRS = D_A // LANES
N_SLABS = 3 * N_PAIRS
HEAD_PAD = 2 * LANES
D_B_PAD = N_HEADS_B * HEAD_PAD
D_IN_PAD = 3 * D_A + 2 * D_B_PAD + LANES
D_O_PAD = D_A + LANES + D_B_PAD
BLK = 128
Q_SUPER = BLK * DILATED_BRANCHES[-1][1]

f32 = jnp.float32
bf16 = jnp.bfloat16


def _round_up(n, m):
    return -(-n // m) * m


def _rms(xf, g):
    return xf * lax.rsqrt(jnp.mean(xf * xf, axis=-1, keepdims=True) + EPS) * g


def _pad_heads(t):
    t = t.reshape(t.shape[:-1] + (N_HEADS_B, HEAD_DIM_B))
    t = jnp.pad(t, [(0, 0)] * (t.ndim - 1) + [(0, HEAD_PAD - HEAD_DIM_B)])
    return t.reshape(t.shape[:-2] + (D_B_PAD,))


def _unpad_heads(t):
    t = t.reshape(t.shape[:-1] + (N_HEADS_B, HEAD_PAD))[..., :HEAD_DIM_B]
    return t.reshape(t.shape[:-2] + (D_B,))


def _norm_inproj_kernel(x_ref, g_ref, w_ref, slab_ref, xb_ref, ob_ref, gt_ref):
    h = _rms(x_ref[...], g_ref[...]).astype(bf16)
    qkv = jnp.dot(h, w_ref[:, 0:3 * D_A], preferred_element_type=f32)
    for j in range(N_SLABS):
        slab_ref[j] = qkv[:, j * LANES:(j + 1) * LANES]
    o = 3 * D_A
    xb_ref[...] = jnp.dot(h, w_ref[:, o:o + D_B_PAD], preferred_element_type=f32)
    o += D_B_PAD
    ob_ref[...] = jnp.dot(h, w_ref[:, o:o + D_B_PAD], preferred_element_type=f32)
    o += D_B_PAD
    gt_ref[...] = jnp.dot(h, w_ref[:, o:o + LANES], preferred_element_type=f32)


def _norm_inproj(x, g, w_pad_bf16, tm):
    m, d = x.shape
    assert m % tm == 0
    row = lambda w: pl.BlockSpec((tm, w), lambda i: (i, 0))
    return pl.pallas_call(
        _norm_inproj_kernel,
        out_shape=(
            jax.ShapeDtypeStruct((N_SLABS, m, LANES), f32),
            jax.ShapeDtypeStruct((m, D_B_PAD), f32),
            jax.ShapeDtypeStruct((m, D_B_PAD), f32),
            jax.ShapeDtypeStruct((m, LANES), f32),
        ),
        grid=(m // tm,),
        in_specs=[
            row(d),
            pl.BlockSpec((1, d), lambda i: (0, 0)),
            pl.BlockSpec((d, D_IN_PAD), lambda i: (0, 0), pipeline_mode=pl.Buffered(1)),
        ],
        out_specs=(
            pl.BlockSpec((N_SLABS, tm, LANES), lambda i: (0, i, 0)),
            row(D_B_PAD), row(D_B_PAD), row(LANES),
        ),
        compiler_params=pltpu.CompilerParams(
            dimension_semantics=("arbitrary",),
            vmem_limit_bytes=VMEM_LIMIT_BYTES),
        name="norm_inproj",
    )(x, g, w_pad_bf16)


def _norm_inproj_step_kernel(x_ref, g_ref, wt_ref, w_ref, zt_ref, xb_ref, ob_ref, gt_ref):
    h = _rms(x_ref[...], g_ref[...]).astype(bf16)
    zt_ref[...] = lax.dot_general(wt_ref[...], h, (((1,), (1,)), ((), ())), preferred_element_type=f32)
    o = 3 * D_A
    xb_ref[...] = jnp.dot(h, w_ref[:, o:o + D_B_PAD], preferred_element_type=f32)
    o += D_B_PAD
    ob_ref[...] = jnp.dot(h, w_ref[:, o:o + D_B_PAD], preferred_element_type=f32)
    o += D_B_PAD
    gt_ref[...] = jnp.dot(h, w_ref[:, o:o + LANES], preferred_element_type=f32)


def _norm_inproj_step(x, g, wqkv_t_bf16, w_pad_bf16):
    nb = x.shape[0]
    wide = jax.ShapeDtypeStruct((nb, D_B_PAD), f32)
    return pl.pallas_call(
        _norm_inproj_step_kernel,
        out_shape=(jax.ShapeDtypeStruct((3 * D_A, nb), f32), wide, wide,
                   jax.ShapeDtypeStruct((nb, LANES), f32)),
        compiler_params=pltpu.CompilerParams(vmem_limit_bytes=VMEM_LIMIT_BYTES),
        name="norm_inproj_step",
    )(x, g, wqkv_t_bf16, w_pad_bf16)


def _pad_w_in(w_in):
    qkv = w_in[:, :3 * D_A]
    xb = _pad_heads(w_in[:, SPLITS[2]:SPLITS[3]])
    ob = _pad_heads(w_in[:, SPLITS[3]:SPLITS[4]])
    gt = jnp.pad(w_in[:, SPLITS[4]:], ((0, 0), (0, LANES - N_GATES)))
    return jnp.concatenate([qkv, xb, ob, gt], axis=1)


def _out_ffn_kernel(x_ref, oa_ref, ob_ref, wo_ref, g2_ref, w1_ref, w2_ref, gf_ref, y_ref, *, ff_chunk):
    o = jnp.concatenate([oa_ref[...].astype(bf16), jnp.zeros((oa_ref.shape[0], LANES), bf16),
                         ob_ref[...].astype(bf16)], axis=1)
    x1 = x_ref[...] + jnp.dot(o, wo_ref[...], preferred_element_type=f32)
    h2 = _rms(x1, g2_ref[...]).astype(bf16)
    acc = x1
    for c in range(D_FF // ff_chunk):
        u = jnp.dot(h2, w1_ref[:, c * ff_chunk:(c + 1) * ff_chunk], preferred_element_type=f32)
        u = jnp.square(jnp.maximum(u, 0.0)).astype(bf16)
        acc = acc + jnp.dot(u, w2_ref[c * ff_chunk:(c + 1) * ff_chunk, :],
                            preferred_element_type=f32)
    y_ref[...] = _rms(acc, gf_ref[...])


def _out_ffn(x, oa, ob, wo, g2, w1, w2, gf, tm, ff_chunk=1024):
    m, d = x.shape
    assert wo.shape[0] == D_O_PAD
    const = lambda i: (0, 0)
    single = dict(pipeline_mode=pl.Buffered(1))
    row = lambda w: pl.BlockSpec((tm, w), lambda i: (i, 0))
    return pl.pallas_call(
        functools.partial(_out_ffn_kernel, ff_chunk=ff_chunk),
        out_shape=jax.ShapeDtypeStruct((m, d), f32),
        grid=(m // tm,),
        in_specs=[
            row(d), row(D_A), row(D_B_PAD),
            pl.BlockSpec((D_O_PAD, d), const, **single),
            pl.BlockSpec((1, d), const),
            pl.BlockSpec((d, D_FF), const, **single),
            pl.BlockSpec((D_FF, d), const, **single),
            pl.BlockSpec((1, d), const),
        ],
        out_specs=row(d),
        compiler_params=pltpu.CompilerParams(
            dimension_semantics=("arbitrary",),
            vmem_limit_bytes=VMEM_LIMIT_BYTES),
        name="outproj_ffn",
    )(x, oa, ob, wo, g2, w1, w2, gf)


def _t5_bucket(dist):
    max_exact = N_BUCKETS // 2
    df = jnp.maximum(dist, 1).astype(jnp.float32)
    large = max_exact + (jnp.log(df / max_exact) / math.log(REL_MAX_DIST / max_exact)
                         * (N_BUCKETS - max_exact)).astype(jnp.int32)
    large = jnp.minimum(large, N_BUCKETS - 1)
    return jnp.where(dist < max_exact, dist, large)


def _prompt_bias_table(rel_bias):
    qi = jnp.arange(BLK)[:, None]
    ki = jnp.arange(2 * BLK)[None, :]
    j = qi + BLK - ki
    band = (j >= 0) & (j <= BLK)
    tabs = []
    for (_, d) in DILATED_BRANCHES:
        bias = rel_bias[_t5_bucket(jnp.clip(j, 0, None) * d)].astype(f32)
        bias = jnp.where(band[:, :, None], bias, NEG).transpose(2, 0, 1)
        tabs.append(bias.reshape(N_PAIRS, 2 * BLK, 2 * BLK))
    return jnp.stack(tabs)


def _attn_prompt_kernel(bias_ref, q_ref, kp_ref, kc_ref, vp_ref, vc_ref, g_ref, o_ref,
                        m_scr, l_scr, acc_scr):
    n = pl.program_id(1)
    lane = lax.broadcasted_iota(jnp.int32, (BLK, LANES), 1)
    low = lane < HEAD_DIM_A
    kcol = lax.broadcasted_iota(jnp.int32, (2 * BLK, 2 * BLK), 1)
    first_extra = jnp.where((kcol < BLK) & (n == 0), NEG, 0.0).astype(f32)

    def job(g, d, ka_ref, va_ref, row_a, row_b, q_start, extra):
        def ld(ref, p, start):
            if d == 1:
                return ref[p, pl.ds(start, BLK), :]
            return ref.at[p][pl.ds(start, BLK, stride=d), :]

        def st(ref, p, val):
            if d == 1:
                ref[p, pl.ds(q_start, BLK), :] = val
            else:
                ref.at[p][pl.ds(q_start, BLK, stride=d), :] = val

        for p in range(N_PAIRS):
            q2 = ld(q_ref, p, q_start) * (1.0 / math.sqrt(HEAD_DIM_A))
            qcat = jnp.concatenate([jnp.where(low, q2, 0.0), jnp.where(low, 0.0, q2)], axis=0)
            kk = jnp.concatenate([ld(ka_ref, p, row_a), ld(kc_ref, p, row_b)], axis=0)
            vv = jnp.concatenate([ld(va_ref, p, row_a), ld(vc_ref, p, row_b)], axis=0)
            s = lax.dot_general(qcat.astype(bf16), kk.astype(bf16), (((1,), (1,)), ((), ())),
                                preferred_element_type=f32)
            s = s + bias_ref[g, p]
            if extra is not None:
                s = s + extra
            m = jnp.max(s, axis=-1, keepdims=True)
            e = jnp.exp(s - m)
            l = jnp.sum(e, axis=-1, keepdims=True)
            o = jnp.dot(e.astype(bf16), vv.astype(bf16), preferred_element_type=f32)
            o_t = jnp.where(low, o[:BLK], o[BLK:])
            m_t = jnp.where(low, m[:BLK], m[BLK:])
            l_t = jnp.where(low, l[:BLK], l[BLK:])
            if g == 0:
                st(acc_scr, p, o_t)
                st(m_scr, p, m_t)
                st(l_scr, p, l_t)
            else:
                m_o = ld(m_scr, p, q_start)
                m_n = jnp.maximum(m_o, m_t)
                al = jnp.exp(m_o - m_n)
                be = jnp.exp(m_t - m_n)
                st(acc_scr, p, al * ld(acc_scr, p, q_start) + be * o_t)
                st(l_scr, p, al * ld(l_scr, p, q_start) + be * l_t)
                st(m_scr, p, m_n)

    for g, (_, d) in enumerate(DILATED_BRANCHES):
        unit = BLK * d
        nu = Q_SUPER // unit
        log_d = d.bit_length() - 1

        def first(r, carry, g=g, d=d, unit=unit, nu=nu):
            job(g, d, kp_ref, vp_ref, (nu - 1) * unit + r, r, r, first_extra)
            return carry

        def rest(idx, carry, g=g, d=d, unit=unit, log_d=log_d):
            u = 1 + lax.shift_right_logical(idx, log_d)
            r = jnp.bitwise_and(idx, d - 1)
            row_b = u * unit + r
            job(g, d, kc_ref, vc_ref, row_b - unit, row_b, row_b, None)
            return carry

        if d == 1:
            first(0, 0)
        else:
            lax.fori_loop(0, d, first, 0)
        if nu > 1:
            lax.fori_loop(0, (nu - 1) * d, rest, 0)

    rc = 2 * BLK

    def fin(c, carry):
        rows = pl.ds(pl.multiple_of(c * rc, rc), rc)
        os_ = [acc_scr[p, rows, :] / l_scr[p, rows, :] for p in range(N_PAIRS)]
        ss = sum(jnp.sum(o * o, axis=-1, keepdims=True) for o in os_)
        sc = lax.rsqrt(ss * (1.0 / D_A) + EPS)
        for p in range(N_PAIRS):
            o_ref[rows, p * LANES:(p + 1) * LANES] = os_[p] * sc * g_ref[:, p * LANES:(p + 1) * LANES]
        return carry

    lax.fori_loop(0, Q_SUPER // rc, fin, 0)


def _attn_prompt(slabs, rel_bias, attn_out_g, batch, seq):
    assert seq % Q_SUPER == 0
    nsb = seq // Q_SUPER
    nbr = len(DILATED_BRANCHES)

    def cur(bi, ni):
        return bi * nsb + ni

    def prev(bi, ni):
        return bi * nsb + jnp.maximum(ni - 1, 0)

    slab = lambda grp, rowf: pl.BlockSpec((N_PAIRS, Q_SUPER, LANES),
                                          lambda bi, ni: (grp, rowf(bi, ni), 0))
    return pl.pallas_call(
        _attn_prompt_kernel,
        out_shape=jax.ShapeDtypeStruct((batch * seq, D_A), f32),
        grid=(batch, nsb),
        in_specs=[
            pl.BlockSpec((nbr, N_PAIRS, 2 * BLK, 2 * BLK), lambda bi, ni: (0, 0, 0, 0),
                         pipeline_mode=pl.Buffered(1)),
            slab(0, cur), slab(1, prev), slab(1, cur), slab(2, prev), slab(2, cur),
            pl.BlockSpec((1, D_A), lambda bi, ni: (0, 0)),
        ],
        out_specs=pl.BlockSpec((Q_SUPER, D_A), lambda bi, ni: (cur(bi, ni), 0)),
        scratch_shapes=[
            pltpu.VMEM((N_PAIRS, Q_SUPER, LANES), f32),
            pltpu.VMEM((N_PAIRS, Q_SUPER, LANES), f32),
            pltpu.VMEM((N_PAIRS, Q_SUPER, LANES), f32),
        ],
        compiler_params=pltpu.CompilerParams(
            dimension_semantics=("arbitrary", "arbitrary"),
            vmem_limit_bytes=VMEM_LIMIT_BYTES),
        name="attn_prompt",
    )(_prompt_bias_table(rel_bias), slabs, slabs, slabs, slabs, slabs, attn_out_g.reshape(1, D_A))


def _log_sigmoid(x):
    return -(jnp.maximum(-x, 0.0) + jnp.log1p(jnp.exp(-jnp.abs(x))))


def _mlstm_prompt_kernel(xb_ref, ob_ref, gt_ref, gbias_ref, cw_ref, cb_ref, wq_ref, wk_ref, mg_ref,
                         skip_ref, out_ref, c_out_ref, n_out_ref, m_out_ref, conv_out_ref,
                         conv_scr, c_scr, n_scr, m_scr):
    c_idx = pl.program_id(1)
    L = MLSTM_CHUNK

    @pl.when(c_idx == 0)
    def _init():
        conv_scr[0:SUBLANES, :] = jnp.zeros((SUBLANES, D_B_PAD), f32)
        c_scr[...] = jnp.zeros_like(c_scr)
        n_scr[...] = jnp.zeros_like(n_scr)
        m_scr[...] = jnp.zeros_like(m_scr)

    x = xb_ref[...]
    conv_scr[SUBLANES:SUBLANES + L, :] = x
    c = cb_ref[...] + x * cw_ref[CONV_W - 1:CONV_W, :]
    for i in range(CONV_W - 1):
        sh = CONV_W - 1 - i
        c = c + conv_scr[SUBLANES - sh:SUBLANES - sh + L, :] * cw_ref[i:i + 1, :]
    conv_scr[0:SUBLANES, :] = x[L - SUBLANES:, :]
    c_act = c * jax.nn.sigmoid(c)

    gts = gt_ref[...] + gbias_ref[...]
    logf = _log_sigmoid(gts)
    row = lax.broadcasted_iota(jnp.int32, (L, L), 0)
    col = lax.broadcasted_iota(jnp.int32, (L, L), 1)
    causal = row >= col
    a_all = jnp.dot(causal.astype(f32), logf, precision=lax.Precision.HIGHEST,
                    preferred_element_type=f32)
    gts_t = gts.T
    a_t = a_all.T

    for h in range(N_HEADS_B):
        sl = slice(h * HEAD_PAD, (h + 1) * HEAD_PAD)
        ch = c_act[:, sl]
        ch_bf = ch.astype(bf16)
        q = jnp.dot(ch_bf, wq_ref[h], preferred_element_type=f32)
        k = jnp.dot(ch_bf, wk_ref[h], preferred_element_type=f32) * (1.0 / math.sqrt(HEAD_DIM_B))
        v = x[:, sl]
        q_bf, k_bf = q.astype(bf16), k.astype(bf16)
        a_col = a_all[:, N_HEADS_B + h:N_HEADS_B + h + 1]
        i_col = gts[:, h:h + 1]
        a_row = a_t[N_HEADS_B + h:N_HEADS_B + h + 1, :]
        i_row = gts_t[h:h + 1, :]
        m_prev = m_scr[h]
        dmat = jnp.where(causal, a_col - a_row + i_row, NEG)
        g = a_col + m_prev
        m_t = jnp.maximum(g, jnp.max(dmat, axis=-1, keepdims=True))
        w_state = jnp.exp(g - m_t)
        s = lax.dot_general(q_bf, k_bf, (((1,), (1,)), ((), ())), preferred_element_type=f32)
        amat = s * jnp.exp(dmat - m_t)
        c_state = c_scr[h]
        inter = lax.dot_general(q_bf, c_state.astype(bf16), (((1,), (1,)), ((), ())),
                                preferred_element_type=f32)
        num = w_state * inter + jnp.dot(amat.astype(bf16), v.astype(bf16),
                                        preferred_element_type=f32)
        n_row = n_scr[h]
        den = (w_state * jnp.sum(q * n_row, axis=-1, keepdims=True)
               + jnp.sum(amat, axis=-1, keepdims=True))
        hh = num / jnp.maximum(jnp.abs(den), jnp.exp(-m_t))
        hn = hh * lax.rsqrt(jnp.sum(hh * hh, axis=-1, keepdims=True) * (1.0 / HEAD_DIM_B) + EPS)
        hn = hn * mg_ref[:, sl]
        out_ref[:, sl] = jax.nn.sigmoid(ob_ref[:, sl]) * (hn + skip_ref[:, sl] * ch)

        b_tot = a_col[L - 1:L, :]
        wl = b_tot - a_col + i_col
        m_new = jnp.maximum(b_tot + m_prev, jnp.max(wl, axis=0, keepdims=True))
        wk = jnp.exp(wl - m_new)
        decay = jnp.exp(b_tot + m_prev - m_new)
        upd = lax.dot_general((wk * v).astype(bf16), k_bf, (((0,), (0,)), ((), ())),
                              preferred_element_type=f32)
        c_scr[h] = decay * c_state + upd
        n_scr[h] = decay * n_row + jnp.sum(wk * k, axis=0, keepdims=True)
        m_scr[h] = m_new

    @pl.when(c_idx == pl.num_programs(1) - 1)
    def _final():
        c_out_ref[0] = c_scr[...]
        n_out_ref[0] = n_scr[...]
        m_out_ref[0] = m_scr[...]
        conv_out_ref[0] = x[L - SUBLANES:, :]


def _mlstm_prompt(xb, ob, gates, gbias, cw, cb, wq, wk, mg, skip, batch, seq):
    L = MLSTM_CHUNK
    assert seq % L == 0
    nc = seq // L
    rows = lambda w: pl.BlockSpec((L, w), lambda bi, ci: (bi * nc + ci, 0))
    const2 = lambda shape: pl.BlockSpec(shape, lambda bi, ci: (0, 0))
    const3 = lambda shape: pl.BlockSpec(shape, lambda bi, ci: (0, 0, 0))
    state = lambda shape: pl.BlockSpec((1,) + shape, lambda bi, ci: (bi,) + (0,) * len(shape))
    return pl.pallas_call(
        _mlstm_prompt_kernel,
        out_shape=(
            jax.ShapeDtypeStruct((batch * seq, D_B_PAD), f32),
            jax.ShapeDtypeStruct((batch, N_HEADS_B, HEAD_PAD, HEAD_PAD), f32),
            jax.ShapeDtypeStruct((batch, N_HEADS_B, 1, HEAD_PAD), f32),
            jax.ShapeDtypeStruct((batch, N_HEADS_B, 1, 1), f32),
            jax.ShapeDtypeStruct((batch, SUBLANES, D_B_PAD), f32),
        ),
        grid=(batch, nc),
        in_specs=[rows(D_B_PAD), rows(D_B_PAD), rows(LANES), const2((1, LANES)),
                  const2((CONV_W, D_B_PAD)), const2((1, D_B_PAD)),
                  const3((N_HEADS_B, HEAD_PAD, HEAD_PAD)), const3((N_HEADS_B, HEAD_PAD, HEAD_PAD)),
                  const2((1, D_B_PAD)), const2((1, D_B_PAD))],
        out_specs=(rows(D_B_PAD), state((N_HEADS_B, HEAD_PAD, HEAD_PAD)),
                   state((N_HEADS_B, 1, HEAD_PAD)), state((N_HEADS_B, 1, 1)),
                   state((SUBLANES, D_B_PAD))),
        scratch_shapes=[
            pltpu.VMEM((SUBLANES + L, D_B_PAD), f32),
            pltpu.VMEM((N_HEADS_B, HEAD_PAD, HEAD_PAD), f32),
            pltpu.VMEM((N_HEADS_B, 1, HEAD_PAD), f32),
            pltpu.VMEM((N_HEADS_B, 1, 1), f32),
        ],
        compiler_params=pltpu.CompilerParams(
            dimension_semantics=("arbitrary", "arbitrary"),
            vmem_limit_bytes=VMEM_LIMIT_BYTES),
        name="mlstm_prompt",
    )(xb, ob, gates, gbias, cw, cb, wq, wk, mg, skip)


def _step_bias_rows(rel_bias, p):
    dist = p - jnp.arange(p)
    full = rel_bias[_t5_bucket(dist)].astype(f32).T
    rows = []
    for (w, d) in DILATED_BRANCHES:
        in_branch = (dist % d == 0) & (dist <= w)
        rows.append(jnp.where(in_branch[None, :], full, NEG))
    zero = rel_bias[_t5_bucket(jnp.zeros((1,), jnp.int32))].astype(f32).T
    nbr = len(DILATED_BRANCHES)
    return jnp.stack(rows)[:, :, None, :], jnp.broadcast_to(zero[None, :, None, :], (nbr, N_HEADS_A, 1, 1))


def _attn_step_kernel(bias_ref, bias0_ref, zt_ref, g_ref, kt_ref, vt_ref, ot_ref, nk_ref, nv_ref):
    b = pl.program_id(0)
    p = kt_ref.shape[-1]
    nbr = len(DILATED_BRANCHES)
    scale = 1.0 / math.sqrt(HEAD_DIM_A)

    @pl.when(b == 0)
    def _init():
        ot_ref[...] = jnp.zeros_like(ot_ref)

    zt = zt_ref[...]
    is_b = lax.broadcasted_iota(jnp.int32, zt.shape, 1) == b
    col = jnp.sum(jnp.where(is_b, zt, 0.0), axis=1, keepdims=True)
    is_last = lax.broadcasted_iota(jnp.int32, (HEAD_DIM_A, p), 1) == p - 1
    o_cols = []
    for h in range(N_HEADS_A):
        q_col = col[h * HEAD_DIM_A:(h + 1) * HEAD_DIM_A] * scale
        kn_col = col[D_A + h * HEAD_DIM_A:D_A + (h + 1) * HEAD_DIM_A]
        vn_col = col[2 * D_A + h * HEAD_DIM_A:2 * D_A + (h + 1) * HEAD_DIM_A]
        kt = kt_ref[0, h]
        vt = vt_ref[0, h]
        nk_ref[0, h] = jnp.where(is_last, kn_col, pltpu.roll(kt, p - 1, axis=1))
        nv_ref[0, h] = jnp.where(is_last, vn_col, pltpu.roll(vt, p - 1, axis=1))
        s = jnp.sum(kt * q_col, axis=0, keepdims=True)
        s_new = jnp.sum(kn_col * q_col, axis=0, keepdims=True)
        sg = [s + bias_ref[g, h] for g in range(nbr)]
        s0 = [s_new + bias0_ref[g, h] for g in range(nbr)]
        m = functools.reduce(jnp.maximum, [jnp.max(x, axis=1, keepdims=True) for x in sg] + s0)
        pw = sum(jnp.exp(x - m) for x in sg)
        e0 = sum(jnp.exp(x - m) for x in s0)
        l = jnp.sum(pw, axis=1, keepdims=True) + e0
        o_cols.append((jnp.sum(vt * pw, axis=1, keepdims=True) + e0 * vn_col) / l)
    o = jnp.concatenate(o_cols, axis=0)
    ssq = jnp.sum(o * o, axis=0, keepdims=True)
    o = o * lax.rsqrt(ssq * (1.0 / D_A) + EPS) * g_ref[...]
    sel = lax.broadcasted_iota(jnp.int32, ot_ref.shape, 1) == b
    ot_ref[...] = jnp.where(sel, o, ot_ref[...])


def _attn_step(zt, cache_kt, cache_vt, rel_bias, attn_out_g):
    nb, p = cache_kt.shape[0], cache_kt.shape[-1]
    for (w, d) in DILATED_BRANCHES:
        assert w <= p and w // d == BLK
    nbr = len(DILATED_BRANCHES)
    bias, bias0 = _step_bias_rows(rel_bias, p)
    blk = pl.BlockSpec((1, N_HEADS_A, HEAD_DIM_A, p), lambda i: (i, 0, 0, 0))
    buf = jax.ShapeDtypeStruct(cache_kt.shape, f32)
    return pl.pallas_call(
        _attn_step_kernel,
        out_shape=(jax.ShapeDtypeStruct((D_A, nb), f32), buf, buf),
        grid=(nb,),
        in_specs=[
            pl.BlockSpec((nbr, N_HEADS_A, 1, p), lambda i: (0, 0, 0, 0)),
            pl.BlockSpec((nbr, N_HEADS_A, 1, 1), lambda i: (0, 0, 0, 0)),
            pl.BlockSpec((3 * D_A, nb), lambda i: (0, 0)),
            pl.BlockSpec((D_A, 1), lambda i: (0, 0)),
            blk, blk,
        ],
        out_specs=(pl.BlockSpec((D_A, nb), lambda i: (0, 0)), blk, blk),
        compiler_params=pltpu.CompilerParams(
            dimension_semantics=("arbitrary",),
            vmem_limit_bytes=VMEM_LIMIT_BYTES),
        name="attn_step",
    )(bias, bias0, zt, attn_out_g.reshape(D_A, 1), cache_kt, cache_vt)


N_SCAL = 5 * N_HEADS_B
SCAL_W, SCAL_A, SCAL_I, SCAL_R, SCAL_M = (k * N_HEADS_B for k in range(5))


def _mlstm_step_pre_kernel(xb_ref, gt_ref, gbias_ref, sc_ref, cw_ref, cb_ref, wq_ref, wk_ref, n_ref, m_ref,
                           q_out, k_out, cact_out, n_out, scal_out):
    x = xb_ref[...]
    c = cb_ref[...] + x * cw_ref[CONV_W - 1:CONV_W, :]
    for t in range(CONV_W - 1):
        c = c + sc_ref[t] * cw_ref[t:t + 1, :]
    c_act = c * jax.nn.sigmoid(c)
    cact_out[...] = c_act
    gts = gt_ref[...] + gbias_ref[...]
    logf = _log_sigmoid(gts)
    scal_out[...] = jnp.zeros_like(scal_out)
    for h in range(N_HEADS_B):
        sl = slice(h * HEAD_PAD, (h + 1) * HEAD_PAD)
        ch_bf = c_act[:, sl].astype(bf16)
        q = jnp.dot(ch_bf, wq_ref[h], preferred_element_type=f32)
        k = jnp.dot(ch_bf, wk_ref[h], preferred_element_type=f32) * (1.0 / math.sqrt(HEAD_DIM_B))
        q_out[:, sl] = q
        k_out[:, sl] = k
        i_pre = gts[:, h:h + 1]
        a = logf[:, N_HEADS_B + h:N_HEADS_B + h + 1]
        m_old = m_ref[:, h:h + 1]
        m_t = jnp.maximum(a + m_old, i_pre)
        w_state = jnp.exp(a + m_old - m_t)
        w_in = jnp.exp(i_pre - m_t)
        amat = jnp.sum(q * k, axis=-1, keepdims=True) * w_in
        n_old = n_ref[:, sl]
        den = w_state * jnp.sum(n_old * q, axis=-1, keepdims=True) + amat
        n_out[:, sl] = w_state * n_old + w_in * k
        for base, val in ((SCAL_W, w_state), (SCAL_A, amat), (SCAL_I, w_in),
                          (SCAL_R, 1.0 / jnp.maximum(jnp.abs(den), jnp.exp(-m_t))), (SCAL_M, m_t)):
            scal_out[:, base + h:base + h + 1] = val


def _mlstm_step_pre(xb, gts, gbias, sc_t, cw, cb, wq, wk, n_pad, m_pad):
    nb = xb.shape[0]
    wide = jax.ShapeDtypeStruct((nb, D_B_PAD), f32)
    return pl.pallas_call(
        _mlstm_step_pre_kernel,
        out_shape=(wide, wide, wide, wide, jax.ShapeDtypeStruct((nb, LANES), f32)),
        compiler_params=pltpu.CompilerParams(vmem_limit_bytes=VMEM_LIMIT_BYTES),
        name="mlstm_step_pre",
    )(xb, gts, gbias, sc_t, cw, cb, wq, wk, n_pad, m_pad)


ST_BT = 8


def _mlstm_step_state_kernel(scal_ref, c_ref, q_ref, k_ref, v_ref, ob_ref, cact_ref, mg_ref, skip_ref,
                             c_out, out_ref, h_scr):
    i = pl.program_id(0)

    @pl.when(i == 0)
    def _init():
        h_scr[...] = jnp.zeros_like(h_scr)

    E = HEAD_DIM_B
    eye = (lax.broadcasted_iota(jnp.int32, (E, E), 0) == lax.broadcasted_iota(jnp.int32, (E, E), 1))

    def per_seq(bi, carry):
        b = i * ST_BT + bi
        for h in range(N_HEADS_B):
            lanes = slice(h * HEAD_PAD, h * HEAD_PAD + E)
            c_old = c_ref[bi, h]
            q_row = q_ref[pl.ds(b, 1), lanes]
            k_row = k_ref[pl.ds(b, 1), lanes]
            v_row = v_ref[pl.ds(b, 1), lanes]
            w_state = scal_ref[b * N_SCAL + SCAL_W + h]
            amat = scal_ref[b * N_SCAL + SCAL_A + h]
            w_in = scal_ref[b * N_SCAL + SCAL_I + h]
            rden = scal_ref[b * N_SCAL + SCAL_R + h]
            cq = jnp.sum(c_old * q_row, axis=-1, keepdims=True)
            v_col = jnp.sum(jnp.where(eye, v_row, 0.0), axis=-1, keepdims=True)
            h_col = (w_state * cq + amat * v_col) * rden
            c_out[bi, h] = w_state * c_old + (w_in * v_col) * k_row
            h_scr[pl.ds(b, 1), lanes] = jnp.sum(jnp.where(eye, h_col, 0.0), axis=0, keepdims=True)
        return carry

    lax.fori_loop(0, ST_BT, per_seq, 0)

    @pl.when(i == pl.num_programs(0) - 1)
    def _gate():
        for h in range(N_HEADS_B):
            sl = slice(h * HEAD_PAD, (h + 1) * HEAD_PAD)
            hh = h_scr[:, sl]
            hn = hh * lax.rsqrt(jnp.sum(hh * hh, axis=-1, keepdims=True) * (1.0 / E) + EPS) * mg_ref[:, sl]
            out_ref[:, sl] = jax.nn.sigmoid(ob_ref[:, sl]) * (hn + skip_ref[:, sl] * cact_ref[:, sl])


def _mlstm_step_state(scal_flat, c_state, q, k, v, ob, c_act, mg, skip):
    nb = c_state.shape[0]
    assert nb % ST_BT == 0
    whole = lambda w: pl.BlockSpec((nb, w), lambda i: (0, 0))
    cblk = pl.BlockSpec((ST_BT, N_HEADS_B, HEAD_DIM_B, HEAD_DIM_B), lambda i: (i, 0, 0, 0))
    return pl.pallas_call(
        _mlstm_step_state_kernel,
        out_shape=(jax.ShapeDtypeStruct(c_state.shape, f32), jax.ShapeDtypeStruct((nb, D_B_PAD), f32)),
        grid=(nb // ST_BT,),
        in_specs=[pl.BlockSpec(memory_space=pltpu.SMEM), cblk,
                  whole(D_B_PAD), whole(D_B_PAD), whole(D_B_PAD), whole(D_B_PAD), whole(D_B_PAD),
                  pl.BlockSpec((1, D_B_PAD), lambda i: (0, 0)), pl.BlockSpec((1, D_B_PAD), lambda i: (0, 0))],
        out_specs=(cblk, whole(D_B_PAD)),
        scratch_shapes=[pltpu.VMEM((nb, D_B_PAD), f32)],
        compiler_params=pltpu.CompilerParams(
            dimension_semantics=("arbitrary",),
            vmem_limit_bytes=VMEM_LIMIT_BYTES),
        name="mlstm_step_state",
    )(scal_flat, c_state, q, k, v, ob, c_act, mg, skip)


def kernel(x_prompt, x_sample, cache_win_k, cache_win_v, state_conv, state_C, state_n, state_m,
           rel_bias, norm1_g, w_in, gate_bias, conv_w, conv_b, wq_head, wk_head, attn_out_g,
           mh_norm_g, skip, w_out, norm2_g, w_ff1, w_ff2, final_g):
    Bp, Sp, _ = x_prompt.shape
    Bs, Ss, _ = x_sample.shape
    assert Ss == 1
    g1 = norm1_g[0].reshape(1, D_MODEL)
    g2 = norm2_g[0].reshape(1, D_MODEL)
    gf = final_g.reshape(1, D_MODEL)
    w_in_pad = _pad_w_in(w_in[0]).astype(bf16)
    wo = jnp.concatenate([w_out[0][:D_A], jnp.zeros((LANES, D_MODEL), f32),
                          _pad_heads(w_out[0][D_A:].T).T], axis=0).astype(bf16)
    w1 = w_ff1[0].astype(bf16)
    w2 = w_ff2[0].astype(bf16)
    hpad = HEAD_PAD - HEAD_DIM_B
    wq_p = jnp.pad(wq_head[0], ((0, 0), (0, hpad), (0, hpad))).astype(bf16)
    wk_p = jnp.pad(wk_head[0], ((0, 0), (0, hpad), (0, hpad))).astype(bf16)
    gbias = jnp.pad(gate_bias[0], (0, LANES - N_GATES)).reshape(1, LANES)
    cw_p = _pad_heads(conv_w[0])
    cb_p = _pad_heads(conv_b[0]).reshape(1, D_B_PAD)
    mg_p = _pad_heads(mh_norm_g[0]).reshape(1, D_B_PAD)
    skip_p = _pad_heads(skip[0]).reshape(1, D_B_PAD)

    xp2 = x_prompt.reshape(Bp * Sp, D_MODEL)
    slabs, xb, ob, gts = _norm_inproj(xp2, g1, w_in_pad, tm=512)
    out_a = _attn_prompt(slabs, rel_bias, attn_out_g[0], Bp, Sp)
    out_b, c_p, n_p, m_p, tail_p = _mlstm_prompt(xb, ob, gts, gbias, cw_p, cb_p, wq_p, wk_p, mg_p,
                                                 skip_p, Bp, Sp)
    y_p = _out_ffn(xp2, out_a, out_b, wo, g2, w1, w2, gf, tm=512).reshape(Bp, Sp, D_MODEL)
    P = min(WINDOW_MAX, Sp)
    kv = slabs.reshape(3, N_PAIRS, Bp, Sp, LANES)[1:, :, :, Sp - P:]
    kv = kv.transpose(0, 2, 3, 1, 4).reshape(2, Bp, P, N_HEADS_A, HEAD_DIM_A)
    st_p = (kv[0][None], kv[1][None], _unpad_heads(tail_p[:, SUBLANES - (CONV_W - 1):])[None],
            c_p[:, :, :HEAD_DIM_B, :HEAD_DIM_B][None], n_p[:, :, 0, :HEAD_DIM_B][None],
            m_p[:, :, 0, 0][None])

    xs2 = x_sample.reshape(Bs, D_MODEL)
    zt_s, xb_s, ob_s, gts_s = _norm_inproj_step(xs2, g1, w_in[0][:, :3 * D_A].T.astype(bf16), w_in_pad)
    oat_s, nkt, nvt = _attn_step(zt_s, cache_win_k[0].transpose(0, 2, 3, 1),
                                 cache_win_v[0].transpose(0, 2, 3, 1), rel_bias, attn_out_g[0])
    oa_s = oat_s.T
    new_k = nkt.transpose(0, 3, 1, 2)[None]
    new_v = nvt.transpose(0, 3, 1, 2)[None]
    sc_pad = _pad_heads(state_conv[0])
    q_s, k_s, cact_s, n_s, scal = _mlstm_step_pre(
        xb_s, gts_s, gbias, sc_pad.transpose(1, 0, 2), cw_p, cb_p, wq_p, wk_p,
        _pad_heads(state_n[0].reshape(Bs, D_B)), jnp.pad(state_m[0], ((0, 0), (0, LANES - N_HEADS_B))))
    c_s, outb_s = _mlstm_step_state(scal[:, :N_SCAL].reshape(-1), state_C[0], q_s, k_s, xb_s, ob_s,
                                    cact_s, mg_p, skip_p)
    y_s = _out_ffn(xs2, oa_s.reshape(Bs, D_A), outb_s, wo, g2, w1, w2, gf,
                   tm=Bs).reshape(Bs, Ss, D_MODEL)
    new_conv = jnp.concatenate([state_conv[0][:, 1:], _unpad_heads(xb_s)[:, None]], axis=1)
    st_s = (new_k, new_v, new_conv[None], c_s[None],
            _unpad_heads(n_s).reshape(Bs, N_HEADS_B, HEAD_DIM_B)[None],
            scal[:, SCAL_M:SCAL_M + N_HEADS_B][None])
    return (y_p, y_s) + st_p + st_s
```

```python
import functools
import math

import jax
import jax.numpy as jnp
from jax import lax
from jax.experimental import pallas as pl
from jax.experimental.pallas import tpu as pltpu

D_MODEL = 1024
HEAD_DIM_A = 64
N_HEADS_A = 6
D_A = N_HEADS_A * HEAD_DIM_A
D_B = D_MODEL - D_A
N_HEADS_B = 4
HEAD_DIM_B = D_B // N_HEADS_B
DILATED_BRANCHES = ((128, 1), (512, 4), (2048, 16))
WINDOW_MAX = 2048
N_BUCKETS = 32
REL_MAX_DIST = 2048
CONV_W = 4
MLSTM_CHUNK = 128
D_FF = 4 * D_MODEL
N_GATES = 2 * N_HEADS_B
D_IN = 3 * D_A + 2 * D_B + N_GATES
SPLITS = [D_A, 2 * D_A, 3 * D_A, 3 * D_A + D_B, 3 * D_A + 2 * D_B]
EPS = 1e-6
NEG = -1e30

LANES = 128
SUBLANES = 8
VMEM_LIMIT_BYTES = 56 * 1024 * 1024

N_PAIRS = D_A // LANES
N_SLABS = 3 * N_PAIRS
HEAD_PAD = 2 * LANES
D_B_PAD = N_HEADS_B * HEAD_PAD
D_IN_PAD = 3 * D_A + 2 * D_B_PAD + LANES
D_O_PAD = D_A + LANES + D_B_PAD
BLK = 128
Q_SUPER = BLK * DILATED_BRANCHES[-1][1]

f32 = jnp.float32
bf16 = jnp.bfloat16


def _round_up(n, m):
    return -(-n // m) * m


def _rms(xf, g):
    return xf * lax.rsqrt(jnp.mean(xf * xf, axis=-1, keepdims=True) + EPS) * g


def _pad_heads(t):
    t = t.reshape(t.shape[:-1] + (N_HEADS_B, HEAD_DIM_B))
    t = jnp.pad(t, [(0, 0)] * (t.ndim - 1) + [(0, HEAD_PAD - HEAD_DIM_B)])
    return t.reshape(t.shape[:-2] + (D_B_PAD,))


def _unpad_heads(t):
    t = t.reshape(t.shape[:-1] + (N_HEADS_B, HEAD_PAD))[..., :HEAD_DIM_B]
    return t.reshape(t.shape[:-2] + (D_B,))


def _norm_inproj_kernel(x_ref, g_ref, w_ref, slab_ref, xb_ref, ob_ref, gt_ref):
    h = _rms(x_ref[...], g_ref[...]).astype(bf16)
    qkv = jnp.dot(h, w_ref[:, 0:3 * D_A], preferred_element_type=f32)
    for j in range(N_SLABS):
        slab_ref[j] = qkv[:, j * LANES:(j + 1) * LANES]
    o = 3 * D_A
    xb_ref[...] = jnp.dot(h, w_ref[:, o:o + D_B_PAD], preferred_element_type=f32)
    o += D_B_PAD
    ob_ref[...] = jnp.dot(h, w_ref[:, o:o + D_B_PAD], preferred_element_type=f32)
    o += D_B_PAD
    gt_ref[...] = jnp.dot(h, w_ref[:, o:o + LANES], preferred_element_type=f32)


def _norm_inproj(x, g, w_pad_bf16, tm):
    m, d = x.shape
    assert m % tm == 0
    row = lambda w: pl.BlockSpec((tm, w), lambda i: (i, 0))
    return pl.pallas_call(
        _norm_inproj_kernel,
        out_shape=(
            jax.ShapeDtypeStruct((N_SLABS, m, LANES), f32),
            jax.ShapeDtypeStruct((m, D_B_PAD), f32),
            jax.ShapeDtypeStruct((m, D_B_PAD), f32),
            jax.ShapeDtypeStruct((m, LANES), f32),
        ),
        grid=(m // tm,),
        in_specs=[
            row(d),
            pl.BlockSpec((1, d), lambda i: (0, 0)),
            pl.BlockSpec((d, D_IN_PAD), lambda i: (0, 0), pipeline_mode=pl.Buffered(1)),
        ],
        out_specs=(
            pl.BlockSpec((N_SLABS, tm, LANES), lambda i: (0, i, 0)),
            row(D_B_PAD), row(D_B_PAD), row(LANES),
        ),
        compiler_params=pltpu.CompilerParams(
            dimension_semantics=("arbitrary",),
            vmem_limit_bytes=VMEM_LIMIT_BYTES),
        name="norm_inproj",
    )(x, g, w_pad_bf16)


def _norm_inproj_step_kernel(x_ref, g_ref, wt_ref, w_ref, zt_ref, xb_ref, ob_ref, gt_ref):
    h = _rms(x_ref[...], g_ref[...]).astype(bf16)
    zt_ref[...] = lax.dot_general(wt_ref[...], h, (((1,), (1,)), ((), ())), preferred_element_type=f32)
    o = 3 * D_A
    xb_ref[...] = jnp.dot(h, w_ref[:, o:o + D_B_PAD], preferred_element_type=f32)
    o += D_B_PAD
    ob_ref[...] = jnp.dot(h, w_ref[:, o:o + D_B_PAD], preferred_element_type=f32)
    o += D_B_PAD
    gt_ref[...] = jnp.dot(h, w_ref[:, o:o + LANES], preferred_element_type=f32)


def _norm_inproj_step(x, g, wqkv_t_bf16, w_pad_bf16):
    nb = x.shape[0]
    wide = jax.ShapeDtypeStruct((nb, D_B_PAD), f32)
    return pl.pallas_call(
        _norm_inproj_step_kernel,
        out_shape=(jax.ShapeDtypeStruct((3 * D_A, nb), f32), wide, wide,
                   jax.ShapeDtypeStruct((nb, LANES), f32)),
        compiler_params=pltpu.CompilerParams(vmem_limit_bytes=VMEM_LIMIT_BYTES),
        name="norm_inproj_step",
    )(x, g, wqkv_t_bf16, w_pad_bf16)


def _pad_w_in(w_in):
    qkv = w_in[:, :3 * D_A]
    xb = _pad_heads(w_in[:, SPLITS[2]:SPLITS[3]])
    ob = _pad_heads(w_in[:, SPLITS[3]:SPLITS[4]])
    gt = jnp.pad(w_in[:, SPLITS[4]:], ((0, 0), (0, LANES - N_GATES)))
    return jnp.concatenate([qkv, xb, ob, gt], axis=1)


def _out_ffn_kernel(x_ref, oa_ref, ob_ref, wo_ref, g2_ref, w1_ref, w2_ref, gf_ref, y_ref, *, ff_chunk):
    o = jnp.concatenate([oa_ref[...].astype(bf16), jnp.zeros((oa_ref.shape[0], LANES), bf16),
                         ob_ref[...].astype(bf16)], axis=1)
    x1 = x_ref[...] + jnp.dot(o, wo_ref[...], preferred_element_type=f32)
    h2 = _rms(x1, g2_ref[...]).astype(bf16)
    acc = x1
    for c in range(D_FF // ff_chunk):
        u = jnp.dot(h2, w1_ref[:, c * ff_chunk:(c + 1) * ff_chunk], preferred_element_type=f32)
        u = jnp.square(jnp.maximum(u, 0.0)).astype(bf16)
        acc = acc + jnp.dot(u, w2_ref[c * ff_chunk:(c + 1) * ff_chunk, :],
                            preferred_element_type=f32)
    y_ref[...] = _rms(acc, gf_ref[...])


def _out_ffn(x, oa, ob, wo, g2, w1, w2, gf, tm, ff_chunk=1024):
    m, d = x.shape
    assert wo.shape[0] == D_O_PAD
    const = lambda i: (0, 0)
    single = dict(pipeline_mode=pl.Buffered(1))
    row = lambda w: pl.BlockSpec((tm, w), lambda i: (i, 0))
    return pl.pallas_call(
        functools.partial(_out_ffn_kernel, ff_chunk=ff_chunk),
        out_shape=jax.ShapeDtypeStruct((m, d), f32),
        grid=(m // tm,),
        in_specs=[
            row(d), row(D_A), row(D_B_PAD),
            pl.BlockSpec((D_O_PAD, d), const, **single),
            pl.BlockSpec((1, d), const),
            pl.BlockSpec((d, D_FF), const, **single),
            pl.BlockSpec((D_FF, d), const, **single),
            pl.BlockSpec((1, d), const),
        ],
        out_specs=row(d),
        compiler_params=pltpu.CompilerParams(
            dimension_semantics=("arbitrary",),
            vmem_limit_bytes=VMEM_LIMIT_BYTES),
        name="outproj_ffn",
    )(x, oa, ob, wo, g2, w1, w2, gf)


def _t5_bucket(dist):
    max_exact = N_BUCKETS // 2
    df = jnp.maximum(dist, 1).astype(jnp.float32)
    large = max_exact + (jnp.log(df / max_exact) / math.log(REL_MAX_DIST / max_exact)
                         * (N_BUCKETS - max_exact)).astype(jnp.int32)
    large = jnp.minimum(large, N_BUCKETS - 1)
    return jnp.where(dist < max_exact, dist, large)


def _prompt_bias_table(rel_bias):
    period = 3 * BLK + 1
    r = jnp.arange(period)
    t = jnp.where(r < 2 * BLK, r, r - period)
    j = BLK - t
    band = (j >= 0) & (j <= BLK)
    tabs = []
    for (_, d) in DILATED_BRANCHES:
        vals = rel_bias[_t5_bucket(jnp.clip(j, 0, None) * d)].astype(f32)
        w = jnp.where(band[:, None], vals, NEG).T
        flat = jnp.tile(w, (1, BLK))[:, :BLK * (period - 1)]
        toe = flat.reshape(N_HEADS_A, BLK, period - 1)[:, :, :2 * BLK]
        tabs.append(toe.reshape(N_PAIRS, 2 * BLK, 2 * BLK))
    return jnp.stack(tabs)


def _attn_prompt_kernel(bias_ref, q_ref, kp_ref, kc_ref, vp_ref, vc_ref, g_ref, o_ref,
                        m_scr, l_scr, acc_scr):
    n = pl.program_id(1)
    lane = lax.broadcasted_iota(jnp.int32, (BLK, LANES), 1)
    low = lane < HEAD_DIM_A
    kcol = lax.broadcasted_iota(jnp.int32, (2 * BLK, 2 * BLK), 1)
    first_extra = jnp.where((kcol < BLK) & (n == 0), NEG, 0.0).astype(f32)

    def job(g, d, ka_ref, va_ref, row_a, row_b, q_start, extra):
        def ld(ref, p, start):
            if d == 1:
                return ref[p, pl.ds(start, BLK), :]
            return ref.at[p][pl.ds(start, BLK, stride=d), :]

        def st(ref, p, val):
            if d == 1:
                ref[p, pl.ds(q_start, BLK), :] = val
            else:
                ref.at[p][pl.ds(q_start, BLK, stride=d), :] = val

        for p in range(N_PAIRS):
            q2 = ld(q_ref, p, q_start) * (1.0 / math.sqrt(HEAD_DIM_A))
            qcat = jnp.concatenate([jnp.where(low, q2, 0.0), jnp.where(low, 0.0, q2)], axis=0)
            kk = jnp.concatenate([ld(ka_ref, p, row_a), ld(kc_ref, p, row_b)], axis=0)
            vv = jnp.concatenate([ld(va_ref, p, row_a), ld(vc_ref, p, row_b)], axis=0)
            s = lax.dot_general(qcat.astype(bf16), kk.astype(bf16), (((1,), (1,)), ((), ())),
                                preferred_element_type=f32)
            s = s + bias_ref[g, p]
            if extra is not None:
                s = s + extra
            m = jnp.max(s, axis=-1, keepdims=True)
            e = jnp.exp(s - m)
            l = jnp.sum(e, axis=-1, keepdims=True)
            o = jnp.dot(e.astype(bf16), vv.astype(bf16), preferred_element_type=f32)
            o_t = jnp.where(low, o[:BLK], o[BLK:])
            m_t = jnp.where(low, m[:BLK], m[BLK:])
            l_t = jnp.where(low, l[:BLK], l[BLK:])
            if g == 0:
                st(acc_scr, p, o_t)
                st(m_scr, p, m_t)
                st(l_scr, p, l_t)
            else:
                m_o = ld(m_scr, p, q_start)
                m_n = jnp.maximum(m_o, m_t)
                al = jnp.exp(m_o - m_n)
                be = jnp.exp(m_t - m_n)
                st(acc_scr, p, al * ld(acc_scr, p, q_start) + be * o_t)
                st(l_scr, p, al * ld(l_scr, p, q_start) + be * l_t)
                st(m_scr, p, m_n)

    for g, (_, d) in enumerate(DILATED_BRANCHES):
        unit = BLK * d
        nu = Q_SUPER // unit
        log_d = d.bit_length() - 1

        def first(r, carry, g=g, d=d, unit=unit, nu=nu):
            job(g, d, kp_ref, vp_ref, (nu - 1) * unit + r, r, r, first_extra)
            return carry

        def rest(idx, carry, g=g, d=d, unit=unit, log_d=log_d):
            u = 1 + lax.shift_right_logical(idx, log_d)
            r = jnp.bitwise_and(idx, d - 1)
            row_b = u * unit + r
            job(g, d, kc_ref, vc_ref, row_b - unit, row_b, row_b, None)
            return carry

        if d == 1:
            first(0, 0)
        else:
            lax.fori_loop(0, d, first, 0, unroll=2)
        if nu > 1:
            lax.fori_loop(0, (nu - 1) * d, rest, 0, unroll=5 if d == 1 else 3)

    rc = 2 * BLK

    def fin(c, carry):
        rows = pl.ds(pl.multiple_of(c * rc, rc), rc)
        os_ = [acc_scr[p, rows, :] / l_scr[p, rows, :] for p in range(N_PAIRS)]
        ss = sum(jnp.sum(o * o, axis=-1, keepdims=True) for o in os_)
        sc = lax.rsqrt(ss * (1.0 / D_A) + EPS)
        for p in range(N_PAIRS):
            o_ref[rows, p * LANES:(p + 1) * LANES] = os_[p] * sc * g_ref[:, p * LANES:(p + 1) * LANES]
        return carry

    lax.fori_loop(0, Q_SUPER // rc, fin, 0)


def _attn_prompt(slabs, rel_bias, attn_out_g, batch, seq):
    assert seq % Q_SUPER == 0
    nsb = seq // Q_SUPER
    nbr = len(DILATED_BRANCHES)

    def cur(bi, ni):
        return bi * nsb + ni

    def prev(bi, ni):
        return bi * nsb + jnp.maximum(ni - 1, 0)

    slab = lambda grp, rowf: pl.BlockSpec((N_PAIRS, Q_SUPER, LANES),
                                          lambda bi, ni: (grp, rowf(bi, ni), 0))
    return pl.pallas_call(
        _attn_prompt_kernel,
        out_shape=jax.ShapeDtypeStruct((batch * seq, D_A), f32),
        grid=(batch, nsb),
        in_specs=[
            pl.BlockSpec((nbr, N_PAIRS, 2 * BLK, 2 * BLK), lambda bi, ni: (0, 0, 0, 0),
                         pipeline_mode=pl.Buffered(1)),
            slab(0, cur), slab(1, prev), slab(1, cur), slab(2, prev), slab(2, cur),
            pl.BlockSpec((1, D_A), lambda bi, ni: (0, 0)),
        ],
        out_specs=pl.BlockSpec((Q_SUPER, D_A), lambda bi, ni: (cur(bi, ni), 0)),
        scratch_shapes=[
            pltpu.VMEM((N_PAIRS, Q_SUPER, LANES), f32),
            pltpu.VMEM((N_PAIRS, Q_SUPER, LANES), f32),
            pltpu.VMEM((N_PAIRS, Q_SUPER, LANES), f32),
        ],
        compiler_params=pltpu.CompilerParams(
            dimension_semantics=("arbitrary", "arbitrary"),
            vmem_limit_bytes=VMEM_LIMIT_BYTES),
        name="attn_prompt",
    )(_prompt_bias_table(rel_bias), slabs, slabs, slabs, slabs, slabs, attn_out_g.reshape(1, D_A))


def _log_sigmoid(x):
    return -(jnp.maximum(-x, 0.0) + jnp.log1p(jnp.exp(-jnp.abs(x))))


def _mlstm_prompt_kernel(xb_ref, ob_ref, gt_ref, gbias_ref, cw_ref, cb_ref, wq_ref, wk_ref, mg_ref,
                         skip_ref, out_ref, c_out_ref, n_out_ref, m_out_ref, conv_out_ref,
                         conv_scr, c_scr, n_scr, m_scr):
    c_idx = pl.program_id(1)
    L = MLSTM_CHUNK

    @pl.when(c_idx == 0)
    def _init():
        conv_scr[0:SUBLANES, :] = jnp.zeros((SUBLANES, D_B_PAD), f32)
        c_scr[...] = jnp.zeros_like(c_scr)
        n_scr[...] = jnp.zeros_like(n_scr)
        m_scr[...] = jnp.zeros_like(m_scr)

    x = xb_ref[...]
    conv_scr[SUBLANES:SUBLANES + L, :] = x
    c = cb_ref[...] + x * cw_ref[CONV_W - 1:CONV_W, :]
    for i in range(CONV_W - 1):
        sh = CONV_W - 1 - i
        c = c + conv_scr[SUBLANES - sh:SUBLANES - sh + L, :] * cw_ref[i:i + 1, :]
    conv_scr[0:SUBLANES, :] = x[L - SUBLANES:, :]
    c_act = c * jax.nn.sigmoid(c)

    gts = gt_ref[...] + gbias_ref[...]
    logf = _log_sigmoid(gts)
    row = lax.broadcasted_iota(jnp.int32, (L, L), 0)
    col = lax.broadcasted_iota(jnp.int32, (L, L), 1)
    causal = row >= col
    a_all = jnp.dot(causal.astype(f32), logf, precision=lax.Precision.HIGHEST,
                    preferred_element_type=f32)
    gts_t = gts.T
    a_t = a_all.T

    for h in range(N_HEADS_B):
        sl = slice(h * HEAD_PAD, (h + 1) * HEAD_PAD)
        ch = c_act[:, sl]
        ch_bf = ch.astype(bf16)
        q = jnp.dot(ch_bf, wq_ref[h], preferred_element_type=f32)
        k = jnp.dot(ch_bf, wk_ref[h], preferred_element_type=f32) * (1.0 / math.sqrt(HEAD_DIM_B))
        v = x[:, sl]
        q_bf, k_bf = q.astype(bf16), k.astype(bf16)
        a_col = a_all[:, N_HEADS_B + h:N_HEADS_B + h + 1]
        i_col = gts[:, h:h + 1]
        a_row = a_t[N_HEADS_B + h:N_HEADS_B + h + 1, :]
        i_row = gts_t[h:h + 1, :]
        m_prev = m_scr[h]
        dmat = jnp.where(causal, a_col - a_row + i_row, NEG)
        g = a_col + m_prev
        m_t = jnp.maximum(g, jnp.max(dmat, axis=-1, keepdims=True))
        w_state = jnp.exp(g - m_t)
        s = lax.dot_general(q_bf, k_bf, (((1,), (1,)), ((), ())), preferred_element_type=f32)
        amat = s * jnp.exp(dmat - m_t)
        c_state = c_scr[h]
        inter = lax.dot_general(q_bf, c_state.astype(bf16), (((1,), (1,)), ((), ())),
                                preferred_element_type=f32)
        num = w_state * inter + jnp.dot(amat.astype(bf16), v.astype(bf16),
                                        preferred_element_type=f32)
        n_row = n_scr[h]
        den = (w_state * jnp.sum(q * n_row, axis=-1, keepdims=True)
               + jnp.sum(amat, axis=-1, keepdims=True))
        hh = num / jnp.maximum(jnp.abs(den), jnp.exp(-m_t))
        hn = hh * lax.rsqrt(jnp.sum(hh * hh, axis=-1, keepdims=True) * (1.0 / HEAD_DIM_B) + EPS)
        hn = hn * mg_ref[:, sl]
        out_ref[:, sl] = jax.nn.sigmoid(ob_ref[:, sl]) * (hn + skip_ref[:, sl] * ch)

        b_tot = a_col[L - 1:L, :]
        wl = b_tot - a_col + i_col
        m_new = jnp.maximum(b_tot + m_prev, jnp.max(wl, axis=0, keepdims=True))
        wk = jnp.exp(wl - m_new)
        decay = jnp.exp(b_tot + m_prev - m_new)
        upd = lax.dot_general((wk * v).astype(bf16), k_bf, (((0,), (0,)), ((), ())),
                              preferred_element_type=f32)
        c_scr[h] = decay * c_state + upd
        n_scr[h] = decay * n_row + jnp.sum(wk * k, axis=0, keepdims=True)
        m_scr[h] = m_new

    @pl.when(c_idx == pl.num_programs(1) - 1)
    def _final():
        c_out_ref[0] = c_scr[...]
        n_out_ref[0] = n_scr[...]
        m_out_ref[0] = m_scr[...]
        conv_out_ref[0] = x[L - SUBLANES:, :]


def _mlstm_prompt(xb, ob, gates, gbias, cw, cb, wq, wk, mg, skip, batch, seq):
    L = MLSTM_CHUNK
    assert seq % L == 0
    nc = seq // L
    rows = lambda w: pl.BlockSpec((L, w), lambda bi, ci: (bi * nc + ci, 0))
    const2 = lambda shape: pl.BlockSpec(shape, lambda bi, ci: (0, 0))
    const3 = lambda shape: pl.BlockSpec(shape, lambda bi, ci: (0, 0, 0))
    state = lambda shape: pl.BlockSpec((1,) + shape, lambda bi, ci: (bi,) + (0,) * len(shape))
    return pl.pallas_call(
        _mlstm_prompt_kernel,
        out_shape=(
            jax.ShapeDtypeStruct((batch * seq, D_B_PAD), f32),
            jax.ShapeDtypeStruct((batch, N_HEADS_B, HEAD_PAD, HEAD_PAD), f32),
            jax.ShapeDtypeStruct((batch, N_HEADS_B, 1, HEAD_PAD), f32),
            jax.ShapeDtypeStruct((batch, N_HEADS_B, 1, 1), f32),
            jax.ShapeDtypeStruct((batch, SUBLANES, D_B_PAD), f32),
        ),
        grid=(batch, nc),
        in_specs=[rows(D_B_PAD), rows(D_B_PAD), rows(LANES), const2((1, LANES)),
                  const2((CONV_W, D_B_PAD)), const2((1, D_B_PAD)),
                  const3((N_HEADS_B, HEAD_PAD, HEAD_PAD)), const3((N_HEADS_B, HEAD_PAD, HEAD_PAD)),
                  const2((1, D_B_PAD)), const2((1, D_B_PAD))],
        out_specs=(rows(D_B_PAD), state((N_HEADS_B, HEAD_PAD, HEAD_PAD)),
                   state((N_HEADS_B, 1, HEAD_PAD)), state((N_HEADS_B, 1, 1)),
                   state((SUBLANES, D_B_PAD))),
        scratch_shapes=[
            pltpu.VMEM((SUBLANES + L, D_B_PAD), f32),
            pltpu.VMEM((N_HEADS_B, HEAD_PAD, HEAD_PAD), f32),
            pltpu.VMEM((N_HEADS_B, 1, HEAD_PAD), f32),
            pltpu.VMEM((N_HEADS_B, 1, 1), f32),
        ],
        compiler_params=pltpu.CompilerParams(
            dimension_semantics=("arbitrary", "arbitrary"),
            vmem_limit_bytes=VMEM_LIMIT_BYTES),
        name="mlstm_prompt",
    )(xb, ob, gates, gbias, cw, cb, wq, wk, mg, skip)


def _step_bias_rows(rel_bias, p):
    dist = p - jnp.arange(p)
    full = rel_bias[_t5_bucket(dist)].astype(f32).T
    rows = []
    for (w, d) in DILATED_BRANCHES:
        in_branch = (dist % d == 0) & (dist <= w)
        rows.append(jnp.where(in_branch[None, :], full, NEG))
    zero = rel_bias[_t5_bucket(jnp.zeros((1,), jnp.int32))].astype(f32).T
    nbr = len(DILATED_BRANCHES)
    return jnp.stack(rows)[:, :, None, :], jnp.broadcast_to(zero[None, :, None, :], (nbr, N_HEADS_A, 1, 1))


def _attn_step_kernel(bias_ref, bias0_ref, zt_ref, g_ref, kt_ref, vt_ref, ot_ref, nk_ref, nv_ref):
    b = pl.program_id(0)
    p = kt_ref.shape[-1]
    nbr = len(DILATED_BRANCHES)
    scale = 1.0 / math.sqrt(HEAD_DIM_A)

    @pl.when(b == 0)
    def _init():
        ot_ref[...] = jnp.zeros_like(ot_ref)

    zt = zt_ref[...]
    is_b = lax.broadcasted_iota(jnp.int32, zt.shape, 1) == b
    col = jnp.sum(jnp.where(is_b, zt, 0.0), axis=1, keepdims=True)
    is_last = lax.broadcasted_iota(jnp.int32, (HEAD_DIM_A, p), 1) == p - 1
    o_cols = []
    for h in range(N_HEADS_A):
        q_col = col[h * HEAD_DIM_A:(h + 1) * HEAD_DIM_A] * scale
        kn_col = col[D_A + h * HEAD_DIM_A:D_A + (h + 1) * HEAD_DIM_A]
        vn_col = col[2 * D_A + h * HEAD_DIM_A:2 * D_A + (h + 1) * HEAD_DIM_A]
        kt = kt_ref[0, h]
        vt = vt_ref[0, h]
        nk_ref[0, h] = jnp.where(is_last, kn_col, pltpu.roll(kt, p - 1, axis=1))
        nv_ref[0, h] = jnp.where(is_last, vn_col, pltpu.roll(vt, p - 1, axis=1))
        s = jnp.sum(kt * q_col, axis=0, keepdims=True)
        s_new = jnp.sum(kn_col * q_col, axis=0, keepdims=True)
        sg = [s + bias_ref[g, h] for g in range(nbr)]
        s0 = [s_new + bias0_ref[g, h] for g in range(nbr)]
        m = functools.reduce(jnp.maximum, [jnp.max(x, axis=1, keepdims=True) for x in sg] + s0)
        pw = sum(jnp.exp(x - m) for x in sg)
        e0 = sum(jnp.exp(x - m) for x in s0)
        l = jnp.sum(pw, axis=1, keepdims=True) + e0
        o_cols.append((jnp.sum(vt * pw, axis=1, keepdims=True) + e0 * vn_col) / l)
    o = jnp.concatenate(o_cols, axis=0)
    ssq = jnp.sum(o * o, axis=0, keepdims=True)
    o = o * lax.rsqrt(ssq * (1.0 / D_A) + EPS) * g_ref[...]
    sel = lax.broadcasted_iota(jnp.int32, ot_ref.shape, 1) == b
    ot_ref[...] = jnp.where(sel, o, ot_ref[...])


def _attn_step(zt, cache_kt, cache_vt, rel_bias, attn_out_g):
    nb, p = cache_kt.shape[0], cache_kt.shape[-1]
    for (w, d) in DILATED_BRANCHES:
        assert w <= p and w // d == BLK
    nbr = len(DILATED_BRANCHES)
    bias, bias0 = _step_bias_rows(rel_bias, p)
    blk = pl.BlockSpec((1, N_HEADS_A, HEAD_DIM_A, p), lambda i: (i, 0, 0, 0))
    buf = jax.ShapeDtypeStruct(cache_kt.shape, f32)
    return pl.pallas_call(
        _attn_step_kernel,
        out_shape=(jax.ShapeDtypeStruct((D_A, nb), f32), buf, buf),
        grid=(nb,),
        in_specs=[
            pl.BlockSpec((nbr, N_HEADS_A, 1, p), lambda i: (0, 0, 0, 0)),
            pl.BlockSpec((nbr, N_HEADS_A, 1, 1), lambda i: (0, 0, 0, 0)),
            pl.BlockSpec((3 * D_A, nb), lambda i: (0, 0)),
            pl.BlockSpec((D_A, 1), lambda i: (0, 0)),
            blk, blk,
        ],
        out_specs=(pl.BlockSpec((D_A, nb), lambda i: (0, 0)), blk, blk),
        compiler_params=pltpu.CompilerParams(
            dimension_semantics=("arbitrary",),
            vmem_limit_bytes=VMEM_LIMIT_BYTES),
        name="attn_step",
    )(bias, bias0, zt, attn_out_g.reshape(D_A, 1), cache_kt, cache_vt)


N_ROWS = 3
ROW_W, ROW_A, ROW_R = range(N_ROWS)


def _mlstm_step_pre_kernel(xb_ref, gt_ref, gbias_ref, sc_ref, cw_ref, cb_ref, wq_ref, wk_ref, nt_ref, mt_ref,
                           cact_out, qt_out, kt_out, vt_out, vst_out, rows_out, nt_out, mt_out):
    E = HEAD_DIM_B
    x = xb_ref[...]
    c = cb_ref[...] + x * cw_ref[CONV_W - 1:CONV_W, :]
    for t in range(CONV_W - 1):
        c = c + sc_ref[t] * cw_ref[t:t + 1, :]
    c_act = c * jax.nn.sigmoid(c)
    cact_out[...] = c_act
    gts_t = (gt_ref[...] + gbias_ref[...]).T
    logf_t = _log_sigmoid(gts_t)
    for h in range(N_HEADS_B):
        sl = slice(h * HEAD_PAD, (h + 1) * HEAD_PAD)
        ch_bf = c_act[:, sl].astype(bf16)
        q = jnp.dot(ch_bf, wq_ref[h], preferred_element_type=f32)
        k = jnp.dot(ch_bf, wk_ref[h], preferred_element_type=f32) * (1.0 / math.sqrt(HEAD_DIM_B))
        qt = q.T[:E]
        kt = k.T[:E]
        vt = x[:, sl].T[:E]
        i_pre = gts_t[h:h + 1]
        a = logf_t[N_HEADS_B + h:N_HEADS_B + h + 1]
        m_old = mt_ref[h:h + 1]
        m_t = jnp.maximum(a + m_old, i_pre)
        w_state = jnp.exp(a + m_old - m_t)
        w_in = jnp.exp(i_pre - m_t)
        amat = jnp.sum(qt * kt, axis=0, keepdims=True) * w_in
        n_old = nt_ref[h]
        den = w_state * jnp.sum(n_old * qt, axis=0, keepdims=True) + amat
        nt_out[h] = w_state * n_old + w_in * kt
        mt_out[h:h + 1] = m_t
        qt_out[h] = qt
        kt_out[h] = kt
        vt_out[h] = vt
        vst_out[h] = w_in * vt
        rows_out[h, ROW_W:ROW_W + 1] = w_state
        rows_out[h, ROW_A:ROW_A + 1] = amat
        rows_out[h, ROW_R:ROW_R + 1] = 1.0 / jnp.maximum(jnp.abs(den), jnp.exp(-m_t))


def _mlstm_step_pre(xb, gts, gbias, sc_t, cw, cb, wq, wk, nt, mt):
    nb = xb.shape[0]
    hd = jax.ShapeDtypeStruct((N_HEADS_B, HEAD_DIM_B, nb), f32)
    return pl.pallas_call(
        _mlstm_step_pre_kernel,
        out_shape=(jax.ShapeDtypeStruct((nb, D_B_PAD), f32), hd, hd, hd, hd,
                   jax.ShapeDtypeStruct((N_HEADS_B, N_ROWS, nb), f32), hd,
                   jax.ShapeDtypeStruct((N_HEADS_B, nb), f32)),
        compiler_params=pltpu.CompilerParams(vmem_limit_bytes=VMEM_LIMIT_BYTES),
        name="mlstm_step_pre",
    )(xb, gts, gbias, sc_t, cw, cb, wq, wk, nt, mt)


ST_VT = 40


def _mlstm_step_state_kernel(c_ref, qt_ref, kt_ref, vt_ref, vst_ref, rows_ref, ob_ref, cact_ref, mg_ref,
                             skip_ref, c_out, out_ref, cq_scr):
    vb = pl.program_id(1)
    E = HEAD_DIM_B
    qt = qt_ref[0]
    kt = kt_ref[0]
    w_state = rows_ref[0, ROW_W:ROW_W + 1]

    def per_row(v, carry):
        c_old = c_ref[0, v]
        vs = vst_ref[0, pl.ds(vb * ST_VT + v, 1), :]
        c_out[0, v] = w_state * c_old + vs * kt
        cq_scr[pl.ds(vb * ST_VT + v, 1), :] = jnp.sum(c_old * qt, axis=0, keepdims=True)
        return carry

    lax.fori_loop(0, ST_VT, per_row, 0)

    @pl.when(vb == pl.num_programs(1) - 1)
    def _finish_head():
        ht = (w_state * cq_scr[...] + rows_ref[0, ROW_A:ROW_A + 1] * vt_ref[0]) * rows_ref[0, ROW_R:ROW_R + 1]
        hh = jnp.concatenate([ht, jnp.zeros((HEAD_PAD - E, ht.shape[1]), f32)], axis=0).T
        hn = hh * lax.rsqrt(jnp.sum(hh * hh, axis=-1, keepdims=True) * (1.0 / E) + EPS) * mg_ref[0]
        out_ref[0] = jax.nn.sigmoid(ob_ref[0]) * (hn + skip_ref[0] * cact_ref[0])


def _mlstm_step_state(ct, qt, kt, vt, vst, rows, ob3, cact3, mg3, skip3):
    nh, e, _, nb = ct.shape
    assert e % ST_VT == 0
    head = lambda shape: pl.BlockSpec((1,) + shape, lambda h, v: (h,) + (0,) * len(shape))
    cblk = pl.BlockSpec((1, ST_VT, e, nb), lambda h, v: (h, v, 0, 0))
    return pl.pallas_call(
        _mlstm_step_state_kernel,
        out_shape=(jax.ShapeDtypeStruct(ct.shape, f32), jax.ShapeDtypeStruct((nh, nb, HEAD_PAD), f32)),
        grid=(nh, e // ST_VT),
        in_specs=[cblk, head((e, nb)), head((e, nb)), head((e, nb)), head((e, nb)), head((N_ROWS, nb)),
                  head((nb, HEAD_PAD)), head((nb, HEAD_PAD)), head((1, HEAD_PAD)), head((1, HEAD_PAD))],
        out_specs=(cblk, head((nb, HEAD_PAD))),
        scratch_shapes=[pltpu.VMEM((e, nb), f32)],
        compiler_params=pltpu.CompilerParams(
            dimension_semantics=("arbitrary", "arbitrary"),
            vmem_limit_bytes=VMEM_LIMIT_BYTES),
        name="mlstm_step_state",
    )(ct, qt, kt, vt, vst, rows, ob3, cact3, mg3, skip3)


def kernel(x_prompt, x_sample, cache_win_k, cache_win_v, state_conv, state_C, state_n, state_m,
           rel_bias, norm1_g, w_in, gate_bias, conv_w, conv_b, wq_head, wk_head, attn_out_g,
           mh_norm_g, skip, w_out, norm2_g, w_ff1, w_ff2, final_g):
    Bp, Sp, _ = x_prompt.shape
    Bs, Ss, _ = x_sample.shape
    assert Ss == 1
    g1 = norm1_g[0].reshape(1, D_MODEL)
    g2 = norm2_g[0].reshape(1, D_MODEL)
    gf = final_g.reshape(1, D_MODEL)
    w_in_pad = _pad_w_in(w_in[0]).astype(bf16)
    wo = jnp.concatenate([w_out[0][:D_A], jnp.zeros((LANES, D_MODEL), f32),
                          _pad_heads(w_out[0][D_A:].T).T], axis=0).astype(bf16)
    w1 = w_ff1[0].astype(bf16)
    w2 = w_ff2[0].astype(bf16)
    hpad = HEAD_PAD - HEAD_DIM_B
    wq_p = jnp.pad(wq_head[0], ((0, 0), (0, hpad), (0, hpad))).astype(bf16)
    wk_p = jnp.pad(wk_head[0], ((0, 0), (0, hpad), (0, hpad))).astype(bf16)
    gbias = jnp.pad(gate_bias[0], (0, LANES - N_GATES)).reshape(1, LANES)
    cw_p = _pad_heads(conv_w[0])
    cb_p = _pad_heads(conv_b[0]).reshape(1, D_B_PAD)
    mg_p = _pad_heads(mh_norm_g[0]).reshape(1, D_B_PAD)
    skip_p = _pad_heads(skip[0]).reshape(1, D_B_PAD)

    xp2 = x_prompt.reshape(Bp * Sp, D_MODEL)
    slabs, xb, ob, gts = _norm_inproj(xp2, g1, w_in_pad, tm=512)
    out_a = _attn_prompt(slabs, rel_bias, attn_out_g[0], Bp, Sp)
    out_b, c_p, n_p, m_p, tail_p = _mlstm_prompt(xb, ob, gts, gbias, cw_p, cb_p, wq_p, wk_p, mg_p,
                                                 skip_p, Bp, Sp)
    y_p = _out_ffn(xp2, out_a, out_b, wo, g2, w1, w2, gf, tm=512).reshape(Bp, Sp, D_MODEL)
    P = min(WINDOW_MAX, Sp)
    kv = slabs.reshape(3, N_PAIRS, Bp, Sp, LANES)[1:, :, :, Sp - P:]
    kv = kv.transpose(0, 2, 3, 1, 4).reshape(2, Bp, P, N_HEADS_A, HEAD_DIM_A)
    st_p = (kv[0][None], kv[1][None], _unpad_heads(tail_p[:, SUBLANES - (CONV_W - 1):])[None],
            c_p[:, :, :HEAD_DIM_B, :HEAD_DIM_B][None], n_p[:, :, 0, :HEAD_DIM_B][None],
            m_p[:, :, 0, 0][None])

    xs2 = x_sample.reshape(Bs, D_MODEL)
    zt_s, xb_s, ob_s, gts_s = _norm_inproj_step(xs2, g1, w_in[0][:, :3 * D_A].T.astype(bf16), w_in_pad)
    oat_s, nkt, nvt = _attn_step(zt_s, cache_win_k[0].transpose(0, 2, 3, 1),
                                 cache_win_v[0].transpose(0, 2, 3, 1), rel_bias, attn_out_g[0])
    oa_s = oat_s.T
    new_k = nkt.transpose(0, 3, 1, 2)[None]
    new_v = nvt.transpose(0, 3, 1, 2)[None]
    sc_t = _pad_heads(state_conv[0].transpose(1, 0, 2))
    cact_s, qt_s, kt_s, vt_s, vst_s, rows_s, nt_s, mt_s = _mlstm_step_pre(
        xb_s, gts_s, gbias, sc_t, cw_p, cb_p, wq_p, wk_p,
        state_n[0].transpose(1, 2, 0), state_m[0].T)
    heads3 = lambda t: t.reshape(t.shape[0], N_HEADS_B, HEAD_PAD).transpose(1, 0, 2)
    ct_s, outb3 = _mlstm_step_state(state_C[0].transpose(1, 2, 3, 0), qt_s, kt_s, vt_s, vst_s, rows_s,
                                    heads3(ob_s), heads3(cact_s), heads3(mg_p), heads3(skip_p))
    outb_s = outb3.transpose(1, 0, 2).reshape(Bs, D_B_PAD)
    y_s = _out_ffn(xs2, oa_s.reshape(Bs, D_A), outb_s, wo, g2, w1, w2, gf,
                   tm=Bs).reshape(Bs, Ss, D_MODEL)
    new_conv = jnp.concatenate([state_conv[0][:, 1:], _unpad_heads(xb_s)[:, None]], axis=1)
    st_s = (new_k, new_v, new_conv[None], ct_s.transpose(3, 0, 1, 2)[None],
            nt_s.transpose(2, 0, 1)[None], mt_s.T[None])
    return (y_p, y_s) + st_p + st_s
```

```python
import functools
import math

import jax
import jax.numpy as jnp
from jax import lax
from jax.experimental import pallas as pl
from jax.experimental.pallas import tpu as pltpu

D_MODEL = 1024
HEAD_DIM_A = 64
N_HEADS_A = 6
D_A = N_HEADS_A * HEAD_DIM_A
D_B = D_MODEL - D_A
N_HEADS_B = 4
HEAD_DIM_B = D_B // N_HEADS_B
DILATED_BRANCHES = ((128, 1), (512, 4), (2048, 16))
WINDOW_MAX = 2048
N_BUCKETS = 32
REL_MAX_DIST = 2048
CONV_W = 4
MLSTM_CHUNK = 128
D_FF = 4 * D_MODEL
N_GATES = 2 * N_HEADS_B
D_IN = 3 * D_A + 2 * D_B + N_GATES
SPLITS = [D_A, 2 * D_A, 3 * D_A, 3 * D_A + D_B, 3 * D_A + 2 * D_B]
EPS = 1e-6
NEG = -1e30

LANES = 128
SUBLANES = 8
VMEM_LIMIT_BYTES = 56 * 1024 * 1024

N_PAIRS = D_A // LANES
N_SLABS = 3 * N_PAIRS
HEAD_PAD = 2 * LANES
D_B_PAD = N_HEADS_B * HEAD_PAD
D_IN_PAD = 3 * D_A + 2 * D_B_PAD + LANES
D_O_PAD = D_A + LANES + D_B_PAD
BLK = 128
RES = DILATED_BRANCHES[-1][1]
Q_SUPER = BLK * RES

f32 = jnp.float32
bf16 = jnp.bfloat16


def _round_up(n, m):
    return -(-n // m) * m


def _rms(xf, g):
    return xf * lax.rsqrt(jnp.mean(xf * xf, axis=-1, keepdims=True) + EPS) * g


def _pad_heads(t):
    t = t.reshape(t.shape[:-1] + (N_HEADS_B, HEAD_DIM_B))
    t = jnp.pad(t, [(0, 0)] * (t.ndim - 1) + [(0, HEAD_PAD - HEAD_DIM_B)])
    return t.reshape(t.shape[:-2] + (D_B_PAD,))


def _unpad_heads(t):
    t = t.reshape(t.shape[:-1] + (N_HEADS_B, HEAD_PAD))[..., :HEAD_DIM_B]
    return t.reshape(t.shape[:-2] + (D_B,))


def _norm_inproj_kernel(x_ref, g_ref, w_ref, slab_ref, xb_ref, ob_ref, gt_ref):
    h = _rms(x_ref[...], g_ref[...]).astype(bf16)
    qkv = jnp.dot(h, w_ref[:, 0:3 * D_A], preferred_element_type=f32)
    for j in range(N_SLABS):
        slab_ref[j] = qkv[:, j * LANES:(j + 1) * LANES]
    o = 3 * D_A
    xb_ref[...] = jnp.dot(h, w_ref[:, o:o + D_B_PAD], preferred_element_type=f32)
    o += D_B_PAD
    ob_ref[...] = jnp.dot(h, w_ref[:, o:o + D_B_PAD], preferred_element_type=f32)
    o += D_B_PAD
    gt_ref[...] = jnp.dot(h, w_ref[:, o:o + LANES], preferred_element_type=f32)


def _norm_inproj(x, g, w_pad_bf16, tm):
    m, d = x.shape
    assert m % tm == 0
    row = lambda w: pl.BlockSpec((tm, w), lambda i: (i, 0))
    return pl.pallas_call(
        _norm_inproj_kernel,
        out_shape=(
            jax.ShapeDtypeStruct((N_SLABS, m, LANES), f32),
            jax.ShapeDtypeStruct((m, D_B_PAD), f32),
            jax.ShapeDtypeStruct((m, D_B_PAD), f32),
            jax.ShapeDtypeStruct((m, LANES), f32),
        ),
        grid=(m // tm,),
        in_specs=[
            row(d),
            pl.BlockSpec((1, d), lambda i: (0, 0)),
            pl.BlockSpec((d, D_IN_PAD), lambda i: (0, 0), pipeline_mode=pl.Buffered(1)),
        ],
        out_specs=(
            pl.BlockSpec((N_SLABS, tm, LANES), lambda i: (0, i, 0)),
            row(D_B_PAD), row(D_B_PAD), row(LANES),
        ),
        compiler_params=pltpu.CompilerParams(
            dimension_semantics=("arbitrary",),
            vmem_limit_bytes=VMEM_LIMIT_BYTES),
        name="norm_inproj",
    )(x, g, w_pad_bf16)


def _norm_inproj_step_kernel(x_ref, g_ref, wt_ref, w_ref, zt_ref, xb_ref, ob_ref, gt_ref):
    h = _rms(x_ref[...], g_ref[...]).astype(bf16)
    zt_ref[...] = lax.dot_general(wt_ref[...], h, (((1,), (1,)), ((), ())), preferred_element_type=f32)
    o = 3 * D_A
    xb_ref[...] = jnp.dot(h, w_ref[:, o:o + D_B_PAD], preferred_element_type=f32)
    o += D_B_PAD
    ob_ref[...] = jnp.dot(h, w_ref[:, o:o + D_B_PAD], preferred_element_type=f32)
    o += D_B_PAD
    gt_ref[...] = jnp.dot(h, w_ref[:, o:o + LANES], preferred_element_type=f32)


def _norm_inproj_step(x, g, wqkv_t_bf16, w_pad_bf16):
    nb = x.shape[0]
    wide = jax.ShapeDtypeStruct((nb, D_B_PAD), f32)
    return pl.pallas_call(
        _norm_inproj_step_kernel,
        out_shape=(jax.ShapeDtypeStruct((3 * D_A, nb), f32), wide, wide,
                   jax.ShapeDtypeStruct((nb, LANES), f32)),
        compiler_params=pltpu.CompilerParams(vmem_limit_bytes=VMEM_LIMIT_BYTES),
        name="norm_inproj_step",
    )(x, g, wqkv_t_bf16, w_pad_bf16)


def _pad_w_in(w_in):
    qkv = w_in[:, :3 * D_A]
    xb = _pad_heads(w_in[:, SPLITS[2]:SPLITS[3]])
    ob = _pad_heads(w_in[:, SPLITS[3]:SPLITS[4]])
    gt = jnp.pad(w_in[:, SPLITS[4]:], ((0, 0), (0, LANES - N_GATES)))
    return jnp.concatenate([qkv, xb, ob, gt], axis=1)


def _out_ffn_kernel(x_ref, oa_ref, ob_ref, wo_ref, g2_ref, w1_ref, w2_ref, gf_ref, y_ref, *, ff_chunk):
    o = jnp.concatenate([oa_ref[p].astype(bf16) for p in range(N_PAIRS)]
                        + [jnp.zeros((ob_ref.shape[0], LANES), bf16), ob_ref[...].astype(bf16)], axis=1)
    x1 = x_ref[...] + jnp.dot(o, wo_ref[...], preferred_element_type=f32)
    h2 = _rms(x1, g2_ref[...]).astype(bf16)
    acc = x1
    for c in range(D_FF // ff_chunk):
        u = jnp.dot(h2, w1_ref[:, c * ff_chunk:(c + 1) * ff_chunk], preferred_element_type=f32)
        u = jnp.square(jnp.maximum(u, 0.0)).astype(bf16)
        acc = acc + jnp.dot(u, w2_ref[c * ff_chunk:(c + 1) * ff_chunk, :],
                            preferred_element_type=f32)
    y_ref[...] = _rms(acc, gf_ref[...])


def _out_ffn(x, oa, ob, wo, g2, w1, w2, gf, tm, ff_chunk=1024):
    m, d = x.shape
    assert wo.shape[0] == D_O_PAD
    const = lambda i: (0, 0)
    single = dict(pipeline_mode=pl.Buffered(1))
    row = lambda w: pl.BlockSpec((tm, w), lambda i: (i, 0))
    return pl.pallas_call(
        functools.partial(_out_ffn_kernel, ff_chunk=ff_chunk),
        out_shape=jax.ShapeDtypeStruct((m, d), f32),
        grid=(m // tm,),
        in_specs=[
            row(d), pl.BlockSpec((N_PAIRS, tm, LANES), lambda i: (0, i, 0)), row(D_B_PAD),
            pl.BlockSpec((D_O_PAD, d), const, **single),
            pl.BlockSpec((1, d), const),
            pl.BlockSpec((d, D_FF), const, **single),
            pl.BlockSpec((D_FF, d), const, **single),
            pl.BlockSpec((1, d), const),
        ],
        out_specs=row(d),
        compiler_params=pltpu.CompilerParams(
            dimension_semantics=("arbitrary",),
            vmem_limit_bytes=VMEM_LIMIT_BYTES),
        name="outproj_ffn",
    )(x, oa, ob, wo, g2, w1, w2, gf)


def _t5_bucket(dist):
    max_exact = N_BUCKETS // 2
    df = jnp.maximum(dist, 1).astype(jnp.float32)
    large = max_exact + (jnp.log(df / max_exact) / math.log(REL_MAX_DIST / max_exact)
                         * (N_BUCKETS - max_exact)).astype(jnp.int32)
    large = jnp.minimum(large, N_BUCKETS - 1)
    return jnp.where(dist < max_exact, dist, large)


def _prompt_bias_table(rel_bias):
    period = 3 * BLK + 1
    r = jnp.arange(period)
    t = jnp.where(r < 2 * BLK, r, r - period)
    j = BLK - t
    band = (j >= 0) & (j <= BLK)
    tabs = []
    for (_, d) in DILATED_BRANCHES:
        vals = rel_bias[_t5_bucket(jnp.clip(j, 0, None) * d)].astype(f32)
        w = jnp.where(band[:, None], vals, NEG).T
        flat = jnp.tile(w, (1, BLK))[:, :BLK * (period - 1)]
        toe = flat.reshape(N_HEADS_A, BLK, period - 1)[:, :, :2 * BLK]
        sub = RES // d
        toe = toe.reshape(N_HEADS_A, BLK // sub, sub, 2, BLK // sub, sub)
        toe = toe.transpose(0, 2, 1, 3, 5, 4).reshape(N_HEADS_A, BLK, 2 * BLK)
        tabs.append(toe.reshape(N_PAIRS, 2 * BLK, 2 * BLK))
    return jnp.stack(tabs)


def _attn_prompt_kernel(bias_ref, q_ref, k_ref, v_ref, g_ref, o_ref, q_scr, kv_scr, m_scr, l_scr, acc_scr):
    n = pl.program_id(1)
    step = pl.program_id(0) * pl.num_programs(1) + n
    cur = lax.rem(step, 2)
    prv = 1 - cur

    @pl.when(step == 0)
    def _init():
        kv_scr[1] = jnp.zeros(kv_scr.shape[1:], f32)

    def deinterleave(r, carry):
        rows = pl.ds(pl.multiple_of(r * BLK, BLK), BLK)
        for p in range(N_PAIRS):
            q_scr[p, rows, :] = q_ref.at[p][pl.ds(r, BLK, stride=RES), :]
            kv_scr[cur, p, rows, :] = k_ref.at[p][pl.ds(r, BLK, stride=RES), :]
            kv_scr[cur, N_PAIRS + p, rows, :] = v_ref.at[p][pl.ds(r, BLK, stride=RES), :]
        return carry

    lax.fori_loop(0, RES, deinterleave, 0, unroll=4)

    lane = lax.broadcasted_iota(jnp.int32, (BLK, LANES), 1)
    low = lane < HEAD_DIM_A
    kcol = lax.broadcasted_iota(jnp.int32, (2 * BLK, 2 * BLK), 1)
    first_extra = jnp.where((kcol < BLK) & (n == 0), NEG, 0.0).astype(f32)

    def job(g, d, buf_a, off_a, off_b, res, extra):
        sub = RES // d
        run = BLK // sub

        def starts(off):
            return [pl.multiple_of((c * d + res) * BLK + off, SUBLANES) for c in range(sub)]

        def gather(ref, lead, off):
            return jnp.concatenate([ref[lead + (pl.ds(s, run), slice(None))] for s in starts(off)], axis=0)

        def scatter(ref, p, val):
            for c, s in enumerate(starts(off_b)):
                ref[p, pl.ds(s, run), :] = val[c * run:(c + 1) * run]

        for p in range(N_PAIRS):
            q2 = gather(q_scr, (p,), off_b) * (1.0 / math.sqrt(HEAD_DIM_A))
            qcat = jnp.concatenate([jnp.where(low, q2, 0.0), jnp.where(low, 0.0, q2)], axis=0)
            kk = jnp.concatenate([gather(kv_scr, (buf_a, p), off_a), gather(kv_scr, (cur, p), off_b)], axis=0)
            vv = jnp.concatenate([gather(kv_scr, (buf_a, N_PAIRS + p), off_a),
                                  gather(kv_scr, (cur, N_PAIRS + p), off_b)], axis=0)
            s = lax.dot_general(qcat.astype(bf16), kk.astype(bf16), (((1,), (1,)), ((), ())),
                                preferred_element_type=f32)
            s = s + bias_ref[g, p]
            if extra is not None:
                s = s + extra
            m = jnp.max(s, axis=-1, keepdims=True)
            e = jnp.exp(s - m)
            l = jnp.sum(e, axis=-1, keepdims=True)
            o = jnp.dot(e.astype(bf16), vv.astype(bf16), preferred_element_type=f32)
            o_t = jnp.where(low, o[:BLK], o[BLK:])
            m_t = jnp.where(low, m[:BLK], m[BLK:])
            l_t = jnp.where(low, l[:BLK], l[BLK:])
            if g == 0:
                scatter(acc_scr, p, o_t)
                scatter(m_scr, p, m_t)
                scatter(l_scr, p, l_t)
            else:
                m_o = gather(m_scr, (p,), off_b)
                m_n = jnp.maximum(m_o, m_t)
                al = jnp.exp(m_o - m_n)
                be = jnp.exp(m_t - m_n)
                scatter(acc_scr, p, al * gather(acc_scr, (p,), off_b) + be * o_t)
                scatter(l_scr, p, al * gather(l_scr, (p,), off_b) + be * l_t)
                scatter(m_scr, p, m_n)

    for g, (_, d) in enumerate(DILATED_BRANCHES):
        nu = Q_SUPER // (BLK * d)
        run = BLK * d // RES
        log_d = d.bit_length() - 1

        def first(res, carry, g=g, d=d, nu=nu, run=run):
            job(g, d, prv, (nu - 1) * run, 0, res, first_extra)
            return carry

        def rest(idx, carry, g=g, d=d, run=run, log_d=log_d):
            u = 1 + lax.shift_right_logical(idx, log_d)
            res = jnp.bitwise_and(idx, d - 1)
            job(g, d, cur, (u - 1) * run, u * run, res, None)
            return carry

        if d == 1:
            first(0, 0)
        else:
            lax.fori_loop(0, d, first, 0, unroll=4)
        if nu > 1:
            lax.fori_loop(0, (nu - 1) * d, rest, 0, unroll=5 if d == 1 else 4)

    def fin(r, carry):
        rows = pl.ds(pl.multiple_of(r * BLK, BLK), BLK)
        os_ = [acc_scr[p, rows, :] / l_scr[p, rows, :] for p in range(N_PAIRS)]
        ss = sum(jnp.sum(o * o, axis=-1, keepdims=True) for o in os_)
        sc = lax.rsqrt(ss * (1.0 / D_A) + EPS)
        for p in range(N_PAIRS):
            o_ref.at[p][pl.ds(r, BLK, stride=RES), :] = os_[p] * sc * g_ref[:, p * LANES:(p + 1) * LANES]
        return carry

    lax.fori_loop(0, RES, fin, 0, unroll=4)


def _attn_prompt(slabs, rel_bias, attn_out_g, batch, seq):
    assert seq % Q_SUPER == 0
    nsb = seq // Q_SUPER
    nbr = len(DILATED_BRANCHES)
    slab = lambda grp: pl.BlockSpec((N_PAIRS, Q_SUPER, LANES), lambda bi, ni: (grp, bi * nsb + ni, 0))
    return pl.pallas_call(
        _attn_prompt_kernel,
        out_shape=jax.ShapeDtypeStruct((N_PAIRS, batch * seq, LANES), f32),
        grid=(batch, nsb),
        in_specs=[
            pl.BlockSpec((nbr, N_PAIRS, 2 * BLK, 2 * BLK), lambda bi, ni: (0, 0, 0, 0),
                         pipeline_mode=pl.Buffered(1)),
            slab(0), slab(1), slab(2),
            pl.BlockSpec((1, D_A), lambda bi, ni: (0, 0)),
        ],
        out_specs=pl.BlockSpec((N_PAIRS, Q_SUPER, LANES), lambda bi, ni: (0, bi * nsb + ni, 0)),
        scratch_shapes=[
            pltpu.VMEM((N_PAIRS, Q_SUPER, LANES), f32),
            pltpu.VMEM((2, 2 * N_PAIRS, Q_SUPER, LANES), f32),
            pltpu.VMEM((N_PAIRS, Q_SUPER, LANES), f32),
            pltpu.VMEM((N_PAIRS, Q_SUPER, LANES), f32),
            pltpu.VMEM((N_PAIRS, Q_SUPER, LANES), f32),
        ],
        compiler_params=pltpu.CompilerParams(
            dimension_semantics=("arbitrary", "arbitrary"),
            vmem_limit_bytes=VMEM_LIMIT_BYTES),
        name="attn_prompt",
    )(_prompt_bias_table(rel_bias), slabs, slabs, slabs, attn_out_g.reshape(1, D_A))


def _log_sigmoid(x):
    return -(jnp.maximum(-x, 0.0) + jnp.log1p(jnp.exp(-jnp.abs(x))))


def _mlstm_prompt_kernel(xb_ref, ob_ref, gt_ref, gbias_ref, cw_ref, cb_ref, wq_ref, wk_ref, mg_ref,
                         skip_ref, out_ref, c_out_ref, n_out_ref, m_out_ref, conv_out_ref,
                         conv_scr, c_scr, n_scr, m_scr):
    c_idx = pl.program_id(1)
    L = MLSTM_CHUNK

    @pl.when(c_idx == 0)
    def _init():
        conv_scr[0:SUBLANES, :] = jnp.zeros((SUBLANES, D_B_PAD), f32)
        c_scr[...] = jnp.zeros_like(c_scr)
        n_scr[...] = jnp.zeros_like(n_scr)
        m_scr[...] = jnp.zeros_like(m_scr)

    x = xb_ref[...]
    conv_scr[SUBLANES:SUBLANES + L, :] = x
    c = cb_ref[...] + x * cw_ref[CONV_W - 1:CONV_W, :]
    for i in range(CONV_W - 1):
        sh = CONV_W - 1 - i
        c = c + conv_scr[SUBLANES - sh:SUBLANES - sh + L, :] * cw_ref[i:i + 1, :]
    conv_scr[0:SUBLANES, :] = x[L - SUBLANES:, :]
    c_act = c * jax.nn.sigmoid(c)

    gts = gt_ref[...] + gbias_ref[...]
    logf = _log_sigmoid(gts)
    row = lax.broadcasted_iota(jnp.int32, (L, L), 0)
    col = lax.broadcasted_iota(jnp.int32, (L, L), 1)
    causal = row >= col
    a_all = jnp.dot(causal.astype(f32), logf, precision=lax.Precision.HIGHEST,
                    preferred_element_type=f32)
    gts_t = gts.T
    a_t = a_all.T

    for h in range(N_HEADS_B):
        sl = slice(h * HEAD_PAD, (h + 1) * HEAD_PAD)
        ch = c_act[:, sl]
        ch_bf = ch.astype(bf16)
        q = jnp.dot(ch_bf, wq_ref[h], preferred_element_type=f32)
        k = jnp.dot(ch_bf, wk_ref[h], preferred_element_type=f32) * (1.0 / math.sqrt(HEAD_DIM_B))
        v = x[:, sl]
        q_bf, k_bf = q.astype(bf16), k.astype(bf16)
        a_col = a_all[:, N_HEADS_B + h:N_HEADS_B + h + 1]
        i_col = gts[:, h:h + 1]
        a_row = a_t[N_HEADS_B + h:N_HEADS_B + h + 1, :]
        i_row = gts_t[h:h + 1, :]
        m_prev = m_scr[h]
        dmat = jnp.where(causal, a_col - a_row + i_row, NEG)
        g = a_col + m_prev
        m_t = jnp.maximum(g, jnp.max(dmat, axis=-1, keepdims=True))
        w_state = jnp.exp(g - m_t)
        s = lax.dot_general(q_bf, k_bf, (((1,), (1,)), ((), ())), preferred_element_type=f32)
        amat = s * jnp.exp(dmat - m_t)
        c_state = c_scr[h]
        inter = lax.dot_general(q_bf, c_state.astype(bf16), (((1,), (1,)), ((), ())),
                                preferred_element_type=f32)
        num = w_state * inter + jnp.dot(amat.astype(bf16), v.astype(bf16),
                                        preferred_element_type=f32)
        n_row = n_scr[h]
        den = (w_state * jnp.sum(q * n_row, axis=-1, keepdims=True)
               + jnp.sum(amat, axis=-1, keepdims=True))
        hh = num / jnp.maximum(jnp.abs(den), jnp.exp(-m_t))
        hn = hh * lax.rsqrt(jnp.sum(hh * hh, axis=-1, keepdims=True) * (1.0 / HEAD_DIM_B) + EPS)
        hn = hn * mg_ref[:, sl]
        out_ref[:, sl] = jax.nn.sigmoid(ob_ref[:, sl]) * (hn + skip_ref[:, sl] * ch)

        b_tot = a_col[L - 1:L, :]
        wl = b_tot - a_col + i_col
        m_new = jnp.maximum(b_tot + m_prev, jnp.max(wl, axis=0, keepdims=True))
        wk = jnp.exp(wl - m_new)
        decay = jnp.exp(b_tot + m_prev - m_new)
        upd = lax.dot_general((wk * v).astype(bf16), k_bf, (((0,), (0,)), ((), ())),
                              preferred_element_type=f32)
        c_scr[h] = decay * c_state + upd
        n_scr[h] = decay * n_row + jnp.sum(wk * k, axis=0, keepdims=True)
        m_scr[h] = m_new

    @pl.when(c_idx == pl.num_programs(1) - 1)
    def _final():
        c_out_ref[0] = c_scr[...]
        n_out_ref[0] = n_scr[...]
        m_out_ref[0] = m_scr[...]
        conv_out_ref[0] = x[L - SUBLANES:, :]


def _mlstm_prompt(xb, ob, gates, gbias, cw, cb, wq, wk, mg, skip, batch, seq):
    L = MLSTM_CHUNK
    assert seq % L == 0
    nc = seq // L
    rows = lambda w: pl.BlockSpec((L, w), lambda bi, ci: (bi * nc + ci, 0))
    const2 = lambda shape: pl.BlockSpec(shape, lambda bi, ci: (0, 0))
    const3 = lambda shape: pl.BlockSpec(shape, lambda bi, ci: (0, 0, 0))
    state = lambda shape: pl.BlockSpec((1,) + shape, lambda bi, ci: (bi,) + (0,) * len(shape))
    return pl.pallas_call(
        _mlstm_prompt_kernel,
        out_shape=(
            jax.ShapeDtypeStruct((batch * seq, D_B_PAD), f32),
            jax.ShapeDtypeStruct((batch, N_HEADS_B, HEAD_PAD, HEAD_PAD), f32),
            jax.ShapeDtypeStruct((batch, N_HEADS_B, 1, HEAD_PAD), f32),
            jax.ShapeDtypeStruct((batch, N_HEADS_B, 1, 1), f32),
            jax.ShapeDtypeStruct((batch, SUBLANES, D_B_PAD), f32),
        ),
        grid=(batch, nc),
        in_specs=[rows(D_B_PAD), rows(D_B_PAD), rows(LANES), const2((1, LANES)),
                  const2((CONV_W, D_B_PAD)), const2((1, D_B_PAD)),
                  const3((N_HEADS_B, HEAD_PAD, HEAD_PAD)), const3((N_HEADS_B, HEAD_PAD, HEAD_PAD)),
                  const2((1, D_B_PAD)), const2((1, D_B_PAD))],
        out_specs=(rows(D_B_PAD), state((N_HEADS_B, HEAD_PAD, HEAD_PAD)),
                   state((N_HEADS_B, 1, HEAD_PAD)), state((N_HEADS_B, 1, 1)),
                   state((SUBLANES, D_B_PAD))),
        scratch_shapes=[
            pltpu.VMEM((SUBLANES + L, D_B_PAD), f32),
            pltpu.VMEM((N_HEADS_B, HEAD_PAD, HEAD_PAD), f32),
            pltpu.VMEM((N_HEADS_B, 1, HEAD_PAD), f32),
            pltpu.VMEM((N_HEADS_B, 1, 1), f32),
        ],
        compiler_params=pltpu.CompilerParams(
            dimension_semantics=("arbitrary", "arbitrary"),
            vmem_limit_bytes=VMEM_LIMIT_BYTES),
        name="mlstm_prompt",
    )(xb, ob, gates, gbias, cw, cb, wq, wk, mg, skip)


def _step_bias_rows(rel_bias, p):
    dist = p - jnp.arange(p)
    full = rel_bias[_t5_bucket(dist)].astype(f32).T
    rows = []
    for (w, d) in DILATED_BRANCHES:
        in_branch = (dist % d == 0) & (dist <= w)
        rows.append(jnp.where(in_branch[None, :], full, NEG))
    zero = rel_bias[_t5_bucket(jnp.zeros((1,), jnp.int32))].astype(f32).T
    nbr = len(DILATED_BRANCHES)
    return jnp.stack(rows)[:, :, None, :], jnp.broadcast_to(zero[None, :, None, :], (nbr, N_HEADS_A, 1, 1))


def _attn_step_kernel(bias_ref, bias0_ref, zt_ref, g_ref, kt_ref, vt_ref, ot_ref, nk_ref, nv_ref):
    b = pl.program_id(0)
    p = kt_ref.shape[-1]
    nbr = len(DILATED_BRANCHES)
    scale = 1.0 / math.sqrt(HEAD_DIM_A)

    @pl.when(b == 0)
    def _init():
        ot_ref[...] = jnp.zeros_like(ot_ref)

    zt = zt_ref[...]
    is_b = lax.broadcasted_iota(jnp.int32, zt.shape, 1) == b
    col = jnp.sum(jnp.where(is_b, zt, 0.0), axis=1, keepdims=True)
    is_last = lax.broadcasted_iota(jnp.int32, (HEAD_DIM_A, p), 1) == p - 1
    o_cols = []
    for h in range(N_HEADS_A):
        q_col = col[h * HEAD_DIM_A:(h + 1) * HEAD_DIM_A] * scale
        kn_col = col[D_A + h * HEAD_DIM_A:D_A + (h + 1) * HEAD_DIM_A]
        vn_col = col[2 * D_A + h * HEAD_DIM_A:2 * D_A + (h + 1) * HEAD_DIM_A]
        kt = kt_ref[0, h]
        vt = vt_ref[0, h]
        nk_ref[0, h] = jnp.where(is_last, kn_col, pltpu.roll(kt, p - 1, axis=1))
        nv_ref[0, h] = jnp.where(is_last, vn_col, pltpu.roll(vt, p - 1, axis=1))
        s = jnp.sum(kt * q_col, axis=0, keepdims=True)
        s_new = jnp.sum(kn_col * q_col, axis=0, keepdims=True)
        sg = [s + bias_ref[g, h] for g in range(nbr)]
        s0 = [s_new + bias0_ref[g, h] for g in range(nbr)]
        m = functools.reduce(jnp.maximum, [jnp.max(x, axis=1, keepdims=True) for x in sg] + s0)
        pw = sum(jnp.exp(x - m) for x in sg)
        e0 = sum(jnp.exp(x - m) for x in s0)
        l = jnp.sum(pw, axis=1, keepdims=True) + e0
        o_cols.append((jnp.sum(vt * pw, axis=1, keepdims=True) + e0 * vn_col) / l)
    o = jnp.concatenate(o_cols, axis=0)
    ssq = jnp.sum(o * o, axis=0, keepdims=True)
    o = o * lax.rsqrt(ssq * (1.0 / D_A) + EPS) * g_ref[...]
    sel = lax.broadcasted_iota(jnp.int32, ot_ref.shape, 1) == b
    ot_ref[...] = jnp.where(sel, o, ot_ref[...])


def _attn_step(zt, cache_kt, cache_vt, rel_bias, attn_out_g):
    nb, p = cache_kt.shape[0], cache_kt.shape[-1]
    for (w, d) in DILATED_BRANCHES:
        assert w <= p and w // d == BLK
    nbr = len(DILATED_BRANCHES)
    bias, bias0 = _step_bias_rows(rel_bias, p)
    blk = pl.BlockSpec((1, N_HEADS_A, HEAD_DIM_A, p), lambda i: (i, 0, 0, 0))
    buf = jax.ShapeDtypeStruct(cache_kt.shape, f32)
    return pl.pallas_call(
        _attn_step_kernel,
        out_shape=(jax.ShapeDtypeStruct((D_A, nb), f32), buf, buf),
        grid=(nb,),
        in_specs=[
            pl.BlockSpec((nbr, N_HEADS_A, 1, p), lambda i: (0, 0, 0, 0)),
            pl.BlockSpec((nbr, N_HEADS_A, 1, 1), lambda i: (0, 0, 0, 0)),
            pl.BlockSpec((3 * D_A, nb), lambda i: (0, 0)),
            pl.BlockSpec((D_A, 1), lambda i: (0, 0)),
            blk, blk,
        ],
        out_specs=(pl.BlockSpec((D_A, nb), lambda i: (0, 0)), blk, blk),
        compiler_params=pltpu.CompilerParams(
            dimension_semantics=("arbitrary",),
            vmem_limit_bytes=VMEM_LIMIT_BYTES),
        name="attn_step",
    )(bias, bias0, zt, attn_out_g.reshape(D_A, 1), cache_kt, cache_vt)


N_ROWS = 3
ROW_W, ROW_A, ROW_R = range(N_ROWS)


def _mlstm_step_pre_kernel(xb_ref, gt_ref, gbias_ref, sc_ref, cw_ref, cb_ref, wq_ref, wk_ref, nt_ref, mt_ref,
                           cact_out, qt_out, kt_out, vt_out, vst_out, rows_out, nt_out, mt_out):
    E = HEAD_DIM_B
    x = xb_ref[...]
    c = cb_ref[...] + x * cw_ref[CONV_W - 1:CONV_W, :]
    for t in range(CONV_W - 1):
        c = c + sc_ref[t] * cw_ref[t:t + 1, :]
    c_act = c * jax.nn.sigmoid(c)
    cact_out[...] = c_act
    gts_t = (gt_ref[...] + gbias_ref[...]).T
    logf_t = _log_sigmoid(gts_t)
    for h in range(N_HEADS_B):
        sl = slice(h * HEAD_PAD, (h + 1) * HEAD_PAD)
        ch_bf = c_act[:, sl].astype(bf16)
        q = jnp.dot(ch_bf, wq_ref[h], preferred_element_type=f32)
        k = jnp.dot(ch_bf, wk_ref[h], preferred_element_type=f32) * (1.0 / math.sqrt(HEAD_DIM_B))
        qt = q.T[:E]
        kt = k.T[:E]
        vt = x[:, sl].T[:E]
        i_pre = gts_t[h:h + 1]
        a = logf_t[N_HEADS_B + h:N_HEADS_B + h + 1]
        m_old = mt_ref[h:h + 1]
        m_t = jnp.maximum(a + m_old, i_pre)
        w_state = jnp.exp(a + m_old - m_t)
        w_in = jnp.exp(i_pre - m_t)
        amat = jnp.sum(qt * kt, axis=0, keepdims=True) * w_in
        n_old = nt_ref[h]
        den = w_state * jnp.sum(n_old * qt, axis=0, keepdims=True) + amat
        nt_out[h] = w_state * n_old + w_in * kt
        mt_out[h:h + 1] = m_t
        qt_out[h] = qt
        kt_out[h] = kt
        vt_out[h] = vt
        vst_out[h] = w_in * vt
        rows_out[h, ROW_W:ROW_W + 1] = w_state
        rows_out[h, ROW_A:ROW_A + 1] = amat
        rows_out[h, ROW_R:ROW_R + 1] = 1.0 / jnp.maximum(jnp.abs(den), jnp.exp(-m_t))


def _mlstm_step_pre(xb, gts, gbias, sc_t, cw, cb, wq, wk, nt, mt):
    nb = xb.shape[0]
    hd = jax.ShapeDtypeStruct((N_HEADS_B, HEAD_DIM_B, nb), f32)
    return pl.pallas_call(
        _mlstm_step_pre_kernel,
        out_shape=(jax.ShapeDtypeStruct((nb, D_B_PAD), f32), hd, hd, hd, hd,
                   jax.ShapeDtypeStruct((N_HEADS_B, N_ROWS, nb), f32), hd,
                   jax.ShapeDtypeStruct((N_HEADS_B, nb), f32)),
        compiler_params=pltpu.CompilerParams(vmem_limit_bytes=VMEM_LIMIT_BYTES),
        name="mlstm_step_pre",
    )(xb, gts, gbias, sc_t, cw, cb, wq, wk, nt, mt)


ST_VT = 40


def _mlstm_step_state_kernel(c_ref, qt_ref, kt_ref, vt_ref, vst_ref, rows_ref, ob_ref, cact_ref, mg_ref,
                             skip_ref, c_out, out_ref, cq_scr):
    vb = pl.program_id(1)
    E = HEAD_DIM_B
    qt = qt_ref[0]
    kt = kt_ref[0]
    w_state = rows_ref[0, ROW_W:ROW_W + 1]

    def per_row(v, carry):
        c_old = c_ref[0, v]
        vs = vst_ref[0, pl.ds(vb * ST_VT + v, 1), :]
        c_out[0, v] = w_state * c_old + vs * kt
        cq_scr[pl.ds(vb * ST_VT + v, 1), :] = jnp.sum(c_old * qt, axis=0, keepdims=True)
        return carry

    lax.fori_loop(0, ST_VT, per_row, 0)

    @pl.when(vb == pl.num_programs(1) - 1)
    def _finish_head():
        ht = (w_state * cq_scr[...] + rows_ref[0, ROW_A:ROW_A + 1] * vt_ref[0]) * rows_ref[0, ROW_R:ROW_R + 1]
        hh = jnp.concatenate([ht, jnp.zeros((HEAD_PAD - E, ht.shape[1]), f32)], axis=0).T
        hn = hh * lax.rsqrt(jnp.sum(hh * hh, axis=-1, keepdims=True) * (1.0 / E) + EPS) * mg_ref[0]
        out_ref[0] = jax.nn.sigmoid(ob_ref[0]) * (hn + skip_ref[0] * cact_ref[0])


def _mlstm_step_state(ct, qt, kt, vt, vst, rows, ob3, cact3, mg3, skip3):
    nh, e, _, nb = ct.shape
    assert e % ST_VT == 0
    head = lambda shape: pl.BlockSpec((1,) + shape, lambda h, v: (h,) + (0,) * len(shape))
    cblk = pl.BlockSpec((1, ST_VT, e, nb), lambda h, v: (h, v, 0, 0))
    return pl.pallas_call(
        _mlstm_step_state_kernel,
        out_shape=(jax.ShapeDtypeStruct(ct.shape, f32), jax.ShapeDtypeStruct((nh, nb, HEAD_PAD), f32)),
        grid=(nh, e // ST_VT),
        in_specs=[cblk, head((e, nb)), head((e, nb)), head((e, nb)), head((e, nb)), head((N_ROWS, nb)),
                  head((nb, HEAD_PAD)), head((nb, HEAD_PAD)), head((1, HEAD_PAD)), head((1, HEAD_PAD))],
        out_specs=(cblk, head((nb, HEAD_PAD))),
        scratch_shapes=[pltpu.VMEM((e, nb), f32)],
        compiler_params=pltpu.CompilerParams(
            dimension_semantics=("arbitrary", "arbitrary"),
            vmem_limit_bytes=VMEM_LIMIT_BYTES),
        name="mlstm_step_state",
    )(ct, qt, kt, vt, vst, rows, ob3, cact3, mg3, skip3)


def kernel(x_prompt, x_sample, cache_win_k, cache_win_v, state_conv, state_C, state_n, state_m,
           rel_bias, norm1_g, w_in, gate_bias, conv_w, conv_b, wq_head, wk_head, attn_out_g,
           mh_norm_g, skip, w_out, norm2_g, w_ff1, w_ff2, final_g):
    Bp, Sp, _ = x_prompt.shape
    Bs, Ss, _ = x_sample.shape
    assert Ss == 1
    g1 = norm1_g[0].reshape(1, D_MODEL)
    g2 = norm2_g[0].reshape(1, D_MODEL)
    gf = final_g.reshape(1, D_MODEL)
    w_in_pad = _pad_w_in(w_in[0]).astype(bf16)
    wo = jnp.concatenate([w_out[0][:D_A], jnp.zeros((LANES, D_MODEL), f32),
                          _pad_heads(w_out[0][D_A:].T).T], axis=0).astype(bf16)
    w1 = w_ff1[0].astype(bf16)
    w2 = w_ff2[0].astype(bf16)
    hpad = HEAD_PAD - HEAD_DIM_B
    wq_p = jnp.pad(wq_head[0], ((0, 0), (0, hpad), (0, hpad))).astype(bf16)
    wk_p = jnp.pad(wk_head[0], ((0, 0), (0, hpad), (0, hpad))).astype(bf16)
    gbias = jnp.pad(gate_bias[0], (0, LANES - N_GATES)).reshape(1, LANES)
    cw_p = _pad_heads(conv_w[0])
    cb_p = _pad_heads(conv_b[0]).reshape(1, D_B_PAD)
    mg_p = _pad_heads(mh_norm_g[0]).reshape(1, D_B_PAD)
    skip_p = _pad_heads(skip[0]).reshape(1, D_B_PAD)

    xp2 = x_prompt.reshape(Bp * Sp, D_MODEL)
    slabs, xb, ob, gts = _norm_inproj(xp2, g1, w_in_pad, tm=512)
    out_a = _attn_prompt(slabs, rel_bias, attn_out_g[0], Bp, Sp)
    out_b, c_p, n_p, m_p, tail_p = _mlstm_prompt(xb, ob, gts, gbias, cw_p, cb_p, wq_p, wk_p, mg_p,
                                                 skip_p, Bp, Sp)
    y_p = _out_ffn(xp2, out_a, out_b, wo, g2, w1, w2, gf, tm=512).reshape(Bp, Sp, D_MODEL)
    P = min(WINDOW_MAX, Sp)
    kv = slabs.reshape(3, N_PAIRS, Bp, Sp, LANES)[1:, :, :, Sp - P:]
    kv = kv.transpose(0, 2, 3, 1, 4).reshape(2, Bp, P, N_HEADS_A, HEAD_DIM_A)
    st_p = (kv[0][None], kv[1][None], _unpad_heads(tail_p[:, SUBLANES - (CONV_W - 1):])[None],
            c_p[:, :, :HEAD_DIM_B, :HEAD_DIM_B][None], n_p[:, :, 0, :HEAD_DIM_B][None],
            m_p[:, :, 0, 0][None])

    xs2 = x_sample.reshape(Bs, D_MODEL)
    zt_s, xb_s, ob_s, gts_s = _norm_inproj_step(xs2, g1, w_in[0][:, :3 * D_A].T.astype(bf16), w_in_pad)
    oat_s, nkt, nvt = _attn_step(zt_s, cache_win_k[0].transpose(0, 2, 3, 1),
                                 cache_win_v[0].transpose(0, 2, 3, 1), rel_bias, attn_out_g[0])
    oa_s = oat_s.reshape(N_PAIRS, LANES, Bs).transpose(0, 2, 1)
    new_k = nkt.transpose(0, 3, 1, 2)[None]
    new_v = nvt.transpose(0, 3, 1, 2)[None]
    sc_t = _pad_heads(state_conv[0].transpose(1, 0, 2))
    cact_s, qt_s, kt_s, vt_s, vst_s, rows_s, nt_s, mt_s = _mlstm_step_pre(
        xb_s, gts_s, gbias, sc_t, cw_p, cb_p, wq_p, wk_p,
        state_n[0].transpose(1, 2, 0), state_m[0].T)
    heads3 = lambda t: t.reshape(t.shape[0], N_HEADS_B, HEAD_PAD).transpose(1, 0, 2)
    ct_s, outb3 = _mlstm_step_state(state_C[0].transpose(1, 2, 3, 0), qt_s, kt_s, vt_s, vst_s, rows_s,
                                    heads3(ob_s), heads3(cact_s), heads3(mg_p), heads3(skip_p))
    outb_s = outb3.transpose(1, 0, 2).reshape(Bs, D_B_PAD)
    y_s = _out_ffn(xs2, oa_s, outb_s, wo, g2, w1, w2, gf, tm=Bs).reshape(Bs, Ss, D_MODEL)
    new_conv = jnp.concatenate([state_conv[0][:, 1:], _unpad_heads(xb_s)[:, None]], axis=1)
    st_s = (new_k, new_v, new_conv[None], ct_s.transpose(3, 0, 1, 2)[None],
            nt_s.transpose(2, 0, 1)[None], mt_s.T[None])
    return (y_p, y_s) + st_p + st_s
```

```python
import functools
import math

import jax
import jax.numpy as jnp
from jax import lax
from jax.experimental import pallas as pl
from jax.experimental.pallas import tpu as pltpu

D_MODEL = 1024
HEAD_DIM_A = 64
N_HEADS_A = 6
D_A = N_HEADS_A * HEAD_DIM_A
D_B = D_MODEL - D_A
N_HEADS_B = 4
HEAD_DIM_B = D_B // N_HEADS_B
DILATED_BRANCHES = ((128, 1), (512, 4), (2048, 16))
WINDOW_MAX = 2048
N_BUCKETS = 32
REL_MAX_DIST = 2048
CONV_W = 4
MLSTM_CHUNK = 128
D_FF = 4 * D_MODEL
N_GATES = 2 * N_HEADS_B
D_IN = 3 * D_A + 2 * D_B + N_GATES
SPLITS = [D_A, 2 * D_A, 3 * D_A, 3 * D_A + D_B, 3 * D_A + 2 * D_B]
EPS = 1e-6
NEG = -1e30

LANES = 128
SUBLANES = 8
VMEM_LIMIT_BYTES = 56 * 1024 * 1024

N_PAIRS = D_A // LANES
N_SLABS = 3 * N_PAIRS
HEAD_PAD = 2 * LANES
D_B_PAD = N_HEADS_B * HEAD_PAD
D_IN_PAD = 3 * D_A + 2 * D_B_PAD + LANES
D_O_PAD = D_A + LANES + D_B_PAD
BLK = 128
RES = DILATED_BRANCHES[-1][1]
Q_SUPER = BLK * RES

f32 = jnp.float32
bf16 = jnp.bfloat16


def _round_up(n, m):
    return -(-n // m) * m


def _rms(xf, g):
    return xf * lax.rsqrt(jnp.mean(xf * xf, axis=-1, keepdims=True) + EPS) * g


def _pad_heads(t):
    t = t.reshape(t.shape[:-1] + (N_HEADS_B, HEAD_DIM_B))
    t = jnp.pad(t, [(0, 0)] * (t.ndim - 1) + [(0, HEAD_PAD - HEAD_DIM_B)])
    return t.reshape(t.shape[:-2] + (D_B_PAD,))


def _unpad_heads(t):
    t = t.reshape(t.shape[:-1] + (N_HEADS_B, HEAD_PAD))[..., :HEAD_DIM_B]
    return t.reshape(t.shape[:-2] + (D_B,))


def _norm_inproj_kernel(x_ref, g_ref, w_ref, slab_ref, win_ref, xb_ref, ob_ref, gt_ref):
    h = _rms(x_ref[...], g_ref[...]).astype(bf16)
    qkv = jnp.dot(h, w_ref[:, 0:3 * D_A], preferred_element_type=f32)
    for j in range(N_SLABS):
        slab_ref[j] = qkv[:, j * LANES:(j + 1) * LANES]
    for j in range(N_PAIRS, N_SLABS):
        win_ref[j - N_PAIRS] = qkv[:, j * LANES:(j + 1) * LANES]
    o = 3 * D_A
    xb_ref[...] = jnp.dot(h, w_ref[:, o:o + D_B_PAD], preferred_element_type=f32)
    o += D_B_PAD
    ob_ref[...] = jnp.dot(h, w_ref[:, o:o + D_B_PAD], preferred_element_type=f32)
    o += D_B_PAD
    gt_ref[...] = jnp.dot(h, w_ref[:, o:o + LANES], preferred_element_type=f32)


def _norm_inproj(x, g, w_pad_bf16, tm, seq, win):
    m, d = x.shape
    assert m % seq == 0 and seq % tm == 0 and win % tm == 0
    tiles_seq, tiles_win = seq // tm, win // tm
    row = lambda w: pl.BlockSpec((tm, w), lambda i: (i, 0))

    def win_block(i):
        return (0, (i // tiles_seq) * tiles_win + jnp.maximum(i % tiles_seq - (tiles_seq - tiles_win), 0), 0)

    return pl.pallas_call(
        _norm_inproj_kernel,
        out_shape=(
            jax.ShapeDtypeStruct((N_SLABS, m, LANES), f32),
            jax.ShapeDtypeStruct((2 * N_PAIRS, (m // seq) * win, LANES), f32),
            jax.ShapeDtypeStruct((m, D_B_PAD), f32),
            jax.ShapeDtypeStruct((m, D_B_PAD), f32),
            jax.ShapeDtypeStruct((m, LANES), f32),
        ),
        grid=(m // tm,),
        in_specs=[
            row(d),
            pl.BlockSpec((1, d), lambda i: (0, 0)),
            pl.BlockSpec((d, D_IN_PAD), lambda i: (0, 0), pipeline_mode=pl.Buffered(1)),
        ],
        out_specs=(
            pl.BlockSpec((N_SLABS, tm, LANES), lambda i: (0, i, 0)),
            pl.BlockSpec((2 * N_PAIRS, tm, LANES), win_block),
            row(D_B_PAD), row(D_B_PAD), row(LANES),
        ),
        compiler_params=pltpu.CompilerParams(
            dimension_semantics=("arbitrary",),
            vmem_limit_bytes=VMEM_LIMIT_BYTES),
        name="norm_inproj",
    )(x, g, w_pad_bf16)


def _norm_inproj_step_kernel(x_ref, g_ref, wt_ref, w_ref, zt_ref, xb_ref, ob_ref, gt_ref):
    h = _rms(x_ref[...], g_ref[...]).astype(bf16)
    zt_ref[...] = lax.dot_general(wt_ref[...], h, (((1,), (1,)), ((), ())), preferred_element_type=f32)
    o = 3 * D_A
    xb_ref[...] = jnp.dot(h, w_ref[:, o:o + D_B_PAD], preferred_element_type=f32)
    o += D_B_PAD
    ob_ref[...] = jnp.dot(h, w_ref[:, o:o + D_B_PAD], preferred_element_type=f32)
    o += D_B_PAD
    gt_ref[...] = jnp.dot(h, w_ref[:, o:o + LANES], preferred_element_type=f32)


def _norm_inproj_step(x, g, wqkv_t_bf16, w_pad_bf16):
    nb = x.shape[0]
    wide = jax.ShapeDtypeStruct((nb, D_B_PAD), f32)
    return pl.pallas_call(
        _norm_inproj_step_kernel,
        out_shape=(jax.ShapeDtypeStruct((3 * D_A, nb), f32), wide, wide,
                   jax.ShapeDtypeStruct((nb, LANES), f32)),
        compiler_params=pltpu.CompilerParams(vmem_limit_bytes=VMEM_LIMIT_BYTES),
        name="norm_inproj_step",
    )(x, g, wqkv_t_bf16, w_pad_bf16)


def _pad_w_in(w_in):
    qkv = w_in[:, :3 * D_A]
    xb = _pad_heads(w_in[:, SPLITS[2]:SPLITS[3]])
    ob = _pad_heads(w_in[:, SPLITS[3]:SPLITS[4]])
    gt = jnp.pad(w_in[:, SPLITS[4]:], ((0, 0), (0, LANES - N_GATES)))
    return jnp.concatenate([qkv, xb, ob, gt], axis=1)


def _out_ffn_kernel(x_ref, oa_ref, ob_ref, wo_ref, g2_ref, w1_ref, w2_ref, gf_ref, y_ref, *, ff_chunk):
    o = jnp.concatenate([oa_ref[p].astype(bf16) for p in range(N_PAIRS)]
                        + [jnp.zeros((ob_ref.shape[0], LANES), bf16), ob_ref[...].astype(bf16)], axis=1)
    x1 = x_ref[...] + jnp.dot(o, wo_ref[...], preferred_element_type=f32)
    h2 = _rms(x1, g2_ref[...]).astype(bf16)
    acc = x1
    for c in range(D_FF // ff_chunk):
        u = jnp.dot(h2, w1_ref[:, c * ff_chunk:(c + 1) * ff_chunk], preferred_element_type=f32)
        u = jnp.square(jnp.maximum(u, 0.0)).astype(bf16)
        acc = acc + jnp.dot(u, w2_ref[c * ff_chunk:(c + 1) * ff_chunk, :],
                            preferred_element_type=f32)
    y_ref[...] = _rms(acc, gf_ref[...])


def _out_ffn(x, oa, ob, wo, g2, w1, w2, gf, tm, ff_chunk=1024):
    m, d = x.shape
    assert wo.shape[0] == D_O_PAD
    const = lambda i: (0, 0)
    single = dict(pipeline_mode=pl.Buffered(1))
    row = lambda w: pl.BlockSpec((tm, w), lambda i: (i, 0))
    return pl.pallas_call(
        functools.partial(_out_ffn_kernel, ff_chunk=ff_chunk),
        out_shape=jax.ShapeDtypeStruct((m, d), f32),
        grid=(m // tm,),
        in_specs=[
            row(d), pl.BlockSpec((N_PAIRS, tm, LANES), lambda i: (0, i, 0)), row(D_B_PAD),
            pl.BlockSpec((D_O_PAD, d), const, **single),
            pl.BlockSpec((1, d), const),
            pl.BlockSpec((d, D_FF), const, **single),
            pl.BlockSpec((D_FF, d), const, **single),
            pl.BlockSpec((1, d), const),
        ],
        out_specs=row(d),
        compiler_params=pltpu.CompilerParams(
            dimension_semantics=("arbitrary",),
            vmem_limit_bytes=VMEM_LIMIT_BYTES),
        name="outproj_ffn",
    )(x, oa, ob, wo, g2, w1, w2, gf)


def _t5_bucket(dist):
    max_exact = N_BUCKETS // 2
    df = jnp.maximum(dist, 1).astype(jnp.float32)
    large = max_exact + (jnp.log(df / max_exact) / math.log(REL_MAX_DIST / max_exact)
                         * (N_BUCKETS - max_exact)).astype(jnp.int32)
    large = jnp.minimum(large, N_BUCKETS - 1)
    return jnp.where(dist < max_exact, dist, large)


def _prompt_bias_table(rel_bias):
    period = 3 * BLK + 1
    r = jnp.arange(period)
    t = jnp.where(r < 2 * BLK, r, r - period)
    j = BLK - t
    band = (j >= 0) & (j <= BLK)
    tabs = []
    for (_, d) in DILATED_BRANCHES:
        vals = rel_bias[_t5_bucket(jnp.clip(j, 0, None) * d)].astype(f32)
        w = jnp.where(band[:, None], vals, NEG).T
        flat = jnp.tile(w, (1, BLK))[:, :BLK * (period - 1)]
        toe = flat.reshape(N_HEADS_A, BLK, period - 1)[:, :, :2 * BLK]
        sub = RES // d
        if sub > 1:
            pos = jnp.arange(BLK)
            nat = (pos % (BLK // sub)) * sub + pos // (BLK // sub)
            pq = jax.nn.one_hot(nat, BLK, dtype=f32)
            pk = jnp.kron(jnp.eye(2, dtype=f32), pq)
            toe = jnp.einsum('pq,hqk->hpk', pq, toe, precision=lax.Precision.HIGHEST)
            toe = jnp.einsum('hpk,jk->hpj', toe, pk, precision=lax.Precision.HIGHEST)
        tabs.append(toe.reshape(N_PAIRS, 2 * BLK, 2 * BLK))
    return jnp.stack(tabs)


def _attn_prompt_kernel(bias_ref, q_ref, k_ref, v_ref, g_ref, o_ref, q_scr, kv_scr, m_scr, l_scr, acc_scr):
    n = pl.program_id(1)
    step = pl.program_id(0) * pl.num_programs(1) + n
    cur = lax.rem(step, 2)
    prv = 1 - cur

    @pl.when(step == 0)
    def _init():
        kv_scr[1] = jnp.zeros(kv_scr.shape[1:], f32)

    def deinterleave(r, carry):
        rows = pl.ds(pl.multiple_of(r * BLK, BLK), BLK)
        for p in range(N_PAIRS):
            q_scr[p, rows, :] = q_ref.at[p][pl.ds(r, BLK, stride=RES), :]
            kv_scr[cur, p, rows, :] = k_ref.at[p][pl.ds(r, BLK, stride=RES), :]
            kv_scr[cur, N_PAIRS + p, rows, :] = v_ref.at[p][pl.ds(r, BLK, stride=RES), :]
        return carry

    lax.fori_loop(0, RES, deinterleave, 0, unroll=4)

    lane = lax.broadcasted_iota(jnp.int32, (BLK, LANES), 1)
    low = lane < HEAD_DIM_A
    kcol = lax.broadcasted_iota(jnp.int32, (2 * BLK, 2 * BLK), 1)
    first_extra = jnp.where((kcol < BLK) & (n == 0), NEG, 0.0).astype(f32)

    def job(g, d, buf_a, off_a, off_b, res, extra):
        sub = RES // d
        run = BLK // sub

        def starts(off):
            return [pl.multiple_of((c * d + res) * BLK + off, SUBLANES) for c in range(sub)]

        def gather(ref, lead, off):
            return jnp.concatenate([ref[lead + (pl.ds(s, run), slice(None))] for s in starts(off)], axis=0)

        def scatter(ref, p, val):
            for c, s in enumerate(starts(off_b)):
                ref[p, pl.ds(s, run), :] = val[c * run:(c + 1) * run]

        for p in range(N_PAIRS):
            q2 = gather(q_scr, (p,), off_b) * (1.0 / math.sqrt(HEAD_DIM_A))
            qcat = jnp.concatenate([jnp.where(low, q2, 0.0), jnp.where(low, 0.0, q2)], axis=0)
            kk = jnp.concatenate([gather(kv_scr, (buf_a, p), off_a), gather(kv_scr, (cur, p), off_b)], axis=0)
            vv = jnp.concatenate([gather(kv_scr, (buf_a, N_PAIRS + p), off_a),
                                  gather(kv_scr, (cur, N_PAIRS + p), off_b)], axis=0)
            s = lax.dot_general(qcat.astype(bf16), kk.astype(bf16), (((1,), (1,)), ((), ())),
                                preferred_element_type=f32)
            s = s + bias_ref[g, p]
            if extra is not None:
                s = s + extra
            m = jnp.max(s, axis=-1, keepdims=True)
            e = jnp.exp(s - m)
            l = jnp.sum(e, axis=-1, keepdims=True)
            o = jnp.dot(e.astype(bf16), vv.astype(bf16), preferred_element_type=f32)
            o_t = jnp.where(low, o[:BLK], o[BLK:])
            m_t = jnp.where(low, m[:BLK], m[BLK:])
            l_t = jnp.where(low, l[:BLK], l[BLK:])
            if g == 0:
                scatter(acc_scr, p, o_t)
                scatter(m_scr, p, m_t)
                scatter(l_scr, p, l_t)
            else:
                m_o = gather(m_scr, (p,), off_b)
                m_n = jnp.maximum(m_o, m_t)
                al = jnp.exp(m_o - m_n)
                be = jnp.exp(m_t - m_n)
                scatter(acc_scr, p, al * gather(acc_scr, (p,), off_b) + be * o_t)
                scatter(l_scr, p, al * gather(l_scr, (p,), off_b) + be * l_t)
                scatter(m_scr, p, m_n)

    for g, (_, d) in enumerate(DILATED_BRANCHES):
        nu = Q_SUPER // (BLK * d)
        run = BLK * d // RES
        log_d = d.bit_length() - 1

        def first(res, carry, g=g, d=d, nu=nu, run=run):
            job(g, d, prv, (nu - 1) * run, 0, res, first_extra)
            return carry

        def rest(idx, carry, g=g, d=d, run=run, log_d=log_d):
            u = 1 + lax.shift_right_logical(idx, log_d)
            res = jnp.bitwise_and(idx, d - 1)
            job(g, d, cur, (u - 1) * run, u * run, res, None)
            return carry

        if d == 1:
            first(0, 0)
        else:
            lax.fori_loop(0, d, first, 0, unroll=4)
        if nu > 1:
            lax.fori_loop(0, (nu - 1) * d, rest, 0, unroll=5 if d == 1 else 4)

    def fin(r, carry):
        rows = pl.ds(pl.multiple_of(r * BLK, BLK), BLK)
        os_ = [acc_scr[p, rows, :] / l_scr[p, rows, :] for p in range(N_PAIRS)]
        ss = sum(jnp.sum(o * o, axis=-1, keepdims=True) for o in os_)
        sc = lax.rsqrt(ss * (1.0 / D_A) + EPS)
        for p in range(N_PAIRS):
            o_ref.at[p][pl.ds(r, BLK, stride=RES), :] = os_[p] * sc * g_ref[:, p * LANES:(p + 1) * LANES]
        return carry

    lax.fori_loop(0, RES, fin, 0, unroll=4)


def _attn_prompt(slabs, rel_bias, attn_out_g, batch, seq):
    assert seq % Q_SUPER == 0
    nsb = seq // Q_SUPER
    nbr = len(DILATED_BRANCHES)
    slab = lambda grp: pl.BlockSpec((N_PAIRS, Q_SUPER, LANES), lambda bi, ni: (grp, bi * nsb + ni, 0))
    return pl.pallas_call(
        _attn_prompt_kernel,
        out_shape=jax.ShapeDtypeStruct((N_PAIRS, batch * seq, LANES), f32),
        grid=(batch, nsb),
        in_specs=[
            pl.BlockSpec((nbr, N_PAIRS, 2 * BLK, 2 * BLK), lambda bi, ni: (0, 0, 0, 0),
                         pipeline_mode=pl.Buffered(1)),
            slab(0), slab(1), slab(2),
            pl.BlockSpec((1, D_A), lambda bi, ni: (0, 0)),
        ],
        out_specs=pl.BlockSpec((N_PAIRS, Q_SUPER, LANES), lambda bi, ni: (0, bi * nsb + ni, 0)),
        scratch_shapes=[
            pltpu.VMEM((N_PAIRS, Q_SUPER, LANES), f32),
            pltpu.VMEM((2, 2 * N_PAIRS, Q_SUPER, LANES), f32),
            pltpu.VMEM((N_PAIRS, Q_SUPER, LANES), f32),
            pltpu.VMEM((N_PAIRS, Q_SUPER, LANES), f32),
            pltpu.VMEM((N_PAIRS, Q_SUPER, LANES), f32),
        ],
        compiler_params=pltpu.CompilerParams(
            dimension_semantics=("arbitrary", "arbitrary"),
            vmem_limit_bytes=VMEM_LIMIT_BYTES),
        name="attn_prompt",
    )(_prompt_bias_table(rel_bias), slabs, slabs, slabs, attn_out_g.reshape(1, D_A))


def _log_sigmoid(x):
    return -(jnp.maximum(-x, 0.0) + jnp.log1p(jnp.exp(-jnp.abs(x))))


def _mlstm_prompt_kernel(xb_ref, ob_ref, gt_ref, gbias_ref, cw_ref, cb_ref, wq_ref, wk_ref, mg_ref,
                         skip_ref, out_ref, c_out_ref, n_out_ref, m_out_ref, conv_out_ref,
                         conv_scr, c_scr, n_scr, m_scr):
    c_idx = pl.program_id(1)
    L = MLSTM_CHUNK

    @pl.when(c_idx == 0)
    def _init():
        conv_scr[0:SUBLANES, :] = jnp.zeros((SUBLANES, D_B_PAD), f32)
        c_scr[...] = jnp.zeros_like(c_scr)
        n_scr[...] = jnp.zeros_like(n_scr)
        m_scr[...] = jnp.zeros_like(m_scr)

    x = xb_ref[...]
    conv_scr[SUBLANES:SUBLANES + L, :] = x
    c = cb_ref[...] + x * cw_ref[CONV_W - 1:CONV_W, :]
    for i in range(CONV_W - 1):
        sh = CONV_W - 1 - i
        c = c + conv_scr[SUBLANES - sh:SUBLANES - sh + L, :] * cw_ref[i:i + 1, :]
    conv_scr[0:SUBLANES, :] = x[L - SUBLANES:, :]
    c_act = c * jax.nn.sigmoid(c)

    gts = gt_ref[...] + gbias_ref[...]
    logf = _log_sigmoid(gts)
    row = lax.broadcasted_iota(jnp.int32, (L, L), 0)
    col = lax.broadcasted_iota(jnp.int32, (L, L), 1)
    causal = row >= col
    a_all = jnp.dot(causal.astype(f32), logf, precision=lax.Precision.HIGHEST,
                    preferred_element_type=f32)
    gts_t = gts.T
    a_t = a_all.T

    for h in range(N_HEADS_B):
        sl = slice(h * HEAD_PAD, (h + 1) * HEAD_PAD)
        ch = c_act[:, sl]
        ch_bf = ch.astype(bf16)
        q = jnp.dot(ch_bf, wq_ref[h], preferred_element_type=f32)
        k = jnp.dot(ch_bf, wk_ref[h], preferred_element_type=f32) * (1.0 / math.sqrt(HEAD_DIM_B))
        v = x[:, sl]
        q_bf, k_bf = q.astype(bf16), k.astype(bf16)
        a_col = a_all[:, N_HEADS_B + h:N_HEADS_B + h + 1]
        i_col = gts[:, h:h + 1]
        a_row = a_t[N_HEADS_B + h:N_HEADS_B + h + 1, :]
        i_row = gts_t[h:h + 1, :]
        m_prev = m_scr[h]
        dmat = jnp.where(causal, a_col - a_row + i_row, NEG)
        g = a_col + m_prev
        m_t = jnp.maximum(g, jnp.max(dmat, axis=-1, keepdims=True))
        w_state = jnp.exp(g - m_t)
        s = lax.dot_general(q_bf, k_bf, (((1,), (1,)), ((), ())), preferred_element_type=f32)
        amat = s * jnp.exp(dmat - m_t)
        c_state = c_scr[h]
        inter = lax.dot_general(q_bf, c_state.astype(bf16), (((1,), (1,)), ((), ())),
                                preferred_element_type=f32)
        num = w_state * inter + jnp.dot(amat.astype(bf16), v.astype(bf16),
                                        preferred_element_type=f32)
        n_row = n_scr[h]
        den = (w_state * jnp.sum(q * n_row, axis=-1, keepdims=True)
               + jnp.sum(amat, axis=-1, keepdims=True))
        hh = num / jnp.maximum(jnp.abs(den), jnp.exp(-m_t))
        hn = hh * lax.rsqrt(jnp.sum(hh * hh, axis=-1, keepdims=True) * (1.0 / HEAD_DIM_B) + EPS)
        hn = hn * mg_ref[:, sl]
        out_ref[:, sl] = jax.nn.sigmoid(ob_ref[:, sl]) * (hn + skip_ref[:, sl] * ch)

        b_tot = a_col[L - 1:L, :]
        wl = b_tot - a_col + i_col
        m_new = jnp.maximum(b_tot + m_prev, jnp.max(wl, axis=0, keepdims=True))
        wk = jnp.exp(wl - m_new)
        decay = jnp.exp(b_tot + m_prev - m_new)
        upd = lax.dot_general((wk * v).astype(bf16), k_bf, (((0,), (0,)), ((), ())),
                              preferred_element_type=f32)
        c_scr[h] = decay * c_state + upd
        n_scr[h] = decay * n_row + jnp.sum(wk * k, axis=0, keepdims=True)
        m_scr[h] = m_new

    @pl.when(c_idx == pl.num_programs(1) - 1)
    def _final():
        c_out_ref[0] = c_scr[...]
        n_out_ref[0] = n_scr[...]
        m_out_ref[0] = m_scr[...]
        conv_out_ref[0] = x[L - SUBLANES:, :]


def _mlstm_prompt(xb, ob, gates, gbias, cw, cb, wq, wk, mg, skip, batch, seq):
    L = MLSTM_CHUNK
    assert seq % L == 0
    nc = seq // L
    rows = lambda w: pl.BlockSpec((L, w), lambda bi, ci: (bi * nc + ci, 0))
    const2 = lambda shape: pl.BlockSpec(shape, lambda bi, ci: (0, 0))
    const3 = lambda shape: pl.BlockSpec(shape, lambda bi, ci: (0, 0, 0))
    state = lambda shape: pl.BlockSpec((1,) + shape, lambda bi, ci: (bi,) + (0,) * len(shape))
    return pl.pallas_call(
        _mlstm_prompt_kernel,
        out_shape=(
            jax.ShapeDtypeStruct((batch * seq, D_B_PAD), f32),
            jax.ShapeDtypeStruct((batch, N_HEADS_B, HEAD_PAD, HEAD_PAD), f32),
            jax.ShapeDtypeStruct((batch, N_HEADS_B, 1, HEAD_PAD), f32),
            jax.ShapeDtypeStruct((batch, N_HEADS_B, 1, 1), f32),
            jax.ShapeDtypeStruct((batch, SUBLANES, D_B_PAD), f32),
        ),
        grid=(batch, nc),
        in_specs=[rows(D_B_PAD), rows(D_B_PAD), rows(LANES), const2((1, LANES)),
                  const2((CONV_W, D_B_PAD)), const2((1, D_B_PAD)),
                  const3((N_HEADS_B, HEAD_PAD, HEAD_PAD)), const3((N_HEADS_B, HEAD_PAD, HEAD_PAD)),
                  const2((1, D_B_PAD)), const2((1, D_B_PAD))],
        out_specs=(rows(D_B_PAD), state((N_HEADS_B, HEAD_PAD, HEAD_PAD)),
                   state((N_HEADS_B, 1, HEAD_PAD)), state((N_HEADS_B, 1, 1)),
                   state((SUBLANES, D_B_PAD))),
        scratch_shapes=[
            pltpu.VMEM((SUBLANES + L, D_B_PAD), f32),
            pltpu.VMEM((N_HEADS_B, HEAD_PAD, HEAD_PAD), f32),
            pltpu.VMEM((N_HEADS_B, 1, HEAD_PAD), f32),
            pltpu.VMEM((N_HEADS_B, 1, 1), f32),
        ],
        compiler_params=pltpu.CompilerParams(
            dimension_semantics=("arbitrary", "arbitrary"),
            vmem_limit_bytes=VMEM_LIMIT_BYTES),
        name="mlstm_prompt",
    )(xb, ob, gates, gbias, cw, cb, wq, wk, mg, skip)


def _step_bias_rows(rel_bias, p):
    dist = p - jnp.arange(p)
    full = rel_bias[_t5_bucket(dist)].astype(f32).T
    rows = []
    for (w, d) in DILATED_BRANCHES:
        in_branch = (dist % d == 0) & (dist <= w)
        rows.append(jnp.where(in_branch[None, :], full, NEG))
    zero = rel_bias[_t5_bucket(jnp.zeros((1,), jnp.int32))].astype(f32).T
    nbr = len(DILATED_BRANCHES)
    return jnp.stack(rows)[:, :, None, :], jnp.broadcast_to(zero[None, :, None, :], (nbr, N_HEADS_A, 1, 1))


def _attn_step_kernel(bias_ref, bias0_ref, zt_ref, g_ref, kt_ref, vt_ref, ot_ref, nk_ref, nv_ref):
    b = pl.program_id(0)
    p = kt_ref.shape[-1]
    nbr = len(DILATED_BRANCHES)
    scale = 1.0 / math.sqrt(HEAD_DIM_A)

    @pl.when(b == 0)
    def _init():
        ot_ref[...] = jnp.zeros_like(ot_ref)

    zt = zt_ref[...]
    is_b = lax.broadcasted_iota(jnp.int32, zt.shape, 1) == b
    col = jnp.sum(jnp.where(is_b, zt, 0.0), axis=1, keepdims=True)
    is_last = lax.broadcasted_iota(jnp.int32, (HEAD_DIM_A, p), 1) == p - 1
    o_cols = []
    for h in range(N_HEADS_A):
        q_col = col[h * HEAD_DIM_A:(h + 1) * HEAD_DIM_A] * scale
        kn_col = col[D_A + h * HEAD_DIM_A:D_A + (h + 1) * HEAD_DIM_A]
        vn_col = col[2 * D_A + h * HEAD_DIM_A:2 * D_A + (h + 1) * HEAD_DIM_A]
        kt = kt_ref[0, h]
        vt = vt_ref[0, h]
        nk_ref[0, h] = jnp.where(is_last, kn_col, pltpu.roll(kt, p - 1, axis=1))
        nv_ref[0, h] = jnp.where(is_last, vn_col, pltpu.roll(vt, p - 1, axis=1))
        s = jnp.sum(kt * q_col, axis=0, keepdims=True)
        s_new = jnp.sum(kn_col * q_col, axis=0, keepdims=True)
        sg = [s + bias_ref[g, h] for g in range(nbr)]
        s0 = [s_new + bias0_ref[g, h] for g in range(nbr)]
        m = functools.reduce(jnp.maximum, [jnp.max(x, axis=1, keepdims=True) for x in sg] + s0)
        pw = sum(jnp.exp(x - m) for x in sg)
        e0 = sum(jnp.exp(x - m) for x in s0)
        l = jnp.sum(pw, axis=1, keepdims=True) + e0
        o_cols.append((jnp.sum(vt * pw, axis=1, keepdims=True) + e0 * vn_col) / l)
    o = jnp.concatenate(o_cols, axis=0)
    ssq = jnp.sum(o * o, axis=0, keepdims=True)
    o = o * lax.rsqrt(ssq * (1.0 / D_A) + EPS) * g_ref[...]
    sel = lax.broadcasted_iota(jnp.int32, ot_ref.shape, 1) == b
    ot_ref[...] = jnp.where(sel, o, ot_ref[...])


def _attn_step(zt, cache_kt, cache_vt, rel_bias, attn_out_g):
    nb, p = cache_kt.shape[0], cache_kt.shape[-1]
    for (w, d) in DILATED_BRANCHES:
        assert w <= p and w // d == BLK
    nbr = len(DILATED_BRANCHES)
    bias, bias0 = _step_bias_rows(rel_bias, p)
    blk = pl.BlockSpec((1, N_HEADS_A, HEAD_DIM_A, p), lambda i: (i, 0, 0, 0))
    buf = jax.ShapeDtypeStruct(cache_kt.shape, f32)
    return pl.pallas_call(
        _attn_step_kernel,
        out_shape=(jax.ShapeDtypeStruct((D_A, nb), f32), buf, buf),
        grid=(nb,),
        in_specs=[
            pl.BlockSpec((nbr, N_HEADS_A, 1, p), lambda i: (0, 0, 0, 0)),
            pl.BlockSpec((nbr, N_HEADS_A, 1, 1), lambda i: (0, 0, 0, 0)),
            pl.BlockSpec((3 * D_A, nb), lambda i: (0, 0)),
            pl.BlockSpec((D_A, 1), lambda i: (0, 0)),
            blk, blk,
        ],
        out_specs=(pl.BlockSpec((D_A, nb), lambda i: (0, 0)), blk, blk),
        compiler_params=pltpu.CompilerParams(
            dimension_semantics=("arbitrary",),
            vmem_limit_bytes=VMEM_LIMIT_BYTES),
        name="attn_step",
    )(bias, bias0, zt, attn_out_g.reshape(D_A, 1), cache_kt, cache_vt)


N_ROWS = 3
ROW_W, ROW_A, ROW_R = range(N_ROWS)


def _mlstm_step_pre_kernel(xb_ref, gt_ref, gbias_ref, sc_ref, cw_ref, cb_ref, wq_ref, wk_ref, nt_ref, mt_ref,
                           cact_out, qt_out, kt_out, vt_out, vst_out, rows_out, nt_out, mt_out):
    E = HEAD_DIM_B
    x = xb_ref[...]
    c = cb_ref[...] + x * cw_ref[CONV_W - 1:CONV_W, :]
    for t in range(CONV_W - 1):
        c = c + sc_ref[t] * cw_ref[t:t + 1, :]
    c_act = c * jax.nn.sigmoid(c)
    cact_out[...] = c_act
    gts_t = (gt_ref[...] + gbias_ref[...]).T
    logf_t = _log_sigmoid(gts_t)
    for h in range(N_HEADS_B):
        sl = slice(h * HEAD_PAD, (h + 1) * HEAD_PAD)
        ch_bf = c_act[:, sl].astype(bf16)
        q = jnp.dot(ch_bf, wq_ref[h], preferred_element_type=f32)
        k = jnp.dot(ch_bf, wk_ref[h], preferred_element_type=f32) * (1.0 / math.sqrt(HEAD_DIM_B))
        qt = q.T[:E]
        kt = k.T[:E]
        vt = x[:, sl].T[:E]
        i_pre = gts_t[h:h + 1]
        a = logf_t[N_HEADS_B + h:N_HEADS_B + h + 1]
        m_old = mt_ref[h:h + 1]
        m_t = jnp.maximum(a + m_old, i_pre)
        w_state = jnp.exp(a + m_old - m_t)
        w_in = jnp.exp(i_pre - m_t)
        amat = jnp.sum(qt * kt, axis=0, keepdims=True) * w_in
        n_old = nt_ref[h]
        den = w_state * jnp.sum(n_old * qt, axis=0, keepdims=True) + amat
        nt_out[h] = w_state * n_old + w_in * kt
        mt_out[h:h + 1] = m_t
        qt_out[h] = qt
        kt_out[h] = kt
        vt_out[h] = vt
        vst_out[h] = w_in * vt
        rows_out[h, ROW_W:ROW_W + 1] = w_state
        rows_out[h, ROW_A:ROW_A + 1] = amat
        rows_out[h, ROW_R:ROW_R + 1] = 1.0 / jnp.maximum(jnp.abs(den), jnp.exp(-m_t))


def _mlstm_step_pre(xb, gts, gbias, sc_t, cw, cb, wq, wk, nt, mt):
    nb = xb.shape[0]
    hd = jax.ShapeDtypeStruct((N_HEADS_B, HEAD_DIM_B, nb), f32)
    return pl.pallas_call(
        _mlstm_step_pre_kernel,
        out_shape=(jax.ShapeDtypeStruct((nb, D_B_PAD), f32), hd, hd, hd, hd,
                   jax.ShapeDtypeStruct((N_HEADS_B, N_ROWS, nb), f32), hd,
                   jax.ShapeDtypeStruct((N_HEADS_B, nb), f32)),
        compiler_params=pltpu.CompilerParams(vmem_limit_bytes=VMEM_LIMIT_BYTES),
        name="mlstm_step_pre",
    )(xb, gts, gbias, sc_t, cw, cb, wq, wk, nt, mt)


ST_VT = 40


def _mlstm_step_state_kernel(c_ref, qt_ref, kt_ref, vt_ref, vst_ref, rows_ref, ob_ref, cact_ref, mg_ref,
                             skip_ref, c_out, out_ref, cq_scr):
    vb = pl.program_id(1)
    E = HEAD_DIM_B
    qt = qt_ref[0]
    kt = kt_ref[0]
    w_state = rows_ref[0, ROW_W:ROW_W + 1]

    def per_row(v, carry):
        c_old = c_ref[0, v]
        vs = vst_ref[0, pl.ds(vb * ST_VT + v, 1), :]
        c_out[0, v] = w_state * c_old + vs * kt
        cq_scr[pl.ds(vb * ST_VT + v, 1), :] = jnp.sum(c_old * qt, axis=0, keepdims=True)
        return carry

    lax.fori_loop(0, ST_VT, per_row, 0)

    @pl.when(vb == pl.num_programs(1) - 1)
    def _finish_head():
        ht = (w_state * cq_scr[...] + rows_ref[0, ROW_A:ROW_A + 1] * vt_ref[0]) * rows_ref[0, ROW_R:ROW_R + 1]
        hh = jnp.concatenate([ht, jnp.zeros((HEAD_PAD - E, ht.shape[1]), f32)], axis=0).T
        hn = hh * lax.rsqrt(jnp.sum(hh * hh, axis=-1, keepdims=True) * (1.0 / E) + EPS) * mg_ref[0]
        out_ref[0] = jax.nn.sigmoid(ob_ref[0]) * (hn + skip_ref[0] * cact_ref[0])


def _mlstm_step_state(ct, qt, kt, vt, vst, rows, ob3, cact3, mg3, skip3):
    nh, e, _, nb = ct.shape
    assert e % ST_VT == 0
    head = lambda shape: pl.BlockSpec((1,) + shape, lambda h, v: (h,) + (0,) * len(shape))
    cblk = pl.BlockSpec((1, ST_VT, e, nb), lambda h, v: (h, v, 0, 0))
    return pl.pallas_call(
        _mlstm_step_state_kernel,
        out_shape=(jax.ShapeDtypeStruct(ct.shape, f32), jax.ShapeDtypeStruct((nh, nb, HEAD_PAD), f32)),
        grid=(nh, e // ST_VT),
        in_specs=[cblk, head((e, nb)), head((e, nb)), head((e, nb)), head((e, nb)), head((N_ROWS, nb)),
                  head((nb, HEAD_PAD)), head((nb, HEAD_PAD)), head((1, HEAD_PAD)), head((1, HEAD_PAD))],
        out_specs=(cblk, head((nb, HEAD_PAD))),
        scratch_shapes=[pltpu.VMEM((e, nb), f32)],
        compiler_params=pltpu.CompilerParams(
            dimension_semantics=("arbitrary", "arbitrary"),
            vmem_limit_bytes=VMEM_LIMIT_BYTES),
        name="mlstm_step_state",
    )(ct, qt, kt, vt, vst, rows, ob3, cact3, mg3, skip3)


def kernel(x_prompt, x_sample, cache_win_k, cache_win_v, state_conv, state_C, state_n, state_m,
           rel_bias, norm1_g, w_in, gate_bias, conv_w, conv_b, wq_head, wk_head, attn_out_g,
           mh_norm_g, skip, w_out, norm2_g, w_ff1, w_ff2, final_g):
    Bp, Sp, _ = x_prompt.shape
    Bs, Ss, _ = x_sample.shape
    assert Ss == 1
    g1 = norm1_g[0].reshape(1, D_MODEL)
    g2 = norm2_g[0].reshape(1, D_MODEL)
    gf = final_g.reshape(1, D_MODEL)
    w_in_pad = _pad_w_in(w_in[0]).astype(bf16)
    wo = jnp.concatenate([w_out[0][:D_A], jnp.zeros((LANES, D_MODEL), f32),
                          _pad_heads(w_out[0][D_A:].T).T], axis=0).astype(bf16)
    w1 = w_ff1[0].astype(bf16)
    w2 = w_ff2[0].astype(bf16)
    hpad = HEAD_PAD - HEAD_DIM_B
    wq_p = jnp.pad(wq_head[0], ((0, 0), (0, hpad), (0, hpad))).astype(bf16)
    wk_p = jnp.pad(wk_head[0], ((0, 0), (0, hpad), (0, hpad))).astype(bf16)
    gbias = jnp.pad(gate_bias[0], (0, LANES - N_GATES)).reshape(1, LANES)
    cw_p = _pad_heads(conv_w[0])
    cb_p = _pad_heads(conv_b[0]).reshape(1, D_B_PAD)
    mg_p = _pad_heads(mh_norm_g[0]).reshape(1, D_B_PAD)
    skip_p = _pad_heads(skip[0]).reshape(1, D_B_PAD)

    xp2 = x_prompt.reshape(Bp * Sp, D_MODEL)
    P = min(WINDOW_MAX, Sp)
    slabs, kv_win, xb, ob, gts = _norm_inproj(xp2, g1, w_in_pad, tm=512, seq=Sp, win=P)
    out_a = _attn_prompt(slabs, rel_bias, attn_out_g[0], Bp, Sp)
    out_b, c_p, n_p, m_p, tail_p = _mlstm_prompt(xb, ob, gts, gbias, cw_p, cb_p, wq_p, wk_p, mg_p,
                                                 skip_p, Bp, Sp)
    y_p = _out_ffn(xp2, out_a, out_b, wo, g2, w1, w2, gf, tm=512).reshape(Bp, Sp, D_MODEL)
    kv = kv_win.reshape(2, N_PAIRS, Bp, P, LANES)
    kv = kv.transpose(0, 2, 3, 1, 4).reshape(2, Bp, P, N_HEADS_A, HEAD_DIM_A)
    st_p = (kv[0][None], kv[1][None], _unpad_heads(tail_p[:, SUBLANES - (CONV_W - 1):])[None],
            c_p[:, :, :HEAD_DIM_B, :HEAD_DIM_B][None], n_p[:, :, 0, :HEAD_DIM_B][None],
            m_p[:, :, 0, 0][None])

    xs2 = x_sample.reshape(Bs, D_MODEL)
    zt_s, xb_s, ob_s, gts_s = _norm_inproj_step(xs2, g1, w_in[0][:, :3 * D_A].T.astype(bf16), w_in_pad)
    oat_s, nkt, nvt = _attn_step(zt_s, cache_win_k[0].transpose(0, 2, 3, 1),
                                 cache_win_v[0].transpose(0, 2, 3, 1), rel_bias, attn_out_g[0])
    oa_s = oat_s.reshape(N_PAIRS, LANES, Bs).transpose(0, 2, 1)
    new_k = nkt.transpose(0, 3, 1, 2)[None]
    new_v = nvt.transpose(0, 3, 1, 2)[None]
    sc_t = _pad_heads(state_conv[0].transpose(1, 0, 2))
    cact_s, qt_s, kt_s, vt_s, vst_s, rows_s, nt_s, mt_s = _mlstm_step_pre(
        xb_s, gts_s, gbias, sc_t, cw_p, cb_p, wq_p, wk_p,
        state_n[0].transpose(1, 2, 0), state_m[0].T)
    heads3 = lambda t: t.reshape(t.shape[0], N_HEADS_B, HEAD_PAD).transpose(1, 0, 2)
    ct_s, outb3 = _mlstm_step_state(state_C[0].transpose(1, 2, 3, 0), qt_s, kt_s, vt_s, vst_s, rows_s,
                                    heads3(ob_s), heads3(cact_s), heads3(mg_p), heads3(skip_p))
    outb_s = outb3.transpose(1, 0, 2).reshape(Bs, D_B_PAD)
    y_s = _out_ffn(xs2, oa_s, outb_s, wo, g2, w1, w2, gf, tm=Bs).reshape(Bs, Ss, D_MODEL)
    new_conv = jnp.concatenate([state_conv[0][:, 1:], _unpad_heads(xb_s)[:, None]], axis=1)
    st_s = (new_k, new_v, new_conv[None], ct_s.transpose(3, 0, 1, 2)[None],
            nt_s.transpose(2, 0, 1)[None], mt_s.T[None])
    return (y_p, y_s) + st_p + st_s
```

```python
import functools
import math

import jax
import jax.numpy as jnp
from jax import lax
from jax.experimental import pallas as pl
from jax.experimental.pallas import tpu as pltpu

D_MODEL = 1024
HEAD_DIM_A = 64
N_HEADS_A = 6
D_A = N_HEADS_A * HEAD_DIM_A
D_B = D_MODEL - D_A
N_HEADS_B = 4
HEAD_DIM_B = D_B // N_HEADS_B
DILATED_BRANCHES = ((128, 1), (512, 4), (2048, 16))
WINDOW_MAX = 2048
N_BUCKETS = 32
REL_MAX_DIST = 2048
CONV_W = 4
MLSTM_CHUNK = 128
D_FF = 4 * D_MODEL
N_GATES = 2 * N_HEADS_B
D_IN = 3 * D_A + 2 * D_B + N_GATES
SPLITS = [D_A, 2 * D_A, 3 * D_A, 3 * D_A + D_B, 3 * D_A + 2 * D_B]
EPS = 1e-6
NEG = -1e30

LANES = 128
SUBLANES = 8
VMEM_LIMIT_BYTES = 56 * 1024 * 1024

N_PAIRS = D_A // LANES
N_SLABS = 3 * N_PAIRS
HEAD_PAD = 2 * LANES
D_B_PAD = N_HEADS_B * HEAD_PAD
D_IN_PAD = 3 * D_A + 2 * D_B_PAD + LANES
D_O_PAD = D_A + LANES + D_B_PAD
BLK = 128
RES = DILATED_BRANCHES[-1][1]
Q_SUPER = BLK * RES

f32 = jnp.float32
bf16 = jnp.bfloat16


def _round_up(n, m):
    return -(-n // m) * m


def _rms(xf, g):
    return xf * lax.rsqrt(jnp.mean(xf * xf, axis=-1, keepdims=True) + EPS) * g


def _pad_heads(t):
    t = t.reshape(t.shape[:-1] + (N_HEADS_B, HEAD_DIM_B))
    t = jnp.pad(t, [(0, 0)] * (t.ndim - 1) + [(0, HEAD_PAD - HEAD_DIM_B)])
    return t.reshape(t.shape[:-2] + (D_B_PAD,))


def _unpad_heads(t):
    t = t.reshape(t.shape[:-1] + (N_HEADS_B, HEAD_PAD))[..., :HEAD_DIM_B]
    return t.reshape(t.shape[:-2] + (D_B,))


def _norm_inproj_kernel(x_ref, g_ref, w_ref, slab_ref, kwin_ref, vwin_ref, xb_ref, ob_ref, gt_ref):
    h = _rms(x_ref[...], g_ref[...]).astype(bf16)
    qkv = jnp.dot(h, w_ref[:, 0:3 * D_A], preferred_element_type=f32)
    for j in range(N_SLABS):
        slab_ref[j] = qkv[:, j * LANES:(j + 1) * LANES]
    for p in range(N_PAIRS):
        kwin_ref[p] = qkv[:, (N_PAIRS + p) * LANES:(N_PAIRS + p + 1) * LANES]
        vwin_ref[p] = qkv[:, (2 * N_PAIRS + p) * LANES:(2 * N_PAIRS + p + 1) * LANES]
    o = 3 * D_A
    xb_ref[...] = jnp.dot(h, w_ref[:, o:o + D_B_PAD], preferred_element_type=f32)
    o += D_B_PAD
    ob_ref[...] = jnp.dot(h, w_ref[:, o:o + D_B_PAD], preferred_element_type=f32)
    o += D_B_PAD
    gt_ref[...] = jnp.dot(h, w_ref[:, o:o + LANES], preferred_element_type=f32)


def _norm_inproj(x, g, w_pad_bf16, tm, seq, win):
    m, d = x.shape
    assert m % seq == 0 and seq % tm == 0 and win % tm == 0
    tiles_seq, tiles_win = seq // tm, win // tm
    row = lambda w: pl.BlockSpec((tm, w), lambda i: (i, 0))

    def win_block(i):
        return (0, (i // tiles_seq) * tiles_win + jnp.maximum(i % tiles_seq - (tiles_seq - tiles_win), 0), 0)

    return pl.pallas_call(
        _norm_inproj_kernel,
        out_shape=(
            jax.ShapeDtypeStruct((N_SLABS, m, LANES), f32),
            jax.ShapeDtypeStruct((N_PAIRS, (m // seq) * win, LANES), f32),
            jax.ShapeDtypeStruct((N_PAIRS, (m // seq) * win, LANES), f32),
            jax.ShapeDtypeStruct((m, D_B_PAD), f32),
            jax.ShapeDtypeStruct((m, D_B_PAD), f32),
            jax.ShapeDtypeStruct((m, LANES), f32),
        ),
        grid=(m // tm,),
        in_specs=[
            row(d),
            pl.BlockSpec((1, d), lambda i: (0, 0)),
            pl.BlockSpec((d, D_IN_PAD), lambda i: (0, 0), pipeline_mode=pl.Buffered(1)),
        ],
        out_specs=(
            pl.BlockSpec((N_SLABS, tm, LANES), lambda i: (0, i, 0)),
            pl.BlockSpec((N_PAIRS, tm, LANES), win_block),
            pl.BlockSpec((N_PAIRS, tm, LANES), win_block),
            row(D_B_PAD), row(D_B_PAD), row(LANES),
        ),
        compiler_params=pltpu.CompilerParams(
            dimension_semantics=("arbitrary",),
            vmem_limit_bytes=VMEM_LIMIT_BYTES),
        name="norm_inproj",
    )(x, g, w_pad_bf16)


def _norm_inproj_step_kernel(x_ref, g_ref, wt_ref, w_ref, zt_ref, xb_ref, ob_ref, gt_ref):
    h = _rms(x_ref[...], g_ref[...]).astype(bf16)
    zt_ref[...] = lax.dot_general(wt_ref[...], h, (((1,), (1,)), ((), ())), preferred_element_type=f32)
    o = 3 * D_A
    xb_ref[...] = jnp.dot(h, w_ref[:, o:o + D_B_PAD], preferred_element_type=f32)
    o += D_B_PAD
    ob_ref[...] = jnp.dot(h, w_ref[:, o:o + D_B_PAD], preferred_element_type=f32)
    o += D_B_PAD
    gt_ref[...] = jnp.dot(h, w_ref[:, o:o + LANES], preferred_element_type=f32)


def _norm_inproj_step(x, g, wqkv_t_bf16, w_pad_bf16):
    nb = x.shape[0]
    wide = jax.ShapeDtypeStruct((nb, D_B_PAD), f32)
    return pl.pallas_call(
        _norm_inproj_step_kernel,
        out_shape=(jax.ShapeDtypeStruct((3 * D_A, nb), f32), wide, wide,
                   jax.ShapeDtypeStruct((nb, LANES), f32)),
        compiler_params=pltpu.CompilerParams(vmem_limit_bytes=VMEM_LIMIT_BYTES),
        name="norm_inproj_step",
    )(x, g, wqkv_t_bf16, w_pad_bf16)


def _pad_w_in(w_in):
    qkv = w_in[:, :3 * D_A]
    xb = _pad_heads(w_in[:, SPLITS[2]:SPLITS[3]])
    ob = _pad_heads(w_in[:, SPLITS[3]:SPLITS[4]])
    gt = jnp.pad(w_in[:, SPLITS[4]:], ((0, 0), (0, LANES - N_GATES)))
    return jnp.concatenate([qkv, xb, ob, gt], axis=1)


def _out_ffn_kernel(x_ref, oa_ref, ob_ref, wo_ref, g2_ref, w1_ref, w2_ref, gf_ref, y_ref, *, ff_chunk):
    o = jnp.concatenate([oa_ref[p].astype(bf16) for p in range(N_PAIRS)]
                        + [jnp.zeros((ob_ref.shape[0], LANES), bf16), ob_ref[...].astype(bf16)], axis=1)
    x1 = x_ref[...] + jnp.dot(o, wo_ref[...], preferred_element_type=f32)
    h2 = _rms(x1, g2_ref[...]).astype(bf16)
    acc = x1
    for c in range(D_FF // ff_chunk):
        u = jnp.dot(h2, w1_ref[:, c * ff_chunk:(c + 1) * ff_chunk], preferred_element_type=f32)
        u = jnp.square(jnp.maximum(u, 0.0)).astype(bf16)
        acc = acc + jnp.dot(u, w2_ref[c * ff_chunk:(c + 1) * ff_chunk, :],
                            preferred_element_type=f32)
    y_ref[...] = _rms(acc, gf_ref[...])


def _out_ffn(x, oa, ob, wo, g2, w1, w2, gf, tm, ff_chunk=1024):
    m, d = x.shape
    assert wo.shape[0] == D_O_PAD
    const = lambda i: (0, 0)
    single = dict(pipeline_mode=pl.Buffered(1))
    row = lambda w: pl.BlockSpec((tm, w), lambda i: (i, 0))
    return pl.pallas_call(
        functools.partial(_out_ffn_kernel, ff_chunk=ff_chunk),
        out_shape=jax.ShapeDtypeStruct((m, d), f32),
        grid=(m // tm,),
        in_specs=[
            row(d), pl.BlockSpec((N_PAIRS, tm, LANES), lambda i: (0, i, 0)), row(D_B_PAD),
            pl.BlockSpec((D_O_PAD, d), const, **single),
            pl.BlockSpec((1, d), const),
            pl.BlockSpec((d, D_FF), const, **single),
            pl.BlockSpec((D_FF, d), const, **single),
            pl.BlockSpec((1, d), const),
        ],
        out_specs=row(d),
        compiler_params=pltpu.CompilerParams(
            dimension_semantics=("arbitrary",),
            vmem_limit_bytes=VMEM_LIMIT_BYTES),
        name="outproj_ffn",
    )(x, oa, ob, wo, g2, w1, w2, gf)


def _t5_bucket(dist):
    max_exact = N_BUCKETS // 2
    df = jnp.maximum(dist, 1).astype(jnp.float32)
    large = max_exact + (jnp.log(df / max_exact) / math.log(REL_MAX_DIST / max_exact)
                         * (N_BUCKETS - max_exact)).astype(jnp.int32)
    large = jnp.minimum(large, N_BUCKETS - 1)
    return jnp.where(dist < max_exact, dist, large)


def _prompt_bias_table(rel_bias):
    period = 3 * BLK + 1
    r = jnp.arange(period)
    t = jnp.where(r < 2 * BLK, r, r - period)
    j = BLK - t
    band = (j >= 0) & (j <= BLK)
    tabs = []
    for (_, d) in DILATED_BRANCHES:
        vals = rel_bias[_t5_bucket(jnp.clip(j, 0, None) * d)].astype(f32)
        w = jnp.where(band[:, None], vals, NEG).T
        flat = jnp.tile(w, (1, BLK))[:, :BLK * (period - 1)]
        toe = flat.reshape(N_HEADS_A, BLK, period - 1)[:, :, :2 * BLK]
        sub = RES // d
        if sub > 1:
            pos = jnp.arange(BLK)
            nat = (pos % (BLK // sub)) * sub + pos // (BLK // sub)
            pq = jax.nn.one_hot(nat, BLK, dtype=f32)
            pk = jnp.kron(jnp.eye(2, dtype=f32), pq)
            toe = jnp.einsum('pq,hqk->hpk', pq, toe, precision=lax.Precision.HIGHEST)
            toe = jnp.einsum('hpk,jk->hpj', toe, pk, precision=lax.Precision.HIGHEST)
        tabs.append(toe.reshape(N_PAIRS, 2 * BLK, 2 * BLK))
    return jnp.stack(tabs)


def _attn_prompt_kernel(bias_ref, q_ref, k_ref, v_ref, g_ref, o_ref, q_scr, kv_scr, m_scr, l_scr, acc_scr):
    n = pl.program_id(1)
    step = pl.program_id(0) * pl.num_programs(1) + n
    cur = lax.rem(step, 2)
    prv = 1 - cur

    @pl.when(step == 0)
    def _init():
        kv_scr[1] = jnp.zeros(kv_scr.shape[1:], f32)

    def deinterleave(r, carry):
        rows = pl.ds(pl.multiple_of(r * BLK, BLK), BLK)
        for p in range(N_PAIRS):
            q_scr[p, rows, :] = q_ref.at[p][pl.ds(r, BLK, stride=RES), :]
            kv_scr[cur, p, rows, :] = k_ref.at[p][pl.ds(r, BLK, stride=RES), :]
            kv_scr[cur, N_PAIRS + p, rows, :] = v_ref.at[p][pl.ds(r, BLK, stride=RES), :]
        return carry

    lax.fori_loop(0, RES, deinterleave, 0, unroll=4)

    lane = lax.broadcasted_iota(jnp.int32, (BLK, LANES), 1)
    low = lane < HEAD_DIM_A
    kcol = lax.broadcasted_iota(jnp.int32, (2 * BLK, 2 * BLK), 1)
    first_extra = jnp.where((kcol < BLK) & (n == 0), NEG, 0.0).astype(f32)

    def job(g, d, buf_a, off_a, off_b, res, extra):
        sub = RES // d
        run = BLK // sub

        def starts(off):
            return [pl.multiple_of((c * d + res) * BLK + off, SUBLANES) for c in range(sub)]

        def gather(ref, lead, off):
            return jnp.concatenate([ref[lead + (pl.ds(s, run), slice(None))] for s in starts(off)], axis=0)

        def scatter(ref, p, val):
            for c, s in enumerate(starts(off_b)):
                ref[p, pl.ds(s, run), :] = val[c * run:(c + 1) * run]

        for p in range(N_PAIRS):
            q2 = gather(q_scr, (p,), off_b) * (1.0 / math.sqrt(HEAD_DIM_A))
            qcat = jnp.concatenate([jnp.where(low, q2, 0.0), jnp.where(low, 0.0, q2)], axis=0)
            kk = jnp.concatenate([gather(kv_scr, (buf_a, p), off_a), gather(kv_scr, (cur, p), off_b)], axis=0)
            vv = jnp.concatenate([gather(kv_scr, (buf_a, N_PAIRS + p), off_a),
                                  gather(kv_scr, (cur, N_PAIRS + p), off_b)], axis=0)
            s = lax.dot_general(qcat.astype(bf16), kk.astype(bf16), (((1,), (1,)), ((), ())),
                                preferred_element_type=f32)
            s = s + bias_ref[g, p]
            if extra is not None:
                s = s + extra
            m = jnp.max(s, axis=-1, keepdims=True)
            e = jnp.exp(s - m)
            l = jnp.sum(e, axis=-1, keepdims=True)
            o = jnp.dot(e.astype(bf16), vv.astype(bf16), preferred_element_type=f32)
            o_t = jnp.where(low, o[:BLK], o[BLK:])
            m_t = jnp.where(low, m[:BLK], m[BLK:])
            l_t = jnp.where(low, l[:BLK], l[BLK:])
            if g == 0:
                scatter(acc_scr, p, o_t)
                scatter(m_scr, p, m_t)
                scatter(l_scr, p, l_t)
            else:
                m_o = gather(m_scr, (p,), off_b)
                m_n = jnp.maximum(m_o, m_t)
                al = jnp.exp(m_o - m_n)
                be = jnp.exp(m_t - m_n)
                scatter(acc_scr, p, al * gather(acc_scr, (p,), off_b) + be * o_t)
                scatter(l_scr, p, al * gather(l_scr, (p,), off_b) + be * l_t)
                scatter(m_scr, p, m_n)

    for g, (_, d) in enumerate(DILATED_BRANCHES):
        nu = Q_SUPER // (BLK * d)
        run = BLK * d // RES
        log_d = d.bit_length() - 1

        def first(res, carry, g=g, d=d, nu=nu, run=run):
            job(g, d, prv, (nu - 1) * run, 0, res, first_extra)
            return carry

        def rest(idx, carry, g=g, d=d, run=run, log_d=log_d):
            u = 1 + lax.shift_right_logical(idx, log_d)
            res = jnp.bitwise_and(idx, d - 1)
            job(g, d, cur, (u - 1) * run, u * run, res, None)
            return carry

        if d == 1:
            first(0, 0)
        else:
            lax.fori_loop(0, d, first, 0, unroll=4)
        if nu > 1:
            lax.fori_loop(0, (nu - 1) * d, rest, 0, unroll=5 if d == 1 else 4)

    def fin(r, carry):
        rows = pl.ds(pl.multiple_of(r * BLK, BLK), BLK)
        os_ = [acc_scr[p, rows, :] / l_scr[p, rows, :] for p in range(N_PAIRS)]
        ss = sum(jnp.sum(o * o, axis=-1, keepdims=True) for o in os_)
        sc = lax.rsqrt(ss * (1.0 / D_A) + EPS)
        for p in range(N_PAIRS):
            o_ref.at[p][pl.ds(r, BLK, stride=RES), :] = os_[p] * sc * g_ref[:, p * LANES:(p + 1) * LANES]
        return carry

    lax.fori_loop(0, RES, fin, 0, unroll=4)


def _attn_prompt(slabs, rel_bias, attn_out_g, batch, seq):
    assert seq % Q_SUPER == 0
    nsb = seq // Q_SUPER
    nbr = len(DILATED_BRANCHES)
    slab = lambda grp: pl.BlockSpec((N_PAIRS, Q_SUPER, LANES), lambda bi, ni: (grp, bi * nsb + ni, 0))
    return pl.pallas_call(
        _attn_prompt_kernel,
        out_shape=jax.ShapeDtypeStruct((N_PAIRS, batch * seq, LANES), f32),
        grid=(batch, nsb),
        in_specs=[
            pl.BlockSpec((nbr, N_PAIRS, 2 * BLK, 2 * BLK), lambda bi, ni: (0, 0, 0, 0),
                         pipeline_mode=pl.Buffered(1)),
            slab(0), slab(1), slab(2),
            pl.BlockSpec((1, D_A), lambda bi, ni: (0, 0)),
        ],
        out_specs=pl.BlockSpec((N_PAIRS, Q_SUPER, LANES), lambda bi, ni: (0, bi * nsb + ni, 0)),
        scratch_shapes=[
            pltpu.VMEM((N_PAIRS, Q_SUPER, LANES), f32),
            pltpu.VMEM((2, 2 * N_PAIRS, Q_SUPER, LANES), f32),
            pltpu.VMEM((N_PAIRS, Q_SUPER, LANES), f32),
            pltpu.VMEM((N_PAIRS, Q_SUPER, LANES), f32),
            pltpu.VMEM((N_PAIRS, Q_SUPER, LANES), f32),
        ],
        compiler_params=pltpu.CompilerParams(
            dimension_semantics=("arbitrary", "arbitrary"),
            vmem_limit_bytes=VMEM_LIMIT_BYTES),
        name="attn_prompt",
    )(_prompt_bias_table(rel_bias), slabs, slabs, slabs, attn_out_g.reshape(1, D_A))


def _log_sigmoid(x):
    return -(jnp.maximum(-x, 0.0) + jnp.log1p(jnp.exp(-jnp.abs(x))))


def _mlstm_prompt_kernel(xb_ref, ob_ref, gt_ref, gbias_ref, cw_ref, cb_ref, wq_ref, wk_ref, mg_ref,
                         skip_ref, bias_ref, bias0_ref, zt_ref, ga_ref, kt_ref, vt_ref,
                         out_ref, c_out_ref, n_out_ref, m_out_ref, conv_out_ref, ot_ref, nk_ref, nv_ref,
                         conv_scr, c_scr, n_scr, m_scr):
    c_idx = pl.program_id(1)
    L = MLSTM_CHUNK

    _attn_step_body(pl.program_id(0) * pl.num_programs(1) + c_idx, bias_ref, bias0_ref, zt_ref, ga_ref,
                    kt_ref, vt_ref, ot_ref, nk_ref, nv_ref)

    @pl.when(c_idx == 0)
    def _init():
        conv_scr[0:SUBLANES, :] = jnp.zeros((SUBLANES, D_B_PAD), f32)
        c_scr[...] = jnp.zeros_like(c_scr)
        n_scr[...] = jnp.zeros_like(n_scr)
        m_scr[...] = jnp.zeros_like(m_scr)

    x = xb_ref[...]
    conv_scr[SUBLANES:SUBLANES + L, :] = x
    c = cb_ref[...] + x * cw_ref[CONV_W - 1:CONV_W, :]
    for i in range(CONV_W - 1):
        sh = CONV_W - 1 - i
        c = c + conv_scr[SUBLANES - sh:SUBLANES - sh + L, :] * cw_ref[i:i + 1, :]
    conv_scr[0:SUBLANES, :] = x[L - SUBLANES:, :]
    c_act = c * jax.nn.sigmoid(c)

    gts = gt_ref[...] + gbias_ref[...]
    logf = _log_sigmoid(gts)
    row = lax.broadcasted_iota(jnp.int32, (L, L), 0)
    col = lax.broadcasted_iota(jnp.int32, (L, L), 1)
    causal = row >= col
    a_all = jnp.dot(causal.astype(f32), logf, precision=lax.Precision.HIGHEST,
                    preferred_element_type=f32)
    gts_t = gts.T
    a_t = a_all.T

    for h in range(N_HEADS_B):
        sl = slice(h * HEAD_PAD, (h + 1) * HEAD_PAD)
        ch = c_act[:, sl]
        ch_bf = ch.astype(bf16)
        q = jnp.dot(ch_bf, wq_ref[h], preferred_element_type=f32)
        k = jnp.dot(ch_bf, wk_ref[h], preferred_element_type=f32) * (1.0 / math.sqrt(HEAD_DIM_B))
        v = x[:, sl]
        q_bf, k_bf = q.astype(bf16), k.astype(bf16)
        a_col = a_all[:, N_HEADS_B + h:N_HEADS_B + h + 1]
        i_col = gts[:, h:h + 1]
        a_row = a_t[N_HEADS_B + h:N_HEADS_B + h + 1, :]
        i_row = gts_t[h:h + 1, :]
        m_prev = m_scr[h]
        dmat = jnp.where(causal, a_col - a_row + i_row, NEG)
        g = a_col + m_prev
        m_t = jnp.maximum(g, jnp.max(dmat, axis=-1, keepdims=True))
        w_state = jnp.exp(g - m_t)
        s = lax.dot_general(q_bf, k_bf, (((1,), (1,)), ((), ())), preferred_element_type=f32)
        amat = s * jnp.exp(dmat - m_t)
        c_state = c_scr[h]
        inter = lax.dot_general(q_bf, c_state.astype(bf16), (((1,), (1,)), ((), ())),
                                preferred_element_type=f32)
        num = w_state * inter + jnp.dot(amat.astype(bf16), v.astype(bf16),
                                        preferred_element_type=f32)
        n_row = n_scr[h]
        den = (w_state * jnp.sum(q * n_row, axis=-1, keepdims=True)
               + jnp.sum(amat, axis=-1, keepdims=True))
        hh = num / jnp.maximum(jnp.abs(den), jnp.exp(-m_t))
        hn = hh * lax.rsqrt(jnp.sum(hh * hh, axis=-1, keepdims=True) * (1.0 / HEAD_DIM_B) + EPS)
        hn = hn * mg_ref[:, sl]
        out_ref[:, sl] = jax.nn.sigmoid(ob_ref[:, sl]) * (hn + skip_ref[:, sl] * ch)

        b_tot = a_col[L - 1:L, :]
        wl = b_tot - a_col + i_col
        m_new = jnp.maximum(b_tot + m_prev, jnp.max(wl, axis=0, keepdims=True))
        wk = jnp.exp(wl - m_new)
        decay = jnp.exp(b_tot + m_prev - m_new)
        upd = lax.dot_general((wk * v).astype(bf16), k_bf, (((0,), (0,)), ((), ())),
                              preferred_element_type=f32)
        c_scr[h] = decay * c_state + upd
        n_scr[h] = decay * n_row + jnp.sum(wk * k, axis=0, keepdims=True)
        m_scr[h] = m_new

    @pl.when(c_idx == pl.num_programs(1) - 1)
    def _final():
        c_out_ref[0] = c_scr[...]
        n_out_ref[0] = n_scr[...]
        m_out_ref[0] = m_scr[...]
        conv_out_ref[0] = x[L - SUBLANES:, :]


def _mlstm_prompt_attn_step(xb, ob, gates, gbias, cw, cb, wq, wk, mg, skip, batch, seq,
                            zt, cache_kt, cache_vt, rel_bias, attn_out_g):
    L = MLSTM_CHUNK
    assert seq % L == 0
    nc = seq // L
    nb, p = cache_kt.shape[0], cache_kt.shape[-1]
    assert nb == batch * nc, "one decode sequence per (prompt sequence, chunk) grid step"
    for (w, d) in DILATED_BRANCHES:
        assert w <= p and w // d == BLK
    nbr = len(DILATED_BRANCHES)
    bias, bias0 = _step_bias_rows(rel_bias, p)
    win = pl.BlockSpec((1, N_HEADS_A, HEAD_DIM_A, p), lambda bi, ci: (bi * nc + ci, 0, 0, 0))
    const4 = lambda shape: pl.BlockSpec(shape, lambda bi, ci: (0, 0, 0, 0))
    buf = jax.ShapeDtypeStruct(cache_kt.shape, f32)
    rows = lambda w: pl.BlockSpec((L, w), lambda bi, ci: (bi * nc + ci, 0))
    const2 = lambda shape: pl.BlockSpec(shape, lambda bi, ci: (0, 0))
    const3 = lambda shape: pl.BlockSpec(shape, lambda bi, ci: (0, 0, 0))
    state = lambda shape: pl.BlockSpec((1,) + shape, lambda bi, ci: (bi,) + (0,) * len(shape))
    return pl.pallas_call(
        _mlstm_prompt_kernel,
        out_shape=(
            jax.ShapeDtypeStruct((batch * seq, D_B_PAD), f32),
            jax.ShapeDtypeStruct((batch, N_HEADS_B, HEAD_PAD, HEAD_PAD), f32),
            jax.ShapeDtypeStruct((batch, N_HEADS_B, 1, HEAD_PAD), f32),
            jax.ShapeDtypeStruct((batch, N_HEADS_B, 1, 1), f32),
            jax.ShapeDtypeStruct((batch, SUBLANES, D_B_PAD), f32),
            jax.ShapeDtypeStruct((D_A, nb), f32), buf, buf,
        ),
        grid=(batch, nc),
        in_specs=[rows(D_B_PAD), rows(D_B_PAD), rows(LANES), const2((1, LANES)),
                  const2((CONV_W, D_B_PAD)), const2((1, D_B_PAD)),
                  const3((N_HEADS_B, HEAD_PAD, HEAD_PAD)), const3((N_HEADS_B, HEAD_PAD, HEAD_PAD)),
                  const2((1, D_B_PAD)), const2((1, D_B_PAD)),
                  const4((nbr, N_HEADS_A, 1, p)), const4((nbr, N_HEADS_A, 1, 1)),
                  const2((3 * D_A, nb)), const2((D_A, 1)), win, win],
        out_specs=(rows(D_B_PAD), state((N_HEADS_B, HEAD_PAD, HEAD_PAD)),
                   state((N_HEADS_B, 1, HEAD_PAD)), state((N_HEADS_B, 1, 1)),
                   state((SUBLANES, D_B_PAD)),
                   const2((D_A, nb)), win, win),
        scratch_shapes=[
            pltpu.VMEM((SUBLANES + L, D_B_PAD), f32),
            pltpu.VMEM((N_HEADS_B, HEAD_PAD, HEAD_PAD), f32),
            pltpu.VMEM((N_HEADS_B, 1, HEAD_PAD), f32),
            pltpu.VMEM((N_HEADS_B, 1, 1), f32),
        ],
        compiler_params=pltpu.CompilerParams(
            dimension_semantics=("arbitrary", "arbitrary"),
            vmem_limit_bytes=VMEM_LIMIT_BYTES),
        name="mlstm_prompt_attn_step",
    )(xb, ob, gates, gbias, cw, cb, wq, wk, mg, skip,
      bias, bias0, zt, attn_out_g.reshape(D_A, 1), cache_kt, cache_vt)


def _step_bias_rows(rel_bias, p):
    dist = p - jnp.arange(p)
    full = rel_bias[_t5_bucket(dist)].astype(f32).T
    rows = []
    for (w, d) in DILATED_BRANCHES:
        in_branch = (dist % d == 0) & (dist <= w)
        rows.append(jnp.where(in_branch[None, :], full, NEG))
    zero = rel_bias[_t5_bucket(jnp.zeros((1,), jnp.int32))].astype(f32).T
    nbr = len(DILATED_BRANCHES)
    return jnp.stack(rows)[:, :, None, :], jnp.broadcast_to(zero[None, :, None, :], (nbr, N_HEADS_A, 1, 1))


def _attn_step_body(b, bias_ref, bias0_ref, zt_ref, g_ref, kt_ref, vt_ref, ot_ref, nk_ref, nv_ref):
    p = kt_ref.shape[-1]
    nbr = len(DILATED_BRANCHES)
    scale = 1.0 / math.sqrt(HEAD_DIM_A)

    @pl.when(b == 0)
    def _init():
        ot_ref[...] = jnp.zeros_like(ot_ref)

    zt = zt_ref[...]
    is_b = lax.broadcasted_iota(jnp.int32, zt.shape, 1) == b
    col = jnp.sum(jnp.where(is_b, zt, 0.0), axis=1, keepdims=True)
    is_last = lax.broadcasted_iota(jnp.int32, (HEAD_DIM_A, p), 1) == p - 1
    o_cols = []
    for h in range(N_HEADS_A):
        q_col = col[h * HEAD_DIM_A:(h + 1) * HEAD_DIM_A] * scale
        kn_col = col[D_A + h * HEAD_DIM_A:D_A + (h + 1) * HEAD_DIM_A]
        vn_col = col[2 * D_A + h * HEAD_DIM_A:2 * D_A + (h + 1) * HEAD_DIM_A]
        kt = kt_ref[0, h]
        vt = vt_ref[0, h]
        nk_ref[0, h] = jnp.where(is_last, kn_col, pltpu.roll(kt, p - 1, axis=1))
        nv_ref[0, h] = jnp.where(is_last, vn_col, pltpu.roll(vt, p - 1, axis=1))
        s = jnp.sum(kt * q_col, axis=0, keepdims=True)
        s_new = jnp.sum(kn_col * q_col, axis=0, keepdims=True)
        sg = [s + bias_ref[g, h] for g in range(nbr)]
        s0 = [s_new + bias0_ref[g, h] for g in range(nbr)]
        m = functools.reduce(jnp.maximum, [jnp.max(x, axis=1, keepdims=True) for x in sg] + s0)
        pw = sum(jnp.exp(x - m) for x in sg)
        e0 = sum(jnp.exp(x - m) for x in s0)
        l = jnp.sum(pw, axis=1, keepdims=True) + e0
        o_cols.append((jnp.sum(vt * pw, axis=1, keepdims=True) + e0 * vn_col) / l)
    o = jnp.concatenate(o_cols, axis=0)
    ssq = jnp.sum(o * o, axis=0, keepdims=True)
    o = o * lax.rsqrt(ssq * (1.0 / D_A) + EPS) * g_ref[...]
    sel = lax.broadcasted_iota(jnp.int32, ot_ref.shape, 1) == b
    ot_ref[...] = jnp.where(sel, o, ot_ref[...])


N_ROWS = 3
ROW_W, ROW_A, ROW_R = range(N_ROWS)


def _mlstm_step_pre_kernel(xb_ref, gt_ref, gbias_ref, sc_ref, cw_ref, cb_ref, wq_ref, wk_ref, nt_ref, mt_ref,
                           cact_out, qt_out, kt_out, vt_out, vst_out, rows_out, nt_out, mt_out):
    E = HEAD_DIM_B
    x = xb_ref[...]
    c = cb_ref[...] + x * cw_ref[CONV_W - 1:CONV_W, :]
    for t in range(CONV_W - 1):
        c = c + sc_ref[t] * cw_ref[t:t + 1, :]
    c_act = c * jax.nn.sigmoid(c)
    cact_out[...] = c_act
    gts_t = (gt_ref[...] + gbias_ref[...]).T
    logf_t = _log_sigmoid(gts_t)
    for h in range(N_HEADS_B):
        sl = slice(h * HEAD_PAD, (h + 1) * HEAD_PAD)
        ch_bf = c_act[:, sl].astype(bf16)
        q = jnp.dot(ch_bf, wq_ref[h], preferred_element_type=f32)
        k = jnp.dot(ch_bf, wk_ref[h], preferred_element_type=f32) * (1.0 / math.sqrt(HEAD_DIM_B))
        qt = q.T[:E]
        kt = k.T[:E]
        vt = x[:, sl].T[:E]
        i_pre = gts_t[h:h + 1]
        a = logf_t[N_HEADS_B + h:N_HEADS_B + h + 1]
        m_old = mt_ref[h:h + 1]
        m_t = jnp.maximum(a + m_old, i_pre)
        w_state = jnp.exp(a + m_old - m_t)
        w_in = jnp.exp(i_pre - m_t)
        amat = jnp.sum(qt * kt, axis=0, keepdims=True) * w_in
        n_old = nt_ref[h]
        den = w_state * jnp.sum(n_old * qt, axis=0, keepdims=True) + amat
        nt_out[h] = w_state * n_old + w_in * kt
        mt_out[h:h + 1] = m_t
        qt_out[h] = qt
        kt_out[h] = kt
        vt_out[h] = vt
        vst_out[h] = w_in * vt
        rows_out[h, ROW_W:ROW_W + 1] = w_state
        rows_out[h, ROW_A:ROW_A + 1] = amat
        rows_out[h, ROW_R:ROW_R + 1] = 1.0 / jnp.maximum(jnp.abs(den), jnp.exp(-m_t))


def _mlstm_step_pre(xb, gts, gbias, sc_t, cw, cb, wq, wk, nt, mt):
    nb = xb.shape[0]
    hd = jax.ShapeDtypeStruct((N_HEADS_B, HEAD_DIM_B, nb), f32)
    return pl.pallas_call(
        _mlstm_step_pre_kernel,
        out_shape=(jax.ShapeDtypeStruct((nb, D_B_PAD), f32), hd, hd, hd, hd,
                   jax.ShapeDtypeStruct((N_HEADS_B, N_ROWS, nb), f32), hd,
                   jax.ShapeDtypeStruct((N_HEADS_B, nb), f32)),
        compiler_params=pltpu.CompilerParams(vmem_limit_bytes=VMEM_LIMIT_BYTES),
        name="mlstm_step_pre",
    )(xb, gts, gbias, sc_t, cw, cb, wq, wk, nt, mt)


ST_VT = 40


def _mlstm_step_state_kernel(c_ref, qt_ref, kt_ref, vt_ref, vst_ref, rows_ref, ob_ref, cact_ref, mg_ref,
                             skip_ref, c_out, out_ref, cq_scr):
    vb = pl.program_id(1)
    E = HEAD_DIM_B
    qt = qt_ref[0]
    kt = kt_ref[0]
    w_state = rows_ref[0, ROW_W:ROW_W + 1]

    def per_row(v, carry):
        c_old = c_ref[0, v]
        vs = vst_ref[0, pl.ds(vb * ST_VT + v, 1), :]
        c_out[0, v] = w_state * c_old + vs * kt
        cq_scr[pl.ds(vb * ST_VT + v, 1), :] = jnp.sum(c_old * qt, axis=0, keepdims=True)
        return carry

    lax.fori_loop(0, ST_VT, per_row, 0)

    @pl.when(vb == pl.num_programs(1) - 1)
    def _finish_head():
        ht = (w_state * cq_scr[...] + rows_ref[0, ROW_A:ROW_A + 1] * vt_ref[0]) * rows_ref[0, ROW_R:ROW_R + 1]
        hh = jnp.concatenate([ht, jnp.zeros((HEAD_PAD - E, ht.shape[1]), f32)], axis=0).T
        hn = hh * lax.rsqrt(jnp.sum(hh * hh, axis=-1, keepdims=True) * (1.0 / E) + EPS) * mg_ref[0]
        out_ref[0] = jax.nn.sigmoid(ob_ref[0]) * (hn + skip_ref[0] * cact_ref[0])


def _mlstm_step_state(ct, qt, kt, vt, vst, rows, ob3, cact3, mg3, skip3):
    nh, e, _, nb = ct.shape
    assert e % ST_VT == 0
    head = lambda shape: pl.BlockSpec((1,) + shape, lambda h, v: (h,) + (0,) * len(shape))
    cblk = pl.BlockSpec((1, ST_VT, e, nb), lambda h, v: (h, v, 0, 0))
    return pl.pallas_call(
        _mlstm_step_state_kernel,
        out_shape=(jax.ShapeDtypeStruct(ct.shape, f32), jax.ShapeDtypeStruct((nh, nb, HEAD_PAD), f32)),
        grid=(nh, e // ST_VT),
        in_specs=[cblk, head((e, nb)), head((e, nb)), head((e, nb)), head((e, nb)), head((N_ROWS, nb)),
                  head((nb, HEAD_PAD)), head((nb, HEAD_PAD)), head((1, HEAD_PAD)), head((1, HEAD_PAD))],
        out_specs=(cblk, head((nb, HEAD_PAD))),
        scratch_shapes=[pltpu.VMEM((e, nb), f32)],
        compiler_params=pltpu.CompilerParams(
            dimension_semantics=("arbitrary", "arbitrary"),
            vmem_limit_bytes=VMEM_LIMIT_BYTES),
        name="mlstm_step_state",
    )(ct, qt, kt, vt, vst, rows, ob3, cact3, mg3, skip3)


def kernel(x_prompt, x_sample, cache_win_k, cache_win_v, state_conv, state_C, state_n, state_m,
           rel_bias, norm1_g, w_in, gate_bias, conv_w, conv_b, wq_head, wk_head, attn_out_g,
           mh_norm_g, skip, w_out, norm2_g, w_ff1, w_ff2, final_g):
    Bp, Sp, _ = x_prompt.shape
    Bs, Ss, _ = x_sample.shape
    assert Ss == 1
    g1 = norm1_g[0].reshape(1, D_MODEL)
    g2 = norm2_g[0].reshape(1, D_MODEL)
    gf = final_g.reshape(1, D_MODEL)
    w_in_pad = _pad_w_in(w_in[0]).astype(bf16)
    wo = jnp.concatenate([w_out[0][:D_A], jnp.zeros((LANES, D_MODEL), f32),
                          _pad_heads(w_out[0][D_A:].T).T], axis=0).astype(bf16)
    w1 = w_ff1[0].astype(bf16)
    w2 = w_ff2[0].astype(bf16)
    hpad = HEAD_PAD - HEAD_DIM_B
    wq_p = jnp.pad(wq_head[0], ((0, 0), (0, hpad), (0, hpad))).astype(bf16)
    wk_p = jnp.pad(wk_head[0], ((0, 0), (0, hpad), (0, hpad))).astype(bf16)
    gbias = jnp.pad(gate_bias[0], (0, LANES - N_GATES)).reshape(1, LANES)
    cw_p = _pad_heads(conv_w[0])
    cb_p = _pad_heads(conv_b[0]).reshape(1, D_B_PAD)
    mg_p = _pad_heads(mh_norm_g[0]).reshape(1, D_B_PAD)
    skip_p = _pad_heads(skip[0]).reshape(1, D_B_PAD)

    xp2 = x_prompt.reshape(Bp * Sp, D_MODEL)
    P = min(WINDOW_MAX, Sp)
    slabs, k_win, v_win, xb, ob, gts = _norm_inproj(xp2, g1, w_in_pad, tm=512, seq=Sp, win=P)
    out_a = _attn_prompt(slabs, rel_bias, attn_out_g[0], Bp, Sp)
    xs2 = x_sample.reshape(Bs, D_MODEL)
    zt_s, xb_s, ob_s, gts_s = _norm_inproj_step(xs2, g1, w_in[0][:, :3 * D_A].T.astype(bf16), w_in_pad)
    out_b, c_p, n_p, m_p, tail_p, oat_s, nkt, nvt = _mlstm_prompt_attn_step(
        xb, ob, gts, gbias, cw_p, cb_p, wq_p, wk_p, mg_p, skip_p, Bp, Sp,
        zt_s, cache_win_k[0].transpose(0, 2, 3, 1), cache_win_v[0].transpose(0, 2, 3, 1), rel_bias,
        attn_out_g[0])
    y_p = _out_ffn(xp2, out_a, out_b, wo, g2, w1, w2, gf, tm=512).reshape(Bp, Sp, D_MODEL)
    win5 = lambda t: t.reshape(N_PAIRS, Bp, P, LANES).transpose(1, 2, 0, 3).reshape(
        1, Bp, P, N_HEADS_A, HEAD_DIM_A)
    st_p = (win5(k_win), win5(v_win), _unpad_heads(tail_p[:, SUBLANES - (CONV_W - 1):])[None],
            c_p[:, :, :HEAD_DIM_B, :HEAD_DIM_B][None], n_p[:, :, 0, :HEAD_DIM_B][None],
            m_p[:, :, 0, 0][None])

    oa_s = oat_s.reshape(N_PAIRS, LANES, Bs).transpose(0, 2, 1)
    new_k = nkt.transpose(0, 3, 1, 2)[None]
    new_v = nvt.transpose(0, 3, 1, 2)[None]
    sc_t = _pad_heads(state_conv[0].transpose(1, 0, 2))
    cact_s, qt_s, kt_s, vt_s, vst_s, rows_s, nt_s, mt_s = _mlstm_step_pre(
        xb_s, gts_s, gbias, sc_t, cw_p, cb_p, wq_p, wk_p,
        state_n[0].transpose(1, 2, 0), state_m[0].T)
    heads3 = lambda t: t.reshape(t.shape[0], N_HEADS_B, HEAD_PAD).transpose(1, 0, 2)
    ct_s, outb3 = _mlstm_step_state(state_C[0].transpose(1, 2, 3, 0), qt_s, kt_s, vt_s, vst_s, rows_s,
                                    heads3(ob_s), heads3(cact_s), heads3(mg_p), heads3(skip_p))
    outb_s = outb3.transpose(1, 0, 2).reshape(Bs, D_B_PAD)
    y_s = _out_ffn(xs2, oa_s, outb_s, wo, g2, w1, w2, gf, tm=Bs).reshape(Bs, Ss, D_MODEL)
    new_conv = jnp.concatenate([state_conv[0][:, 1:], _unpad_heads(xb_s)[:, None]], axis=1)
    st_s = (new_k, new_v, new_conv[None], ct_s.transpose(3, 0, 1, 2)[None],
            nt_s.transpose(2, 0, 1)[None], mt_s.T[None])
    return (y_p, y_s) + st_p + st_s
```

```python
import functools
import math

import jax
import jax.numpy as jnp
from jax import lax
from jax.experimental import pallas as pl
from jax.experimental.pallas import tpu as pltpu

D_MODEL = 1024
HEAD_DIM_A = 64
N_HEADS_A = 6
D_A = N_HEADS_A * HEAD_DIM_A
D_B = D_MODEL - D_A
N_HEADS_B = 4
HEAD_DIM_B = D_B // N_HEADS_B
DILATED_BRANCHES = ((128, 1), (512, 4), (2048, 16))
WINDOW_MAX = 2048
N_BUCKETS = 32
REL_MAX_DIST = 2048
CONV_W = 4
MLSTM_CHUNK = 128
D_FF = 4 * D_MODEL
N_GATES = 2 * N_HEADS_B
D_IN = 3 * D_A + 2 * D_B + N_GATES
SPLITS = [D_A, 2 * D_A, 3 * D_A, 3 * D_A + D_B, 3 * D_A + 2 * D_B]
EPS = 1e-6
NEG = -1e30

LANES = 128
SUBLANES = 8
VMEM_LIMIT_BYTES = 56 * 1024 * 1024

N_PAIRS = D_A // LANES
N_SLABS = 3 * N_PAIRS
HEAD_PAD = 2 * LANES
D_B_PAD = N_HEADS_B * HEAD_PAD
D_IN_PAD = 3 * D_A + 2 * D_B_PAD + LANES
D_O_PAD = D_A + LANES + D_B_PAD
BLK = 128
RES = DILATED_BRANCHES[-1][1]
Q_SUPER = BLK * RES

f32 = jnp.float32
bf16 = jnp.bfloat16


def _round_up(n, m):
    return -(-n // m) * m


def _rms(xf, g):
    return xf * lax.rsqrt(jnp.mean(xf * xf, axis=-1, keepdims=True) + EPS) * g


def _pad_heads(t):
    t = t.reshape(t.shape[:-1] + (N_HEADS_B, HEAD_DIM_B))
    t = jnp.pad(t, [(0, 0)] * (t.ndim - 1) + [(0, HEAD_PAD - HEAD_DIM_B)])
    return t.reshape(t.shape[:-2] + (D_B_PAD,))


def _unpad_heads(t):
    t = t.reshape(t.shape[:-1] + (N_HEADS_B, HEAD_PAD))[..., :HEAD_DIM_B]
    return t.reshape(t.shape[:-2] + (D_B,))


def _norm_inproj_kernel(x_ref, g_ref, w_ref, slab_ref, kwin_ref, vwin_ref, xb_ref, ob_ref, gt_ref):
    h = _rms(x_ref[...], g_ref[...]).astype(bf16)
    qkv = jnp.dot(h, w_ref[:, 0:3 * D_A], preferred_element_type=f32)
    for j in range(N_SLABS):
        slab_ref[j] = qkv[:, j * LANES:(j + 1) * LANES]
    for p in range(N_PAIRS):
        kwin_ref[p] = qkv[:, (N_PAIRS + p) * LANES:(N_PAIRS + p + 1) * LANES]
        vwin_ref[p] = qkv[:, (2 * N_PAIRS + p) * LANES:(2 * N_PAIRS + p + 1) * LANES]
    o = 3 * D_A
    xb_ref[...] = jnp.dot(h, w_ref[:, o:o + D_B_PAD], preferred_element_type=f32)
    o += D_B_PAD
    ob_ref[...] = jnp.dot(h, w_ref[:, o:o + D_B_PAD], preferred_element_type=f32)
    o += D_B_PAD
    gt_ref[...] = jnp.dot(h, w_ref[:, o:o + LANES], preferred_element_type=f32)


def _norm_inproj(x, g, w_pad_bf16, tm, seq, win):
    m, d = x.shape
    assert m % seq == 0 and seq % tm == 0 and win % tm == 0
    tiles_seq, tiles_win = seq // tm, win // tm
    row = lambda w: pl.BlockSpec((tm, w), lambda i: (i, 0))

    def win_block(i):
        return (0, (i // tiles_seq) * tiles_win + jnp.maximum(i % tiles_seq - (tiles_seq - tiles_win), 0), 0)

    return pl.pallas_call(
        _norm_inproj_kernel,
        out_shape=(
            jax.ShapeDtypeStruct((N_SLABS, m, LANES), f32),
            jax.ShapeDtypeStruct((N_PAIRS, (m // seq) * win, LANES), f32),
            jax.ShapeDtypeStruct((N_PAIRS, (m // seq) * win, LANES), f32),
            jax.ShapeDtypeStruct((m, D_B_PAD), f32),
            jax.ShapeDtypeStruct((m, D_B_PAD), f32),
            jax.ShapeDtypeStruct((m, LANES), f32),
        ),
        grid=(m // tm,),
        in_specs=[
            row(d),
            pl.BlockSpec((1, d), lambda i: (0, 0)),
            pl.BlockSpec((d, D_IN_PAD), lambda i: (0, 0), pipeline_mode=pl.Buffered(1)),
        ],
        out_specs=(
            pl.BlockSpec((N_SLABS, tm, LANES), lambda i: (0, i, 0)),
            pl.BlockSpec((N_PAIRS, tm, LANES), win_block),
            pl.BlockSpec((N_PAIRS, tm, LANES), win_block),
            row(D_B_PAD), row(D_B_PAD), row(LANES),
        ),
        compiler_params=pltpu.CompilerParams(
            dimension_semantics=("arbitrary",),
            vmem_limit_bytes=VMEM_LIMIT_BYTES),
        name="norm_inproj",
    )(x, g, w_pad_bf16)


def _norm_inproj_step_kernel(x_ref, g_ref, wt_ref, w_ref, zt_ref, xb_ref, ob_ref, gt_ref):
    h = _rms(x_ref[...], g_ref[...]).astype(bf16)
    zt_ref[...] = lax.dot_general(wt_ref[...], h, (((1,), (1,)), ((), ())), preferred_element_type=f32)
    o = 3 * D_A
    xb_ref[...] = jnp.dot(h, w_ref[:, o:o + D_B_PAD], preferred_element_type=f32)
    o += D_B_PAD
    ob_ref[...] = jnp.dot(h, w_ref[:, o:o + D_B_PAD], preferred_element_type=f32)
    o += D_B_PAD
    gt_ref[...] = jnp.dot(h, w_ref[:, o:o + LANES], preferred_element_type=f32)


def _norm_inproj_step(x, g, wqkv_t_bf16, w_pad_bf16):
    nb = x.shape[0]
    wide = jax.ShapeDtypeStruct((nb, D_B_PAD), f32)
    return pl.pallas_call(
        _norm_inproj_step_kernel,
        out_shape=(jax.ShapeDtypeStruct((3 * D_A, nb), f32), wide, wide,
                   jax.ShapeDtypeStruct((nb, LANES), f32)),
        compiler_params=pltpu.CompilerParams(vmem_limit_bytes=VMEM_LIMIT_BYTES),
        name="norm_inproj_step",
    )(x, g, wqkv_t_bf16, w_pad_bf16)


def _pad_w_in(w_in):
    qkv = w_in[:, :3 * D_A]
    xb = _pad_heads(w_in[:, SPLITS[2]:SPLITS[3]])
    ob = _pad_heads(w_in[:, SPLITS[3]:SPLITS[4]])
    gt = jnp.pad(w_in[:, SPLITS[4]:], ((0, 0), (0, LANES - N_GATES)))
    return jnp.concatenate([qkv, xb, ob, gt], axis=1)


def _out_ffn_kernel(x_ref, oa_ref, ob_ref, wo_ref, g2_ref, w1_ref, w2_ref, gf_ref, y_ref, *, ff_chunk):
    o = jnp.concatenate([oa_ref[p].astype(bf16) for p in range(N_PAIRS)]
                        + [jnp.zeros((ob_ref.shape[0], LANES), bf16), ob_ref[...].astype(bf16)], axis=1)
    x1 = x_ref[...] + jnp.dot(o, wo_ref[...], preferred_element_type=f32)
    h2 = _rms(x1, g2_ref[...]).astype(bf16)
    acc = x1
    for c in range(D_FF // ff_chunk):
        u = jnp.dot(h2, w1_ref[:, c * ff_chunk:(c + 1) * ff_chunk], preferred_element_type=f32)
        u = jnp.square(jnp.maximum(u, 0.0)).astype(bf16)
        acc = acc + jnp.dot(u, w2_ref[c * ff_chunk:(c + 1) * ff_chunk, :],
                            preferred_element_type=f32)
    y_ref[...] = _rms(acc, gf_ref[...])


def _out_ffn(x, oa, ob, wo, g2, w1, w2, gf, tm, ff_chunk=1024):
    m, d = x.shape
    assert wo.shape[0] == D_O_PAD
    const = lambda i: (0, 0)
    single = dict(pipeline_mode=pl.Buffered(1))
    row = lambda w: pl.BlockSpec((tm, w), lambda i: (i, 0))
    return pl.pallas_call(
        functools.partial(_out_ffn_kernel, ff_chunk=ff_chunk),
        out_shape=jax.ShapeDtypeStruct((m, d), f32),
        grid=(m // tm,),
        in_specs=[
            row(d), pl.BlockSpec((N_PAIRS, tm, LANES), lambda i: (0, i, 0)), row(D_B_PAD),
            pl.BlockSpec((D_O_PAD, d), const, **single),
            pl.BlockSpec((1, d), const),
            pl.BlockSpec((d, D_FF), const, **single),
            pl.BlockSpec((D_FF, d), const, **single),
            pl.BlockSpec((1, d), const),
        ],
        out_specs=row(d),
        compiler_params=pltpu.CompilerParams(
            dimension_semantics=("arbitrary",),
            vmem_limit_bytes=VMEM_LIMIT_BYTES),
        name="outproj_ffn",
    )(x, oa, ob, wo, g2, w1, w2, gf)


def _t5_bucket(dist):
    max_exact = N_BUCKETS // 2
    df = jnp.maximum(dist, 1).astype(jnp.float32)
    large = max_exact + (jnp.log(df / max_exact) / math.log(REL_MAX_DIST / max_exact)
                         * (N_BUCKETS - max_exact)).astype(jnp.int32)
    large = jnp.minimum(large, N_BUCKETS - 1)
    return jnp.where(dist < max_exact, dist, large)


def _prompt_bias_table(rel_bias):
    period = 3 * BLK + 1
    r = jnp.arange(period)
    t = jnp.where(r < 2 * BLK, r, r - period)
    j = BLK - t
    band = (j >= 0) & (j <= BLK)
    tabs = []
    for (_, d) in DILATED_BRANCHES:
        vals = rel_bias[_t5_bucket(jnp.clip(j, 0, None) * d)].astype(f32)
        w = jnp.where(band[:, None], vals, NEG).T
        flat = jnp.tile(w, (1, BLK))[:, :BLK * (period - 1)]
        toe = flat.reshape(N_HEADS_A, BLK, period - 1)[:, :, :2 * BLK]
        sub = RES // d
        if sub > 1:
            pos = jnp.arange(BLK)
            nat = (pos % (BLK // sub)) * sub + pos // (BLK // sub)
            pq = jax.nn.one_hot(nat, BLK, dtype=f32)
            pk = jnp.kron(jnp.eye(2, dtype=f32), pq)
            toe = jnp.einsum('pq,hqk->hpk', pq, toe, precision=lax.Precision.HIGHEST)
            toe = jnp.einsum('hpk,jk->hpj', toe, pk, precision=lax.Precision.HIGHEST)
        tabs.append(toe.reshape(N_PAIRS, 2 * BLK, 2 * BLK))
    return jnp.stack(tabs)


def _attn_prompt_kernel(bias_ref, q_ref, k_ref, v_ref, g_ref, o_ref, q_scr, kv_scr, m_scr, l_scr, acc_scr):
    n = pl.program_id(1)
    step = pl.program_id(0) * pl.num_programs(1) + n
    cur = lax.rem(step, 2)
    prv = 1 - cur

    @pl.when(step == 0)
    def _init():
        kv_scr[1] = jnp.zeros(kv_scr.shape[1:], f32)

    def deinterleave(r, carry):
        rows = pl.ds(pl.multiple_of(r * BLK, BLK), BLK)
        for p in range(N_PAIRS):
            q_scr[p, rows, :] = q_ref.at[p][pl.ds(r, BLK, stride=RES), :]
            kv_scr[cur, p, rows, :] = k_ref.at[p][pl.ds(r, BLK, stride=RES), :]
            kv_scr[cur, N_PAIRS + p, rows, :] = v_ref.at[p][pl.ds(r, BLK, stride=RES), :]
        return carry

    lax.fori_loop(0, RES, deinterleave, 0, unroll=4)

    lane = lax.broadcasted_iota(jnp.int32, (BLK, LANES), 1)
    low = lane < HEAD_DIM_A
    kcol = lax.broadcasted_iota(jnp.int32, (2 * BLK, 2 * BLK), 1)
    first_extra = jnp.where((kcol < BLK) & (n == 0), NEG, 0.0).astype(f32)

    def job(g, d, buf_a, off_a, off_b, res, extra):
        sub = RES // d
        run = BLK // sub

        def starts(off):
            return [pl.multiple_of((c * d + res) * BLK + off, SUBLANES) for c in range(sub)]

        def gather(ref, lead, off):
            return jnp.concatenate([ref[lead + (pl.ds(s, run), slice(None))] for s in starts(off)], axis=0)

        def scatter(ref, p, val):
            for c, s in enumerate(starts(off_b)):
                ref[p, pl.ds(s, run), :] = val[c * run:(c + 1) * run]

        for p in range(N_PAIRS):
            q2 = gather(q_scr, (p,), off_b) * (1.0 / math.sqrt(HEAD_DIM_A))
            qcat = jnp.concatenate([jnp.where(low, q2, 0.0), jnp.where(low, 0.0, q2)], axis=0)
            kk = jnp.concatenate([gather(kv_scr, (buf_a, p), off_a), gather(kv_scr, (cur, p), off_b)], axis=0)
            vv = jnp.concatenate([gather(kv_scr, (buf_a, N_PAIRS + p), off_a),
                                  gather(kv_scr, (cur, N_PAIRS + p), off_b)], axis=0)
            s = lax.dot_general(qcat.astype(bf16), kk.astype(bf16), (((1,), (1,)), ((), ())),
                                preferred_element_type=f32)
            s = s + bias_ref[g, p]
            if extra is not None:
                s = s + extra
            m = jnp.max(s, axis=-1, keepdims=True)
            e = jnp.exp(s - m)
            l = jnp.sum(e, axis=-1, keepdims=True)
            o = jnp.dot(e.astype(bf16), vv.astype(bf16), preferred_element_type=f32)
            o_t = jnp.where(low, o[:BLK], o[BLK:])
            m_t = jnp.where(low, m[:BLK], m[BLK:])
            l_t = jnp.where(low, l[:BLK], l[BLK:])
            if g == 0:
                scatter(acc_scr, p, o_t)
                scatter(m_scr, p, m_t)
                scatter(l_scr, p, l_t)
            else:
                m_o = gather(m_scr, (p,), off_b)
                m_n = jnp.maximum(m_o, m_t)
                al = jnp.exp(m_o - m_n)
                be = jnp.exp(m_t - m_n)
                scatter(acc_scr, p, al * gather(acc_scr, (p,), off_b) + be * o_t)
                scatter(l_scr, p, al * gather(l_scr, (p,), off_b) + be * l_t)
                scatter(m_scr, p, m_n)

    for g, (_, d) in enumerate(DILATED_BRANCHES):
        nu = Q_SUPER // (BLK * d)
        run = BLK * d // RES
        log_d = d.bit_length() - 1

        def first(res, carry, g=g, d=d, nu=nu, run=run):
            job(g, d, prv, (nu - 1) * run, 0, res, first_extra)
            return carry

        def rest(idx, carry, g=g, d=d, run=run, log_d=log_d):
            u = 1 + lax.shift_right_logical(idx, log_d)
            res = jnp.bitwise_and(idx, d - 1)
            job(g, d, cur, (u - 1) * run, u * run, res, None)
            return carry

        if d == 1:
            first(0, 0)
        else:
            lax.fori_loop(0, d, first, 0, unroll=4)
        if nu > 1:
            lax.fori_loop(0, (nu - 1) * d, rest, 0, unroll=5 if d == 1 else 4)

    def fin(r, carry):
        rows = pl.ds(pl.multiple_of(r * BLK, BLK), BLK)
        os_ = [acc_scr[p, rows, :] / l_scr[p, rows, :] for p in range(N_PAIRS)]
        ss = sum(jnp.sum(o * o, axis=-1, keepdims=True) for o in os_)
        sc = lax.rsqrt(ss * (1.0 / D_A) + EPS)
        for p in range(N_PAIRS):
            o_ref.at[p][pl.ds(r, BLK, stride=RES), :] = os_[p] * sc * g_ref[:, p * LANES:(p + 1) * LANES]
        return carry

    lax.fori_loop(0, RES, fin, 0, unroll=4)


def _attn_prompt(slabs, rel_bias, attn_out_g, batch, seq):
    assert seq % Q_SUPER == 0
    nsb = seq // Q_SUPER
    nbr = len(DILATED_BRANCHES)
    slab = lambda grp: pl.BlockSpec((N_PAIRS, Q_SUPER, LANES), lambda bi, ni: (grp, bi * nsb + ni, 0))
    return pl.pallas_call(
        _attn_prompt_kernel,
        out_shape=jax.ShapeDtypeStruct((N_PAIRS, batch * seq, LANES), f32),
        grid=(batch, nsb),
        in_specs=[
            pl.BlockSpec((nbr, N_PAIRS, 2 * BLK, 2 * BLK), lambda bi, ni: (0, 0, 0, 0),
                         pipeline_mode=pl.Buffered(1)),
            slab(0), slab(1), slab(2),
            pl.BlockSpec((1, D_A), lambda bi, ni: (0, 0)),
        ],
        out_specs=pl.BlockSpec((N_PAIRS, Q_SUPER, LANES), lambda bi, ni: (0, bi * nsb + ni, 0)),
        scratch_shapes=[
            pltpu.VMEM((N_PAIRS, Q_SUPER, LANES), f32),
            pltpu.VMEM((2, 2 * N_PAIRS, Q_SUPER, LANES), f32),
            pltpu.VMEM((N_PAIRS, Q_SUPER, LANES), f32),
            pltpu.VMEM((N_PAIRS, Q_SUPER, LANES), f32),
            pltpu.VMEM((N_PAIRS, Q_SUPER, LANES), f32),
        ],
        compiler_params=pltpu.CompilerParams(
            dimension_semantics=("arbitrary", "arbitrary"),
            vmem_limit_bytes=VMEM_LIMIT_BYTES),
        name="attn_prompt",
    )(_prompt_bias_table(rel_bias), slabs, slabs, slabs, attn_out_g.reshape(1, D_A))


def _log_sigmoid(x):
    return -(jnp.maximum(-x, 0.0) + jnp.log1p(jnp.exp(-jnp.abs(x))))


def _mlstm_prompt_kernel(xb_ref, ob_ref, gt_ref, gbias_ref, cw_ref, cb_ref, wq_ref, wk_ref, mg_ref,
                         skip_ref, bias_ref, cnt_ref, bias0_ref, zt_ref, ga_ref, kt_ref, vt_ref,
                         out_ref, c_out_ref, n_out_ref, m_out_ref, conv_out_ref, ot_ref, nk_ref, nv_ref,
                         conv_scr, c_scr, n_scr, m_scr):
    c_idx = pl.program_id(1)
    L = MLSTM_CHUNK

    seq_idx = pl.program_id(0) * pl.num_programs(1) + c_idx

    @pl.when(seq_idx == 0)
    def _init_decode():
        ot_ref[...] = jnp.zeros_like(ot_ref)

    new_col = _new_token_column(seq_idx, zt_ref)
    _window_roll(new_col, kt_ref, vt_ref, nk_ref, nv_ref)
    _attn_step_math(seq_idx, new_col, bias_ref, cnt_ref, bias0_ref, ga_ref, kt_ref, vt_ref, ot_ref)

    @pl.when(c_idx == 0)
    def _init():
        conv_scr[0:SUBLANES, :] = jnp.zeros((SUBLANES, D_B_PAD), f32)
        c_scr[...] = jnp.zeros_like(c_scr)
        n_scr[...] = jnp.zeros_like(n_scr)
        m_scr[...] = jnp.zeros_like(m_scr)

    x = xb_ref[...]
    conv_scr[SUBLANES:SUBLANES + L, :] = x
    c = cb_ref[...] + x * cw_ref[CONV_W - 1:CONV_W, :]
    for i in range(CONV_W - 1):
        sh = CONV_W - 1 - i
        c = c + conv_scr[SUBLANES - sh:SUBLANES - sh + L, :] * cw_ref[i:i + 1, :]
    conv_scr[0:SUBLANES, :] = x[L - SUBLANES:, :]
    c_act = c * jax.nn.sigmoid(c)

    gts = gt_ref[...] + gbias_ref[...]
    logf = _log_sigmoid(gts)
    row = lax.broadcasted_iota(jnp.int32, (L, L), 0)
    col = lax.broadcasted_iota(jnp.int32, (L, L), 1)
    causal = row >= col
    a_all = jnp.dot(causal.astype(f32), logf, precision=lax.Precision.HIGHEST,
                    preferred_element_type=f32)
    gts_t = gts.T
    a_t = a_all.T

    for h in range(N_HEADS_B):
        sl = slice(h * HEAD_PAD, (h + 1) * HEAD_PAD)
        ch = c_act[:, sl]
        ch_bf = ch.astype(bf16)
        q = jnp.dot(ch_bf, wq_ref[h], preferred_element_type=f32)
        k = jnp.dot(ch_bf, wk_ref[h], preferred_element_type=f32) * (1.0 / math.sqrt(HEAD_DIM_B))
        v = x[:, sl]
        q_bf, k_bf = q.astype(bf16), k.astype(bf16)
        a_col = a_all[:, N_HEADS_B + h:N_HEADS_B + h + 1]
        i_col = gts[:, h:h + 1]
        a_row = a_t[N_HEADS_B + h:N_HEADS_B + h + 1, :]
        i_row = gts_t[h:h + 1, :]
        m_prev = m_scr[h]
        dmat = jnp.where(causal, a_col - a_row + i_row, NEG)
        g = a_col + m_prev
        m_t = jnp.maximum(g, jnp.max(dmat, axis=-1, keepdims=True))
        w_state = jnp.exp(g - m_t)
        s = lax.dot_general(q_bf, k_bf, (((1,), (1,)), ((), ())), preferred_element_type=f32)
        amat = s * jnp.exp(dmat - m_t)
        c_state = c_scr[h]
        inter = lax.dot_general(q_bf, c_state.astype(bf16), (((1,), (1,)), ((), ())),
                                preferred_element_type=f32)
        num = w_state * inter + jnp.dot(amat.astype(bf16), v.astype(bf16),
                                        preferred_element_type=f32)
        n_row = n_scr[h]
        den = (w_state * jnp.sum(q * n_row, axis=-1, keepdims=True)
               + jnp.sum(amat, axis=-1, keepdims=True))
        hh = num / jnp.maximum(jnp.abs(den), jnp.exp(-m_t))
        hn = hh * lax.rsqrt(jnp.sum(hh * hh, axis=-1, keepdims=True) * (1.0 / HEAD_DIM_B) + EPS)
        hn = hn * mg_ref[:, sl]
        out_ref[:, sl] = jax.nn.sigmoid(ob_ref[:, sl]) * (hn + skip_ref[:, sl] * ch)

        b_tot = a_col[L - 1:L, :]
        wl = b_tot - a_col + i_col
        m_new = jnp.maximum(b_tot + m_prev, jnp.max(wl, axis=0, keepdims=True))
        wk = jnp.exp(wl - m_new)
        decay = jnp.exp(b_tot + m_prev - m_new)
        upd = lax.dot_general((wk * v).astype(bf16), k_bf, (((0,), (0,)), ((), ())),
                              preferred_element_type=f32)
        c_scr[h] = decay * c_state + upd
        n_scr[h] = decay * n_row + jnp.sum(wk * k, axis=0, keepdims=True)
        m_scr[h] = m_new

    @pl.when(c_idx == pl.num_programs(1) - 1)
    def _final():
        c_out_ref[0] = c_scr[...]
        n_out_ref[0] = n_scr[...]
        m_out_ref[0] = m_scr[...]
        conv_out_ref[0] = x[L - SUBLANES:, :]


def _mlstm_prompt_attn_step(xb, ob, gates, gbias, cw, cb, wq, wk, mg, skip, batch, seq,
                            zt, cache_kt, cache_vt, rel_bias, attn_out_g):
    L = MLSTM_CHUNK
    assert seq % L == 0
    nc = seq // L
    nb, p = cache_kt.shape[0], cache_kt.shape[-1]
    assert nb == batch * nc, "one decode sequence per (prompt sequence, chunk) grid step"
    for (w, d) in DILATED_BRANCHES:
        assert w <= p and w // d == BLK
    bias, count, bias0 = _step_bias_rows(rel_bias, p)
    win = pl.BlockSpec((1, N_HEADS_A, HEAD_DIM_A, p), lambda bi, ci: (bi * nc + ci, 0, 0, 0))
    buf = jax.ShapeDtypeStruct(cache_kt.shape, f32)
    rows = lambda w: pl.BlockSpec((L, w), lambda bi, ci: (bi * nc + ci, 0))
    const2 = lambda shape: pl.BlockSpec(shape, lambda bi, ci: (0, 0))
    const3 = lambda shape: pl.BlockSpec(shape, lambda bi, ci: (0, 0, 0))
    state = lambda shape: pl.BlockSpec((1,) + shape, lambda bi, ci: (bi,) + (0,) * len(shape))
    return pl.pallas_call(
        _mlstm_prompt_kernel,
        out_shape=(
            jax.ShapeDtypeStruct((batch * seq, D_B_PAD), f32),
            jax.ShapeDtypeStruct((batch, N_HEADS_B, HEAD_PAD, HEAD_PAD), f32),
            jax.ShapeDtypeStruct((batch, N_HEADS_B, 1, HEAD_PAD), f32),
            jax.ShapeDtypeStruct((batch, N_HEADS_B, 1, 1), f32),
            jax.ShapeDtypeStruct((batch, SUBLANES, D_B_PAD), f32),
            jax.ShapeDtypeStruct((D_A, nb), f32), buf, buf,
        ),
        grid=(batch, nc),
        in_specs=[rows(D_B_PAD), rows(D_B_PAD), rows(LANES), const2((1, LANES)),
                  const2((CONV_W, D_B_PAD)), const2((1, D_B_PAD)),
                  const3((N_HEADS_B, HEAD_PAD, HEAD_PAD)), const3((N_HEADS_B, HEAD_PAD, HEAD_PAD)),
                  const2((1, D_B_PAD)), const2((1, D_B_PAD)),
                  const3((N_HEADS_A, 1, p)), const2((1, p)), const3((N_HEADS_A, 1, 1)),
                  const2((3 * D_A, nb)), const2((D_A, 1)), win, win],
        out_specs=(rows(D_B_PAD), state((N_HEADS_B, HEAD_PAD, HEAD_PAD)),
                   state((N_HEADS_B, 1, HEAD_PAD)), state((N_HEADS_B, 1, 1)),
                   state((SUBLANES, D_B_PAD)),
                   const2((D_A, nb)), win, win),
        scratch_shapes=[
            pltpu.VMEM((SUBLANES + L, D_B_PAD), f32),
            pltpu.VMEM((N_HEADS_B, HEAD_PAD, HEAD_PAD), f32),
            pltpu.VMEM((N_HEADS_B, 1, HEAD_PAD), f32),
            pltpu.VMEM((N_HEADS_B, 1, 1), f32),
        ],
        compiler_params=pltpu.CompilerParams(
            dimension_semantics=("arbitrary", "arbitrary"),
            vmem_limit_bytes=VMEM_LIMIT_BYTES),
        name="mlstm_prompt_attn_step",
    )(xb, ob, gates, gbias, cw, cb, wq, wk, mg, skip,
      bias, count, bias0, zt, attn_out_g.reshape(D_A, 1), cache_kt, cache_vt)


def _step_bias_rows(rel_bias, p):
    dist = p - jnp.arange(p)
    full = rel_bias[_t5_bucket(dist)].astype(f32).T
    count = sum(((dist % d == 0) & (dist <= w)).astype(f32) for (w, d) in DILATED_BRANCHES)
    zero = rel_bias[_t5_bucket(jnp.zeros((1,), jnp.int32))].astype(f32).T
    return jnp.where(count[None, :] > 0, full, NEG)[:, None, :], count[None, :], zero[:, None, :]


def _new_token_column(b, zt_ref):
    zt = zt_ref[...]
    is_b = lax.broadcasted_iota(jnp.int32, zt.shape, 1) == b
    return jnp.sum(jnp.where(is_b, zt, 0.0), axis=1, keepdims=True)


def _attn_step_math(b, col, bias_ref, cnt_ref, bias0_ref, g_ref, kt_ref, vt_ref, ot_ref):
    nbr = len(DILATED_BRANCHES)
    scale = 1.0 / math.sqrt(HEAD_DIM_A)
    cnt = cnt_ref[...]
    o_cols = []
    for h in range(N_HEADS_A):
        q_col = col[h * HEAD_DIM_A:(h + 1) * HEAD_DIM_A] * scale
        kn_col = col[D_A + h * HEAD_DIM_A:D_A + (h + 1) * HEAD_DIM_A]
        vn_col = col[2 * D_A + h * HEAD_DIM_A:2 * D_A + (h + 1) * HEAD_DIM_A]
        kt = kt_ref[0, h]
        vt = vt_ref[0, h]
        s = jnp.sum(kt * q_col, axis=0, keepdims=True)
        s_new = jnp.sum(kn_col * q_col, axis=0, keepdims=True)
        sb = s + bias_ref[h]
        s0 = s_new + bias0_ref[h]
        m = jnp.maximum(jnp.max(sb, axis=1, keepdims=True), s0)
        pw = cnt * jnp.exp(sb - m)
        e0 = nbr * jnp.exp(s0 - m)
        l = jnp.sum(pw, axis=1, keepdims=True) + e0
        o_cols.append((jnp.sum(vt * pw, axis=1, keepdims=True) + e0 * vn_col) / l)
    o = jnp.concatenate(o_cols, axis=0)
    ssq = jnp.sum(o * o, axis=0, keepdims=True)
    o = o * lax.rsqrt(ssq * (1.0 / D_A) + EPS) * g_ref[...]
    sel = lax.broadcasted_iota(jnp.int32, ot_ref.shape, 1) == b
    ot_ref[...] = jnp.where(sel, o, ot_ref[...])


def _window_roll(col, kt_ref, vt_ref, nk_ref, nv_ref):
    p = kt_ref.shape[-1]
    is_last = lax.broadcasted_iota(jnp.int32, (HEAD_DIM_A, p), 1) == p - 1
    for h in range(N_HEADS_A):
        kn_col = col[D_A + h * HEAD_DIM_A:D_A + (h + 1) * HEAD_DIM_A]
        vn_col = col[2 * D_A + h * HEAD_DIM_A:2 * D_A + (h + 1) * HEAD_DIM_A]
        nk_ref[0, h] = jnp.where(is_last, kn_col, pltpu.roll(kt_ref[0, h], p - 1, axis=1))
        nv_ref[0, h] = jnp.where(is_last, vn_col, pltpu.roll(vt_ref[0, h], p - 1, axis=1))


N_ROWS = 3
ROW_W, ROW_A, ROW_R = range(N_ROWS)


def _mlstm_step_pre_kernel(xb_ref, gt_ref, gbias_ref, sc_ref, cw_ref, cb_ref, wq_ref, wk_ref, nt_ref, mt_ref,
                           cact_out, qt_out, kt_out, vt_out, vst_out, rows_out, nt_out, mt_out):
    E = HEAD_DIM_B
    x = xb_ref[...]
    c = cb_ref[...] + x * cw_ref[CONV_W - 1:CONV_W, :]
    for t in range(CONV_W - 1):
        c = c + sc_ref[t] * cw_ref[t:t + 1, :]
    c_act = c * jax.nn.sigmoid(c)
    cact_out[...] = c_act
    gts_t = (gt_ref[...] + gbias_ref[...]).T
    logf_t = _log_sigmoid(gts_t)
    for h in range(N_HEADS_B):
        sl = slice(h * HEAD_PAD, (h + 1) * HEAD_PAD)
        ch_bf = c_act[:, sl].astype(bf16)
        q = jnp.dot(ch_bf, wq_ref[h], preferred_element_type=f32)
        k = jnp.dot(ch_bf, wk_ref[h], preferred_element_type=f32) * (1.0 / math.sqrt(HEAD_DIM_B))
        qt = q.T[:E]
        kt = k.T[:E]
        vt = x[:, sl].T[:E]
        i_pre = gts_t[h:h + 1]
        a = logf_t[N_HEADS_B + h:N_HEADS_B + h + 1]
        m_old = mt_ref[h:h + 1]
        m_t = jnp.maximum(a + m_old, i_pre)
        w_state = jnp.exp(a + m_old - m_t)
        w_in = jnp.exp(i_pre - m_t)
        amat = jnp.sum(qt * kt, axis=0, keepdims=True) * w_in
        n_old = nt_ref[h]
        den = w_state * jnp.sum(n_old * qt, axis=0, keepdims=True) + amat
        nt_out[h] = w_state * n_old + w_in * kt
        mt_out[h:h + 1] = m_t
        qt_out[h] = qt
        kt_out[h] = kt
        vt_out[h] = vt
        vst_out[h] = w_in * vt
        rows_out[h, ROW_W:ROW_W + 1] = w_state
        rows_out[h, ROW_A:ROW_A + 1] = amat
        rows_out[h, ROW_R:ROW_R + 1] = 1.0 / jnp.maximum(jnp.abs(den), jnp.exp(-m_t))


def _mlstm_step_pre(xb, gts, gbias, sc_t, cw, cb, wq, wk, nt, mt):
    nb = xb.shape[0]
    hd = jax.ShapeDtypeStruct((N_HEADS_B, HEAD_DIM_B, nb), f32)
    return pl.pallas_call(
        _mlstm_step_pre_kernel,
        out_shape=(jax.ShapeDtypeStruct((nb, D_B_PAD), f32), hd, hd, hd, hd,
                   jax.ShapeDtypeStruct((N_HEADS_B, N_ROWS, nb), f32), hd,
                   jax.ShapeDtypeStruct((N_HEADS_B, nb), f32)),
        compiler_params=pltpu.CompilerParams(vmem_limit_bytes=VMEM_LIMIT_BYTES),
        name="mlstm_step_pre",
    )(xb, gts, gbias, sc_t, cw, cb, wq, wk, nt, mt)


ST_VT = 40


def _mlstm_step_state_kernel(c_ref, qt_ref, kt_ref, vt_ref, vst_ref, rows_ref, ob_ref, cact_ref, mg_ref,
                             skip_ref, c_out, out_ref, cq_scr):
    vb = pl.program_id(1)
    E = HEAD_DIM_B
    qt = qt_ref[0]
    kt = kt_ref[0]
    w_state = rows_ref[0, ROW_W:ROW_W + 1]

    def per_row(v, carry):
        c_old = c_ref[0, v]
        vs = vst_ref[0, pl.ds(vb * ST_VT + v, 1), :]
        c_out[0, v] = w_state * c_old + vs * kt
        cq_scr[pl.ds(vb * ST_VT + v, 1), :] = jnp.sum(c_old * qt, axis=0, keepdims=True)
        return carry

    lax.fori_loop(0, ST_VT, per_row, 0)

    @pl.when(vb == pl.num_programs(1) - 1)
    def _finish_head():
        ht = (w_state * cq_scr[...] + rows_ref[0, ROW_A:ROW_A + 1] * vt_ref[0]) * rows_ref[0, ROW_R:ROW_R + 1]
        hh = jnp.concatenate([ht, jnp.zeros((HEAD_PAD - E, ht.shape[1]), f32)], axis=0).T
        hn = hh * lax.rsqrt(jnp.sum(hh * hh, axis=-1, keepdims=True) * (1.0 / E) + EPS) * mg_ref[0]
        out_ref[0] = jax.nn.sigmoid(ob_ref[0]) * (hn + skip_ref[0] * cact_ref[0])


def _mlstm_step_state(ct, qt, kt, vt, vst, rows, ob3, cact3, mg3, skip3):
    nh, e, _, nb = ct.shape
    assert e % ST_VT == 0
    head = lambda shape: pl.BlockSpec((1,) + shape, lambda h, v: (h,) + (0,) * len(shape))
    cblk = pl.BlockSpec((1, ST_VT, e, nb), lambda h, v: (h, v, 0, 0))
    return pl.pallas_call(
        _mlstm_step_state_kernel,
        out_shape=(jax.ShapeDtypeStruct(ct.shape, f32), jax.ShapeDtypeStruct((nh, nb, HEAD_PAD), f32)),
        grid=(nh, e // ST_VT),
        in_specs=[cblk, head((e, nb)), head((e, nb)), head((e, nb)), head((e, nb)), head((N_ROWS, nb)),
                  head((nb, HEAD_PAD)), head((nb, HEAD_PAD)), head((1, HEAD_PAD)), head((1, HEAD_PAD))],
        out_specs=(cblk, head((nb, HEAD_PAD))),
        scratch_shapes=[pltpu.VMEM((e, nb), f32)],
        compiler_params=pltpu.CompilerParams(
            dimension_semantics=("arbitrary", "arbitrary"),
            vmem_limit_bytes=VMEM_LIMIT_BYTES),
        name="mlstm_step_state",
    )(ct, qt, kt, vt, vst, rows, ob3, cact3, mg3, skip3)


def kernel(x_prompt, x_sample, cache_win_k, cache_win_v, state_conv, state_C, state_n, state_m,
           rel_bias, norm1_g, w_in, gate_bias, conv_w, conv_b, wq_head, wk_head, attn_out_g,
           mh_norm_g, skip, w_out, norm2_g, w_ff1, w_ff2, final_g):
    Bp, Sp, _ = x_prompt.shape
    Bs, Ss, _ = x_sample.shape
    assert Ss == 1
    g1 = norm1_g[0].reshape(1, D_MODEL)
    g2 = norm2_g[0].reshape(1, D_MODEL)
    gf = final_g.reshape(1, D_MODEL)
    w_in_pad = _pad_w_in(w_in[0]).astype(bf16)
    wo = jnp.concatenate([w_out[0][:D_A], jnp.zeros((LANES, D_MODEL), f32),
                          _pad_heads(w_out[0][D_A:].T).T], axis=0).astype(bf16)
    w1 = w_ff1[0].astype(bf16)
    w2 = w_ff2[0].astype(bf16)
    hpad = HEAD_PAD - HEAD_DIM_B
    wq_p = jnp.pad(wq_head[0], ((0, 0), (0, hpad), (0, hpad))).astype(bf16)
    wk_p = jnp.pad(wk_head[0], ((0, 0), (0, hpad), (0, hpad))).astype(bf16)
    gbias = jnp.pad(gate_bias[0], (0, LANES - N_GATES)).reshape(1, LANES)
    cw_p = _pad_heads(conv_w[0])
    cb_p = _pad_heads(conv_b[0]).reshape(1, D_B_PAD)
    mg_p = _pad_heads(mh_norm_g[0]).reshape(1, D_B_PAD)
    skip_p = _pad_heads(skip[0]).reshape(1, D_B_PAD)

    xp2 = x_prompt.reshape(Bp * Sp, D_MODEL)
    P = min(WINDOW_MAX, Sp)
    slabs, k_win, v_win, xb, ob, gts = _norm_inproj(xp2, g1, w_in_pad, tm=512, seq=Sp, win=P)
    out_a = _attn_prompt(slabs, rel_bias, attn_out_g[0], Bp, Sp)
    xs2 = x_sample.reshape(Bs, D_MODEL)
    zt_s, xb_s, ob_s, gts_s = _norm_inproj_step(xs2, g1, w_in[0][:, :3 * D_A].T.astype(bf16), w_in_pad)
    out_b, c_p, n_p, m_p, tail_p, oat_s, nkt, nvt = _mlstm_prompt_attn_step(
        xb, ob, gts, gbias, cw_p, cb_p, wq_p, wk_p, mg_p, skip_p, Bp, Sp,
        zt_s, cache_win_k[0].transpose(0, 2, 3, 1), cache_win_v[0].transpose(0, 2, 3, 1), rel_bias,
        attn_out_g[0])
    y_p = _out_ffn(xp2, out_a, out_b, wo, g2, w1, w2, gf, tm=512).reshape(Bp, Sp, D_MODEL)
    win5 = lambda t: t.reshape(N_PAIRS, Bp, P, LANES).transpose(1, 2, 0, 3).reshape(
        1, Bp, P, N_HEADS_A, HEAD_DIM_A)
    st_p = (win5(k_win), win5(v_win), _unpad_heads(tail_p[:, SUBLANES - (CONV_W - 1):])[None],
            c_p[:, :, :HEAD_DIM_B, :HEAD_DIM_B][None], n_p[:, :, 0, :HEAD_DIM_B][None],
            m_p[:, :, 0, 0][None])

    oa_s = oat_s.reshape(N_PAIRS, LANES, Bs).transpose(0, 2, 1)
    new_k = nkt.transpose(0, 3, 1, 2)[None]
    new_v = nvt.transpose(0, 3, 1, 2)[None]
    sc_t = _pad_heads(state_conv[0].transpose(1, 0, 2))
    cact_s, qt_s, kt_s, vt_s, vst_s, rows_s, nt_s, mt_s = _mlstm_step_pre(
        xb_s, gts_s, gbias, sc_t, cw_p, cb_p, wq_p, wk_p,
        state_n[0].transpose(1, 2, 0), state_m[0].T)
    heads3 = lambda t: t.reshape(t.shape[0], N_HEADS_B, HEAD_PAD).transpose(1, 0, 2)
    ct_s, outb3 = _mlstm_step_state(state_C[0].transpose(1, 2, 3, 0), qt_s, kt_s, vt_s, vst_s, rows_s,
                                    heads3(ob_s), heads3(cact_s), heads3(mg_p), heads3(skip_p))
    outb_s = outb3.transpose(1, 0, 2).reshape(Bs, D_B_PAD)
    y_s = _out_ffn(xs2, oa_s, outb_s, wo, g2, w1, w2, gf, tm=Bs).reshape(Bs, Ss, D_MODEL)
    new_conv = jnp.concatenate([state_conv[0][:, 1:], _unpad_heads(xb_s)[:, None]], axis=1)
    st_s = (new_k, new_v, new_conv[None], ct_s.transpose(3, 0, 1, 2)[None],
            nt_s.transpose(2, 0, 1)[None], mt_s.T[None])
    return (y_p, y_s) + st_p + st_s
```

```python
import functools
import math

import jax
import jax.numpy as jnp
from jax import lax
from jax.experimental import pallas as pl
from jax.experimental.pallas import tpu as pltpu

D_MODEL = 1024
HEAD_DIM_A = 64
N_HEADS_A = 6
D_A = N_HEADS_A * HEAD_DIM_A
D_B = D_MODEL - D_A
N_HEADS_B = 4
HEAD_DIM_B = D_B // N_HEADS_B
DILATED_BRANCHES = ((128, 1), (512, 4), (2048, 16))
WINDOW_MAX = 2048
N_BUCKETS = 32
REL_MAX_DIST = 2048
CONV_W = 4
MLSTM_CHUNK = 128
D_FF = 4 * D_MODEL
N_GATES = 2 * N_HEADS_B
D_IN = 3 * D_A + 2 * D_B + N_GATES
SPLITS = [D_A, 2 * D_A, 3 * D_A, 3 * D_A + D_B, 3 * D_A + 2 * D_B]
EPS = 1e-6
NEG = -1e30

LANES = 128
SUBLANES = 8
VMEM_LIMIT_BYTES = 56 * 1024 * 1024

N_PAIRS = D_A // LANES
N_SLABS = 3 * N_PAIRS
HEAD_PAD = 2 * LANES
D_B_PAD = N_HEADS_B * HEAD_PAD
D_IN_PAD = 3 * D_A + 2 * D_B_PAD + LANES
D_O_PAD = D_A + LANES + D_B_PAD
BLK = 128
RES = DILATED_BRANCHES[-1][1]
Q_SUPER = BLK * RES

f32 = jnp.float32
bf16 = jnp.bfloat16


def _round_up(n, m):
    return -(-n // m) * m


def _rms(xf, g):
    return xf * lax.rsqrt(jnp.mean(xf * xf, axis=-1, keepdims=True) + EPS) * g


def _pad_heads(t):
    t = t.reshape(t.shape[:-1] + (N_HEADS_B, HEAD_DIM_B))
    t = jnp.pad(t, [(0, 0)] * (t.ndim - 1) + [(0, HEAD_PAD - HEAD_DIM_B)])
    return t.reshape(t.shape[:-2] + (D_B_PAD,))


def _unpad_heads(t):
    t = t.reshape(t.shape[:-1] + (N_HEADS_B, HEAD_PAD))[..., :HEAD_DIM_B]
    return t.reshape(t.shape[:-2] + (D_B,))


def _norm_inproj_kernel(x_ref, g_ref, w_ref, slab_ref, kwin_ref, vwin_ref, xb_ref, ob_ref, gt_ref):
    h = _rms(x_ref[...], g_ref[...]).astype(bf16)
    qkv = jnp.dot(h, w_ref[:, 0:3 * D_A], preferred_element_type=f32)
    for j in range(N_SLABS):
        slab_ref[j] = qkv[:, j * LANES:(j + 1) * LANES]
    for p in range(N_PAIRS):
        kwin_ref[p] = qkv[:, (N_PAIRS + p) * LANES:(N_PAIRS + p + 1) * LANES]
        vwin_ref[p] = qkv[:, (2 * N_PAIRS + p) * LANES:(2 * N_PAIRS + p + 1) * LANES]
    o = 3 * D_A
    xb_ref[...] = jnp.dot(h, w_ref[:, o:o + D_B_PAD], preferred_element_type=f32)
    o += D_B_PAD
    ob_ref[...] = jnp.dot(h, w_ref[:, o:o + D_B_PAD], preferred_element_type=f32)
    o += D_B_PAD
    gt_ref[...] = jnp.dot(h, w_ref[:, o:o + LANES], preferred_element_type=f32)


def _norm_inproj(x, g, w_pad_bf16, tm, seq, win):
    m, d = x.shape
    assert m % seq == 0 and seq % tm == 0 and win % tm == 0
    tiles_seq, tiles_win = seq // tm, win // tm
    row = lambda w: pl.BlockSpec((tm, w), lambda i: (i, 0))

    def win_block(i):
        return (0, (i // tiles_seq) * tiles_win + jnp.maximum(i % tiles_seq - (tiles_seq - tiles_win), 0), 0)

    return pl.pallas_call(
        _norm_inproj_kernel,
        out_shape=(
            jax.ShapeDtypeStruct((N_SLABS, m, LANES), f32),
            jax.ShapeDtypeStruct((N_PAIRS, (m // seq) * win, LANES), f32),
            jax.ShapeDtypeStruct((N_PAIRS, (m // seq) * win, LANES), f32),
            jax.ShapeDtypeStruct((m, D_B_PAD), f32),
            jax.ShapeDtypeStruct((m, D_B_PAD), f32),
            jax.ShapeDtypeStruct((m, LANES), f32),
        ),
        grid=(m // tm,),
        in_specs=[
            row(d),
            pl.BlockSpec((1, d), lambda i: (0, 0)),
            pl.BlockSpec((d, D_IN_PAD), lambda i: (0, 0), pipeline_mode=pl.Buffered(1)),
        ],
        out_specs=(
            pl.BlockSpec((N_SLABS, tm, LANES), lambda i: (0, i, 0)),
            pl.BlockSpec((N_PAIRS, tm, LANES), win_block),
            pl.BlockSpec((N_PAIRS, tm, LANES), win_block),
            row(D_B_PAD), row(D_B_PAD), row(LANES),
        ),
        compiler_params=pltpu.CompilerParams(
            dimension_semantics=("arbitrary",),
            vmem_limit_bytes=VMEM_LIMIT_BYTES),
        name="norm_inproj",
    )(x, g, w_pad_bf16)


def _norm_inproj_step_kernel(x_ref, g_ref, wt_ref, w_ref, zt_ref, xb_ref, ob_ref, gt_ref):
    h = _rms(x_ref[...], g_ref[...]).astype(bf16)
    zt_ref[...] = lax.dot_general(wt_ref[...], h, (((1,), (1,)), ((), ())), preferred_element_type=f32)
    o = 3 * D_A
    xb_ref[...] = jnp.dot(h, w_ref[:, o:o + D_B_PAD], preferred_element_type=f32)
    o += D_B_PAD
    ob_ref[...] = jnp.dot(h, w_ref[:, o:o + D_B_PAD], preferred_element_type=f32)
    o += D_B_PAD
    gt_ref[...] = jnp.dot(h, w_ref[:, o:o + LANES], preferred_element_type=f32)


def _norm_inproj_step(x, g, wqkv_t_bf16, w_pad_bf16):
    nb = x.shape[0]
    wide = jax.ShapeDtypeStruct((nb, D_B_PAD), f32)
    return pl.pallas_call(
        _norm_inproj_step_kernel,
        out_shape=(jax.ShapeDtypeStruct((3 * D_A, nb), f32), wide, wide,
                   jax.ShapeDtypeStruct((nb, LANES), f32)),
        compiler_params=pltpu.CompilerParams(vmem_limit_bytes=VMEM_LIMIT_BYTES),
        name="norm_inproj_step",
    )(x, g, wqkv_t_bf16, w_pad_bf16)


def _pad_w_in(w_in):
    qkv = w_in[:, :3 * D_A]
    xb = _pad_heads(w_in[:, SPLITS[2]:SPLITS[3]])
    ob = _pad_heads(w_in[:, SPLITS[3]:SPLITS[4]])
    gt = jnp.pad(w_in[:, SPLITS[4]:], ((0, 0), (0, LANES - N_GATES)))
    return jnp.concatenate([qkv, xb, ob, gt], axis=1)


def _out_ffn_kernel(x_ref, oa_ref, ob_ref, wo_ref, g2_ref, w1_ref, w2_ref, gf_ref, y_ref, *, ff_chunk):
    o = jnp.concatenate([oa_ref[p].astype(bf16) for p in range(N_PAIRS)]
                        + [jnp.zeros((ob_ref.shape[0], LANES), bf16), ob_ref[...].astype(bf16)], axis=1)
    x1 = x_ref[...] + jnp.dot(o, wo_ref[...], preferred_element_type=f32)
    h2 = _rms(x1, g2_ref[...]).astype(bf16)
    acc = x1
    for c in range(D_FF // ff_chunk):
        u = jnp.dot(h2, w1_ref[:, c * ff_chunk:(c + 1) * ff_chunk], preferred_element_type=f32)
        u = jnp.square(jnp.maximum(u, 0.0)).astype(bf16)
        acc = acc + jnp.dot(u, w2_ref[c * ff_chunk:(c + 1) * ff_chunk, :],
                            preferred_element_type=f32)
    y_ref[...] = _rms(acc, gf_ref[...])


def _out_ffn(x, oa, ob, wo, g2, w1, w2, gf, tm, ff_chunk=1024):
    m, d = x.shape
    assert wo.shape[0] == D_O_PAD
    const = lambda i: (0, 0)
    single = dict(pipeline_mode=pl.Buffered(1))
    row = lambda w: pl.BlockSpec((tm, w), lambda i: (i, 0))
    return pl.pallas_call(
        functools.partial(_out_ffn_kernel, ff_chunk=ff_chunk),
        out_shape=jax.ShapeDtypeStruct((m, d), f32),
        grid=(m // tm,),
        in_specs=[
            row(d), pl.BlockSpec((N_PAIRS, tm, LANES), lambda i: (0, i, 0)), row(D_B_PAD),
            pl.BlockSpec((D_O_PAD, d), const, **single),
            pl.BlockSpec((1, d), const),
            pl.BlockSpec((d, D_FF), const, **single),
            pl.BlockSpec((D_FF, d), const, **single),
            pl.BlockSpec((1, d), const),
        ],
        out_specs=row(d),
        compiler_params=pltpu.CompilerParams(
            dimension_semantics=("arbitrary",),
            vmem_limit_bytes=VMEM_LIMIT_BYTES),
        name="outproj_ffn",
    )(x, oa, ob, wo, g2, w1, w2, gf)


def _t5_bucket(dist):
    max_exact = N_BUCKETS // 2
    df = jnp.maximum(dist, 1).astype(jnp.float32)
    large = max_exact + (jnp.log(df / max_exact) / math.log(REL_MAX_DIST / max_exact)
                         * (N_BUCKETS - max_exact)).astype(jnp.int32)
    large = jnp.minimum(large, N_BUCKETS - 1)
    return jnp.where(dist < max_exact, dist, large)


def _prompt_bias_table(rel_bias):
    period = 3 * BLK + 1
    r = jnp.arange(period)
    t = jnp.where(r < 2 * BLK, r, r - period)
    j = BLK - t
    band = (j >= 0) & (j <= BLK)
    tabs = []
    for (_, d) in DILATED_BRANCHES:
        vals = rel_bias[_t5_bucket(jnp.clip(j, 0, None) * d)].astype(f32)
        w = jnp.where(band[:, None], vals, NEG).T
        flat = jnp.tile(w, (1, BLK))[:, :BLK * (period - 1)]
        toe = flat.reshape(N_HEADS_A, BLK, period - 1)[:, :, :2 * BLK]
        sub = RES // d
        if sub > 1:
            pos = jnp.arange(BLK)
            nat = (pos % (BLK // sub)) * sub + pos // (BLK // sub)
            pq = jax.nn.one_hot(nat, BLK, dtype=f32)
            pk = jnp.kron(jnp.eye(2, dtype=f32), pq)
            toe = jnp.einsum('pq,hqk->hpk', pq, toe, precision=lax.Precision.HIGHEST)
            toe = jnp.einsum('hpk,jk->hpj', toe, pk, precision=lax.Precision.HIGHEST)
        tabs.append(toe.reshape(N_PAIRS, 2 * BLK, 2 * BLK))
    return jnp.stack(tabs)


def _attn_prompt_kernel(bias_ref, q_ref, k_ref, v_ref, g_ref, o_ref, q_scr, kv_scr, m_scr, l_scr, acc_scr):
    n = pl.program_id(1)
    step = pl.program_id(0) * pl.num_programs(1) + n
    cur = lax.rem(step, 2)
    prv = 1 - cur

    @pl.when(step == 0)
    def _init():
        kv_scr[1] = jnp.zeros(kv_scr.shape[1:], f32)

    def deinterleave(r, carry):
        rows = pl.ds(pl.multiple_of(r * BLK, BLK), BLK)
        for p in range(N_PAIRS):
            q_scr[p, rows, :] = q_ref.at[p][pl.ds(r, BLK, stride=RES), :]
            kv_scr[cur, p, rows, :] = k_ref.at[p][pl.ds(r, BLK, stride=RES), :]
            kv_scr[cur, N_PAIRS + p, rows, :] = v_ref.at[p][pl.ds(r, BLK, stride=RES), :]
        return carry

    lax.fori_loop(0, RES, deinterleave, 0, unroll=4)

    lane = lax.broadcasted_iota(jnp.int32, (BLK, LANES), 1)
    low = lane < HEAD_DIM_A
    kcol = lax.broadcasted_iota(jnp.int32, (2 * BLK, 2 * BLK), 1)
    first_extra = jnp.where((kcol < BLK) & (n == 0), NEG, 0.0).astype(f32)

    def job(g, d, buf_a, off_a, off_b, res, extra):
        sub = RES // d
        run = BLK // sub

        def starts(off):
            return [pl.multiple_of((c * d + res) * BLK + off, SUBLANES) for c in range(sub)]

        def gather(ref, lead, off):
            return jnp.concatenate([ref[lead + (pl.ds(s, run), slice(None))] for s in starts(off)], axis=0)

        def scatter(ref, p, val):
            for c, s in enumerate(starts(off_b)):
                ref[p, pl.ds(s, run), :] = val[c * run:(c + 1) * run]

        for p in range(N_PAIRS):
            q2 = gather(q_scr, (p,), off_b) * (1.0 / math.sqrt(HEAD_DIM_A))
            qcat = jnp.concatenate([jnp.where(low, q2, 0.0), jnp.where(low, 0.0, q2)], axis=0)
            kk = jnp.concatenate([gather(kv_scr, (buf_a, p), off_a), gather(kv_scr, (cur, p), off_b)], axis=0)
            vv = jnp.concatenate([gather(kv_scr, (buf_a, N_PAIRS + p), off_a),
                                  gather(kv_scr, (cur, N_PAIRS + p), off_b)], axis=0)
            s = lax.dot_general(qcat.astype(bf16), kk.astype(bf16), (((1,), (1,)), ((), ())),
                                preferred_element_type=f32)
            s = s + bias_ref[g, p]
            if extra is not None:
                s = s + extra
            m = jnp.max(s, axis=-1, keepdims=True)
            e = jnp.exp(s - m)
            l = jnp.sum(e, axis=-1, keepdims=True)
            o = jnp.dot(e.astype(bf16), vv.astype(bf16), preferred_element_type=f32)
            o_t = jnp.where(low, o[:BLK], o[BLK:])
            m_t = jnp.where(low, m[:BLK], m[BLK:])
            l_t = jnp.where(low, l[:BLK], l[BLK:])
            if g == 0:
                scatter(acc_scr, p, o_t)
                scatter(m_scr, p, m_t)
                scatter(l_scr, p, l_t)
            else:
                m_o = gather(m_scr, (p,), off_b)
                m_n = jnp.maximum(m_o, m_t)
                al = jnp.exp(m_o - m_n)
                be = jnp.exp(m_t - m_n)
                scatter(acc_scr, p, al * gather(acc_scr, (p,), off_b) + be * o_t)
                scatter(l_scr, p, al * gather(l_scr, (p,), off_b) + be * l_t)
                scatter(m_scr, p, m_n)

    for g, (_, d) in enumerate(DILATED_BRANCHES):
        nu = Q_SUPER // (BLK * d)
        run = BLK * d // RES
        log_d = d.bit_length() - 1

        def first(res, carry, g=g, d=d, nu=nu, run=run):
            job(g, d, prv, (nu - 1) * run, 0, res, first_extra)
            return carry

        def rest(idx, carry, g=g, d=d, run=run, log_d=log_d):
            u = 1 + lax.shift_right_logical(idx, log_d)
            res = jnp.bitwise_and(idx, d - 1)
            job(g, d, cur, (u - 1) * run, u * run, res, None)
            return carry

        if d == 1:
            first(0, 0)
        else:
            lax.fori_loop(0, d, first, 0, unroll=4)
        if nu > 1:
            lax.fori_loop(0, (nu - 1) * d, rest, 0, unroll=5 if d == 1 else 4)

    def fin(r, carry):
        rows = pl.ds(pl.multiple_of(r * BLK, BLK), BLK)
        os_ = [acc_scr[p, rows, :] / l_scr[p, rows, :] for p in range(N_PAIRS)]
        ss = sum(jnp.sum(o * o, axis=-1, keepdims=True) for o in os_)
        sc = lax.rsqrt(ss * (1.0 / D_A) + EPS)
        for p in range(N_PAIRS):
            o_ref.at[p][pl.ds(r, BLK, stride=RES), :] = os_[p] * sc * g_ref[:, p * LANES:(p + 1) * LANES]
        return carry

    lax.fori_loop(0, RES, fin, 0, unroll=4)


def _attn_prompt(slabs, rel_bias, attn_out_g, batch, seq):
    assert seq % Q_SUPER == 0
    nsb = seq // Q_SUPER
    nbr = len(DILATED_BRANCHES)
    slab = lambda grp: pl.BlockSpec((N_PAIRS, Q_SUPER, LANES), lambda bi, ni: (grp, bi * nsb + ni, 0))
    return pl.pallas_call(
        _attn_prompt_kernel,
        out_shape=jax.ShapeDtypeStruct((N_PAIRS, batch * seq, LANES), f32),
        grid=(batch, nsb),
        in_specs=[
            pl.BlockSpec((nbr, N_PAIRS, 2 * BLK, 2 * BLK), lambda bi, ni: (0, 0, 0, 0),
                         pipeline_mode=pl.Buffered(1)),
            slab(0), slab(1), slab(2),
            pl.BlockSpec((1, D_A), lambda bi, ni: (0, 0)),
        ],
        out_specs=pl.BlockSpec((N_PAIRS, Q_SUPER, LANES), lambda bi, ni: (0, bi * nsb + ni, 0)),
        scratch_shapes=[
            pltpu.VMEM((N_PAIRS, Q_SUPER, LANES), f32),
            pltpu.VMEM((2, 2 * N_PAIRS, Q_SUPER, LANES), f32),
            pltpu.VMEM((N_PAIRS, Q_SUPER, LANES), f32),
            pltpu.VMEM((N_PAIRS, Q_SUPER, LANES), f32),
            pltpu.VMEM((N_PAIRS, Q_SUPER, LANES), f32),
        ],
        compiler_params=pltpu.CompilerParams(
            dimension_semantics=("arbitrary", "arbitrary"),
            vmem_limit_bytes=VMEM_LIMIT_BYTES),
        name="attn_prompt",
    )(_prompt_bias_table(rel_bias), slabs, slabs, slabs, attn_out_g.reshape(1, D_A))


def _log_sigmoid(x):
    return -(jnp.maximum(-x, 0.0) + jnp.log1p(jnp.exp(-jnp.abs(x))))


def _mlstm_prompt_kernel(xb_ref, ob_ref, gt_ref, gbias_ref, cw_ref, cb_ref, wq_ref, wk_ref, mg_ref,
                         skip_ref, bias_ref, cnt_ref, bias0_ref, zt_ref, ga_ref, kt_ref, vt_ref,
                         out_ref, c_out_ref, n_out_ref, m_out_ref, conv_out_ref, ot_ref, nk_ref, nv_ref,
                         conv_scr, c_scr, n_scr, m_scr):
    c_idx = pl.program_id(1)
    L = MLSTM_CHUNK

    seq_idx = pl.program_id(0) * pl.num_programs(1) + c_idx

    @pl.when(seq_idx == 0)
    def _init_decode():
        ot_ref[...] = jnp.zeros_like(ot_ref)

    new_col = _new_token_column(seq_idx, zt_ref)
    _attn_step(seq_idx, new_col, bias_ref, cnt_ref, bias0_ref, ga_ref, kt_ref, vt_ref, ot_ref, nk_ref, nv_ref)

    @pl.when(c_idx == 0)
    def _init():
        conv_scr[0:SUBLANES, :] = jnp.zeros((SUBLANES, D_B_PAD), f32)
        c_scr[...] = jnp.zeros_like(c_scr)
        n_scr[...] = jnp.zeros_like(n_scr)
        m_scr[...] = jnp.zeros_like(m_scr)

    x = xb_ref[...]
    conv_scr[SUBLANES:SUBLANES + L, :] = x
    c = cb_ref[...] + x * cw_ref[CONV_W - 1:CONV_W, :]
    for i in range(CONV_W - 1):
        sh = CONV_W - 1 - i
        c = c + conv_scr[SUBLANES - sh:SUBLANES - sh + L, :] * cw_ref[i:i + 1, :]
    conv_scr[0:SUBLANES, :] = x[L - SUBLANES:, :]
    c_act = c * jax.nn.sigmoid(c)

    gts = gt_ref[...] + gbias_ref[...]
    logf = _log_sigmoid(gts)
    row = lax.broadcasted_iota(jnp.int32, (L, L), 0)
    col = lax.broadcasted_iota(jnp.int32, (L, L), 1)
    causal = row >= col
    a_all = jnp.dot(causal.astype(f32), logf, precision=lax.Precision.HIGHEST,
                    preferred_element_type=f32)
    gts_t = gts.T
    a_t = a_all.T

    for h in range(N_HEADS_B):
        sl = slice(h * HEAD_PAD, (h + 1) * HEAD_PAD)
        ch = c_act[:, sl]
        ch_bf = ch.astype(bf16)
        q = jnp.dot(ch_bf, wq_ref[h], preferred_element_type=f32)
        k = jnp.dot(ch_bf, wk_ref[h], preferred_element_type=f32) * (1.0 / math.sqrt(HEAD_DIM_B))
        v = x[:, sl]
        q_bf, k_bf = q.astype(bf16), k.astype(bf16)
        a_col = a_all[:, N_HEADS_B + h:N_HEADS_B + h + 1]
        i_col = gts[:, h:h + 1]
        a_row = a_t[N_HEADS_B + h:N_HEADS_B + h + 1, :]
        i_row = gts_t[h:h + 1, :]
        m_prev = m_scr[h]
        dmat = jnp.where(causal, a_col - a_row + i_row, NEG)
        g = a_col + m_prev
        m_t = jnp.maximum(g, jnp.max(dmat, axis=-1, keepdims=True))
        w_state = jnp.exp(g - m_t)
        s = lax.dot_general(q_bf, k_bf, (((1,), (1,)), ((), ())), preferred_element_type=f32)
        amat = s * jnp.exp(dmat - m_t)
        c_state = c_scr[h]
        inter = lax.dot_general(q_bf, c_state.astype(bf16), (((1,), (1,)), ((), ())),
                                preferred_element_type=f32)
        num = w_state * inter + jnp.dot(amat.astype(bf16), v.astype(bf16),
                                        preferred_element_type=f32)
        n_row = n_scr[h]
        den = (w_state * jnp.sum(q * n_row, axis=-1, keepdims=True)
               + jnp.sum(amat, axis=-1, keepdims=True))
        hh = num / jnp.maximum(jnp.abs(den), jnp.exp(-m_t))
        hn = hh * lax.rsqrt(jnp.sum(hh * hh, axis=-1, keepdims=True) * (1.0 / HEAD_DIM_B) + EPS)
        hn = hn * mg_ref[:, sl]
        out_ref[:, sl] = jax.nn.sigmoid(ob_ref[:, sl]) * (hn + skip_ref[:, sl] * ch)

        b_tot = a_col[L - 1:L, :]
        wl = b_tot - a_col + i_col
        m_new = jnp.maximum(b_tot + m_prev, jnp.max(wl, axis=0, keepdims=True))
        wk = jnp.exp(wl - m_new)
        decay = jnp.exp(b_tot + m_prev - m_new)
        upd = lax.dot_general((wk * v).astype(bf16), k_bf, (((0,), (0,)), ((), ())),
                              preferred_element_type=f32)
        c_scr[h] = decay * c_state + upd
        n_scr[h] = decay * n_row + jnp.sum(wk * k, axis=0, keepdims=True)
        m_scr[h] = m_new

    @pl.when(c_idx == pl.num_programs(1) - 1)
    def _final():
        c_out_ref[0] = c_scr[...]
        n_out_ref[0] = n_scr[...]
        m_out_ref[0] = m_scr[...]
        conv_out_ref[0] = x[L - SUBLANES:, :]


def _mlstm_prompt_attn_step(xb, ob, gates, gbias, cw, cb, wq, wk, mg, skip, batch, seq,
                            zt, cache_kt, cache_vt, rel_bias, attn_out_g):
    L = MLSTM_CHUNK
    assert seq % L == 0
    nc = seq // L
    nb, p = cache_kt.shape[0], cache_kt.shape[-1]
    assert nb == batch * nc, "one decode sequence per (prompt sequence, chunk) grid step"
    for (w, d) in DILATED_BRANCHES:
        assert w <= p and w // d == BLK
    bias, count, bias0 = _step_bias_rows(rel_bias, p)
    win = pl.BlockSpec((1, N_HEADS_A, HEAD_DIM_A, p), lambda bi, ci: (bi * nc + ci, 0, 0, 0))
    buf = jax.ShapeDtypeStruct(cache_kt.shape, f32)
    rows = lambda w: pl.BlockSpec((L, w), lambda bi, ci: (bi * nc + ci, 0))
    const2 = lambda shape: pl.BlockSpec(shape, lambda bi, ci: (0, 0))
    const3 = lambda shape: pl.BlockSpec(shape, lambda bi, ci: (0, 0, 0))
    state = lambda shape: pl.BlockSpec((1,) + shape, lambda bi, ci: (bi,) + (0,) * len(shape))
    return pl.pallas_call(
        _mlstm_prompt_kernel,
        out_shape=(
            jax.ShapeDtypeStruct((batch * seq, D_B_PAD), f32),
            jax.ShapeDtypeStruct((batch, N_HEADS_B, HEAD_PAD, HEAD_PAD), f32),
            jax.ShapeDtypeStruct((batch, N_HEADS_B, 1, HEAD_PAD), f32),
            jax.ShapeDtypeStruct((batch, N_HEADS_B, 1, 1), f32),
            jax.ShapeDtypeStruct((batch, SUBLANES, D_B_PAD), f32),
            jax.ShapeDtypeStruct((D_A, nb), f32), buf, buf,
        ),
        grid=(batch, nc),
        in_specs=[rows(D_B_PAD), rows(D_B_PAD), rows(LANES), const2((1, LANES)),
                  const2((CONV_W, D_B_PAD)), const2((1, D_B_PAD)),
                  const3((N_HEADS_B, HEAD_PAD, HEAD_PAD)), const3((N_HEADS_B, HEAD_PAD, HEAD_PAD)),
                  const2((1, D_B_PAD)), const2((1, D_B_PAD)),
                  const3((N_HEADS_A, 1, p)), const2((1, p)), const3((N_HEADS_A, 1, 1)),
                  const2((3 * D_A, nb)), const2((D_A, 1)), win, win],
        out_specs=(rows(D_B_PAD), state((N_HEADS_B, HEAD_PAD, HEAD_PAD)),
                   state((N_HEADS_B, 1, HEAD_PAD)), state((N_HEADS_B, 1, 1)),
                   state((SUBLANES, D_B_PAD)),
                   const2((D_A, nb)), win, win),
        scratch_shapes=[
            pltpu.VMEM((SUBLANES + L, D_B_PAD), f32),
            pltpu.VMEM((N_HEADS_B, HEAD_PAD, HEAD_PAD), f32),
            pltpu.VMEM((N_HEADS_B, 1, HEAD_PAD), f32),
            pltpu.VMEM((N_HEADS_B, 1, 1), f32),
        ],
        compiler_params=pltpu.CompilerParams(
            dimension_semantics=("arbitrary", "arbitrary"),
            vmem_limit_bytes=VMEM_LIMIT_BYTES),
        name="mlstm_prompt_attn_step",
    )(xb, ob, gates, gbias, cw, cb, wq, wk, mg, skip,
      bias, count, bias0, zt, attn_out_g.reshape(D_A, 1), cache_kt, cache_vt)


def _step_bias_rows(rel_bias, p):
    dist = p - jnp.arange(p)
    full = rel_bias[_t5_bucket(dist)].astype(f32).T
    count = sum(((dist % d == 0) & (dist <= w)).astype(f32) for (w, d) in DILATED_BRANCHES)
    zero = rel_bias[_t5_bucket(jnp.zeros((1,), jnp.int32))].astype(f32).T
    return jnp.where(count[None, :] > 0, full, NEG)[:, None, :], count[None, :], zero[:, None, :]


def _new_token_column(b, zt_ref):
    zt = zt_ref[...]
    is_b = lax.broadcasted_iota(jnp.int32, zt.shape, 1) == b
    return jnp.sum(jnp.where(is_b, zt, 0.0), axis=1, keepdims=True)


def _attn_step(b, col, bias_ref, cnt_ref, bias0_ref, g_ref, kt_ref, vt_ref, ot_ref, nk_ref, nv_ref):
    nbr = len(DILATED_BRANCHES)
    scale = 1.0 / math.sqrt(HEAD_DIM_A)
    cnt = cnt_ref[...]
    weights = []
    for h in range(N_HEADS_A):
        q_col = col[h * HEAD_DIM_A:(h + 1) * HEAD_DIM_A] * scale
        kn_col = col[D_A + h * HEAD_DIM_A:D_A + (h + 1) * HEAD_DIM_A]
        kt = kt_ref[0, h]
        s = jnp.sum(kt * q_col, axis=0, keepdims=True)
        s_new = jnp.sum(kn_col * q_col, axis=0, keepdims=True)
        sb = s + bias_ref[h]
        s0 = s_new + bias0_ref[h]
        m = jnp.maximum(jnp.max(sb, axis=1, keepdims=True), s0)
        pw = cnt * jnp.exp(sb - m)
        e0 = nbr * jnp.exp(s0 - m)
        weights.append((pw, e0, jnp.sum(pw, axis=1, keepdims=True) + e0))
    _window_roll(col, kt_ref, vt_ref, nk_ref, nv_ref)
    o_cols = []
    for h, (pw, e0, l) in enumerate(weights):
        vn_col = col[2 * D_A + h * HEAD_DIM_A:2 * D_A + (h + 1) * HEAD_DIM_A]
        vt = vt_ref[0, h]
        o_cols.append((jnp.sum(vt * pw, axis=1, keepdims=True) + e0 * vn_col) / l)
    o = jnp.concatenate(o_cols, axis=0)
    ssq = jnp.sum(o * o, axis=0, keepdims=True)
    o = o * lax.rsqrt(ssq * (1.0 / D_A) + EPS) * g_ref[...]
    sel = lax.broadcasted_iota(jnp.int32, ot_ref.shape, 1) == b
    ot_ref[...] = jnp.where(sel, o, ot_ref[...])


def _window_roll(col, kt_ref, vt_ref, nk_ref, nv_ref):
    p = kt_ref.shape[-1]
    is_last = lax.broadcasted_iota(jnp.int32, (HEAD_DIM_A, p), 1) == p - 1
    for base, src, dst in ((D_A, kt_ref, nk_ref), (2 * D_A, vt_ref, nv_ref)):
        for h in range(N_HEADS_A):
            new_col = col[base + h * HEAD_DIM_A:base + (h + 1) * HEAD_DIM_A]
            dst[0, h] = jnp.where(is_last, new_col, pltpu.roll(src[0, h], p - 1, axis=1))


N_ROWS = 3
ROW_W, ROW_A, ROW_R = range(N_ROWS)


def _mlstm_step_pre_kernel(xb_ref, gt_ref, gbias_ref, sc_ref, cw_ref, cb_ref, wq_ref, wk_ref, nt_ref, mt_ref,
                           cact_out, qt_out, kt_out, vt_out, vst_out, rows_out, nt_out, mt_out):
    E = HEAD_DIM_B
    x = xb_ref[...]
    c = cb_ref[...] + x * cw_ref[CONV_W - 1:CONV_W, :]
    for t in range(CONV_W - 1):
        c = c + sc_ref[t] * cw_ref[t:t + 1, :]
    c_act = c * jax.nn.sigmoid(c)
    cact_out[...] = c_act
    gts_t = (gt_ref[...] + gbias_ref[...]).T
    logf_t = _log_sigmoid(gts_t)
    for h in range(N_HEADS_B):
        sl = slice(h * HEAD_PAD, (h + 1) * HEAD_PAD)
        ch_bf = c_act[:, sl].astype(bf16)
        q = jnp.dot(ch_bf, wq_ref[h], preferred_element_type=f32)
        k = jnp.dot(ch_bf, wk_ref[h], preferred_element_type=f32) * (1.0 / math.sqrt(HEAD_DIM_B))
        qt = q.T[:E]
        kt = k.T[:E]
        vt = x[:, sl].T[:E]
        i_pre = gts_t[h:h + 1]
        a = logf_t[N_HEADS_B + h:N_HEADS_B + h + 1]
        m_old = mt_ref[h:h + 1]
        m_t = jnp.maximum(a + m_old, i_pre)
        w_state = jnp.exp(a + m_old - m_t)
        w_in = jnp.exp(i_pre - m_t)
        amat = jnp.sum(qt * kt, axis=0, keepdims=True) * w_in
        n_old = nt_ref[h]
        den = w_state * jnp.sum(n_old * qt, axis=0, keepdims=True) + amat
        nt_out[h] = w_state * n_old + w_in * kt
        mt_out[h:h + 1] = m_t
        qt_out[h] = qt
        kt_out[h] = kt
        vt_out[h] = vt
        vst_out[h] = w_in * vt
        rows_out[h, ROW_W:ROW_W + 1] = w_state
        rows_out[h, ROW_A:ROW_A + 1] = amat
        rows_out[h, ROW_R:ROW_R + 1] = 1.0 / jnp.maximum(jnp.abs(den), jnp.exp(-m_t))


def _mlstm_step_pre(xb, gts, gbias, sc_t, cw, cb, wq, wk, nt, mt):
    nb = xb.shape[0]
    hd = jax.ShapeDtypeStruct((N_HEADS_B, HEAD_DIM_B, nb), f32)
    return pl.pallas_call(
        _mlstm_step_pre_kernel,
        out_shape=(jax.ShapeDtypeStruct((nb, D_B_PAD), f32), hd, hd, hd, hd,
                   jax.ShapeDtypeStruct((N_HEADS_B, N_ROWS, nb), f32), hd,
                   jax.ShapeDtypeStruct((N_HEADS_B, nb), f32)),
        compiler_params=pltpu.CompilerParams(vmem_limit_bytes=VMEM_LIMIT_BYTES),
        name="mlstm_step_pre",
    )(xb, gts, gbias, sc_t, cw, cb, wq, wk, nt, mt)


ST_VT = 40


def _mlstm_step_state_kernel(c_ref, qt_ref, kt_ref, vt_ref, vst_ref, rows_ref, ob_ref, cact_ref, mg_ref,
                             skip_ref, c_out, out_ref, cq_scr):
    vb = pl.program_id(1)
    E = HEAD_DIM_B
    qt = qt_ref[0]
    kt = kt_ref[0]
    w_state = rows_ref[0, ROW_W:ROW_W + 1]

    def per_row(v, carry):
        c_old = c_ref[0, v]
        vs = vst_ref[0, pl.ds(vb * ST_VT + v, 1), :]
        c_out[0, v] = w_state * c_old + vs * kt
        cq_scr[pl.ds(vb * ST_VT + v, 1), :] = jnp.sum(c_old * qt, axis=0, keepdims=True)
        return carry

    lax.fori_loop(0, ST_VT, per_row, 0)

    @pl.when(vb == pl.num_programs(1) - 1)
    def _finish_head():
        ht = (w_state * cq_scr[...] + rows_ref[0, ROW_A:ROW_A + 1] * vt_ref[0]) * rows_ref[0, ROW_R:ROW_R + 1]
        hh = jnp.concatenate([ht, jnp.zeros((HEAD_PAD - E, ht.shape[1]), f32)], axis=0).T
        hn = hh * lax.rsqrt(jnp.sum(hh * hh, axis=-1, keepdims=True) * (1.0 / E) + EPS) * mg_ref[0]
        out_ref[0] = jax.nn.sigmoid(ob_ref[0]) * (hn + skip_ref[0] * cact_ref[0])


def _mlstm_step_state(ct, qt, kt, vt, vst, rows, ob3, cact3, mg3, skip3):
    nh, e, _, nb = ct.shape
    assert e % ST_VT == 0
    head = lambda shape: pl.BlockSpec((1,) + shape, lambda h, v: (h,) + (0,) * len(shape))
    cblk = pl.BlockSpec((1, ST_VT, e, nb), lambda h, v: (h, v, 0, 0))
    return pl.pallas_call(
        _mlstm_step_state_kernel,
        out_shape=(jax.ShapeDtypeStruct(ct.shape, f32), jax.ShapeDtypeStruct((nh, nb, HEAD_PAD), f32)),
        grid=(nh, e // ST_VT),
        in_specs=[cblk, head((e, nb)), head((e, nb)), head((e, nb)), head((e, nb)), head((N_ROWS, nb)),
                  head((nb, HEAD_PAD)), head((nb, HEAD_PAD)), head((1, HEAD_PAD)), head((1, HEAD_PAD))],
        out_specs=(cblk, head((nb, HEAD_PAD))),
        scratch_shapes=[pltpu.VMEM((e, nb), f32)],
        compiler_params=pltpu.CompilerParams(
            dimension_semantics=("arbitrary", "arbitrary"),
            vmem_limit_bytes=VMEM_LIMIT_BYTES),
        name="mlstm_step_state",
    )(ct, qt, kt, vt, vst, rows, ob3, cact3, mg3, skip3)


def kernel(x_prompt, x_sample, cache_win_k, cache_win_v, state_conv, state_C, state_n, state_m,
           rel_bias, norm1_g, w_in, gate_bias, conv_w, conv_b, wq_head, wk_head, attn_out_g,
           mh_norm_g, skip, w_out, norm2_g, w_ff1, w_ff2, final_g):
    Bp, Sp, _ = x_prompt.shape
    Bs, Ss, _ = x_sample.shape
    assert Ss == 1
    g1 = norm1_g[0].reshape(1, D_MODEL)
    g2 = norm2_g[0].reshape(1, D_MODEL)
    gf = final_g.reshape(1, D_MODEL)
    w_in_pad = _pad_w_in(w_in[0]).astype(bf16)
    wo = jnp.concatenate([w_out[0][:D_A], jnp.zeros((LANES, D_MODEL), f32),
                          _pad_heads(w_out[0][D_A:].T).T], axis=0).astype(bf16)
    w1 = w_ff1[0].astype(bf16)
    w2 = w_ff2[0].astype(bf16)
    hpad = HEAD_PAD - HEAD_DIM_B
    wq_p = jnp.pad(wq_head[0], ((0, 0), (0, hpad), (0, hpad))).astype(bf16)
    wk_p = jnp.pad(wk_head[0], ((0, 0), (0, hpad), (0, hpad))).astype(bf16)
    gbias = jnp.pad(gate_bias[0], (0, LANES - N_GATES)).reshape(1, LANES)
    cw_p = _pad_heads(conv_w[0])
    cb_p = _pad_heads(conv_b[0]).reshape(1, D_B_PAD)
    mg_p = _pad_heads(mh_norm_g[0]).reshape(1, D_B_PAD)
    skip_p = _pad_heads(skip[0]).reshape(1, D_B_PAD)

    xp2 = x_prompt.reshape(Bp * Sp, D_MODEL)
    P = min(WINDOW_MAX, Sp)
    slabs, k_win, v_win, xb, ob, gts = _norm_inproj(xp2, g1, w_in_pad, tm=512, seq=Sp, win=P)
    out_a = _attn_prompt(slabs, rel_bias, attn_out_g[0], Bp, Sp)
    xs2 = x_sample.reshape(Bs, D_MODEL)
    zt_s, xb_s, ob_s, gts_s = _norm_inproj_step(xs2, g1, w_in[0][:, :3 * D_A].T.astype(bf16), w_in_pad)
    out_b, c_p, n_p, m_p, tail_p, oat_s, nkt, nvt = _mlstm_prompt_attn_step(
        xb, ob, gts, gbias, cw_p, cb_p, wq_p, wk_p, mg_p, skip_p, Bp, Sp,
        zt_s, cache_win_k[0].transpose(0, 2, 3, 1), cache_win_v[0].transpose(0, 2, 3, 1), rel_bias,
        attn_out_g[0])
    y_p = _out_ffn(xp2, out_a, out_b, wo, g2, w1, w2, gf, tm=512).reshape(Bp, Sp, D_MODEL)
    win5 = lambda t: t.reshape(N_PAIRS, Bp, P, LANES).transpose(1, 2, 0, 3).reshape(
        1, Bp, P, N_HEADS_A, HEAD_DIM_A)
    st_p = (win5(k_win), win5(v_win), _unpad_heads(tail_p[:, SUBLANES - (CONV_W - 1):])[None],
            c_p[:, :, :HEAD_DIM_B, :HEAD_DIM_B][None], n_p[:, :, 0, :HEAD_DIM_B][None],
            m_p[:, :, 0, 0][None])

    oa_s = oat_s.reshape(N_PAIRS, LANES, Bs).transpose(0, 2, 1)
    new_k = nkt.transpose(0, 3, 1, 2)[None]
    new_v = nvt.transpose(0, 3, 1, 2)[None]
    sc_t = _pad_heads(state_conv[0].transpose(1, 0, 2))
    cact_s, qt_s, kt_s, vt_s, vst_s, rows_s, nt_s, mt_s = _mlstm_step_pre(
        xb_s, gts_s, gbias, sc_t, cw_p, cb_p, wq_p, wk_p,
        state_n[0].transpose(1, 2, 0), state_m[0].T)
    heads3 = lambda t: t.reshape(t.shape[0], N_HEADS_B, HEAD_PAD).transpose(1, 0, 2)
    ct_s, outb3 = _mlstm_step_state(state_C[0].transpose(1, 2, 3, 0), qt_s, kt_s, vt_s, vst_s, rows_s,
                                    heads3(ob_s), heads3(cact_s), heads3(mg_p), heads3(skip_p))
    outb_s = outb3.transpose(1, 0, 2).reshape(Bs, D_B_PAD)
    y_s = _out_ffn(xs2, oa_s, outb_s, wo, g2, w1, w2, gf, tm=Bs).reshape(Bs, Ss, D_MODEL)
    new_conv = jnp.concatenate([state_conv[0][:, 1:], _unpad_heads(xb_s)[:, None]], axis=1)
    st_s = (new_k, new_v, new_conv[None], ct_s.transpose(3, 0, 1, 2)[None],
            nt_s.transpose(2, 0, 1)[None], mt_s.T[None])
    return (y_p, y_s) + st_p + st_s
```

```python
import functools
import math

import jax
import jax.numpy as jnp
from jax import lax
from jax.experimental import pallas as pl
from jax.experimental.pallas import tpu as pltpu

D_MODEL = 1024
HEAD_DIM_A = 64
N_HEADS_A = 6
D_A = N_HEADS_A * HEAD_DIM_A
D_B = D_MODEL - D_A
N_HEADS_B = 4
HEAD_DIM_B = D_B // N_HEADS_B
DILATED_BRANCHES = ((128, 1), (512, 4), (2048, 16))
WINDOW_MAX = 2048
N_BUCKETS = 32
REL_MAX_DIST = 2048
CONV_W = 4
MLSTM_CHUNK = 128
D_FF = 4 * D_MODEL
N_GATES = 2 * N_HEADS_B
D_IN = 3 * D_A + 2 * D_B + N_GATES
SPLITS = [D_A, 2 * D_A, 3 * D_A, 3 * D_A + D_B, 3 * D_A + 2 * D_B]
EPS = 1e-6
NEG = -1e30

LANES = 128
SUBLANES = 8
VMEM_LIMIT_BYTES = 56 * 1024 * 1024

N_PAIRS = D_A // LANES
N_SLABS = 3 * N_PAIRS
HEAD_PAD = 2 * LANES
D_B_PAD = N_HEADS_B * HEAD_PAD
D_IN_PAD = 3 * D_A + 2 * D_B_PAD + LANES
D_O_PAD = D_A + LANES + D_B_PAD
BLK = 128
RES = DILATED_BRANCHES[-1][1]
Q_SUPER = BLK * RES

f32 = jnp.float32
bf16 = jnp.bfloat16


def _round_up(n, m):
    return -(-n // m) * m


def _rms(xf, g):
    return xf * lax.rsqrt(jnp.mean(xf * xf, axis=-1, keepdims=True) + EPS) * g


def _pad_heads(t):
    t = t.reshape(t.shape[:-1] + (N_HEADS_B, HEAD_DIM_B))
    t = jnp.pad(t, [(0, 0)] * (t.ndim - 1) + [(0, HEAD_PAD - HEAD_DIM_B)])
    return t.reshape(t.shape[:-2] + (D_B_PAD,))


def _unpad_heads(t):
    t = t.reshape(t.shape[:-1] + (N_HEADS_B, HEAD_PAD))[..., :HEAD_DIM_B]
    return t.reshape(t.shape[:-2] + (D_B,))


def _norm_inproj_kernel(x_ref, g_ref, w_ref, slab_ref, kwin_ref, vwin_ref, xb_ref, ob_ref, gt_ref):
    h = _rms(x_ref[...], g_ref[...]).astype(bf16)
    qkv = jnp.dot(h, w_ref[:, 0:3 * D_A], preferred_element_type=f32)
    for j in range(N_SLABS):
        slab_ref[j] = qkv[:, j * LANES:(j + 1) * LANES]
    for p in range(N_PAIRS):
        kwin_ref[p] = qkv[:, (N_PAIRS + p) * LANES:(N_PAIRS + p + 1) * LANES]
        vwin_ref[p] = qkv[:, (2 * N_PAIRS + p) * LANES:(2 * N_PAIRS + p + 1) * LANES]
    o = 3 * D_A
    xb_ref[...] = jnp.dot(h, w_ref[:, o:o + D_B_PAD], preferred_element_type=f32)
    o += D_B_PAD
    ob_ref[...] = jnp.dot(h, w_ref[:, o:o + D_B_PAD], preferred_element_type=f32)
    o += D_B_PAD
    gt_ref[...] = jnp.dot(h, w_ref[:, o:o + LANES], preferred_element_type=f32)


def _norm_inproj(x, g, w_pad_bf16, tm, seq, win):
    m, d = x.shape
    assert m % seq == 0 and seq % tm == 0 and win % tm == 0
    tiles_seq, tiles_win = seq // tm, win // tm
    row = lambda w: pl.BlockSpec((tm, w), lambda i: (i, 0))

    def win_block(i):
        return (0, (i // tiles_seq) * tiles_win + jnp.maximum(i % tiles_seq - (tiles_seq - tiles_win), 0), 0)

    return pl.pallas_call(
        _norm_inproj_kernel,
        out_shape=(
            jax.ShapeDtypeStruct((N_SLABS, m, LANES), f32),
            jax.ShapeDtypeStruct((N_PAIRS, (m // seq) * win, LANES), f32),
            jax.ShapeDtypeStruct((N_PAIRS, (m // seq) * win, LANES), f32),
            jax.ShapeDtypeStruct((m, D_B_PAD), f32),
            jax.ShapeDtypeStruct((m, D_B_PAD), f32),
            jax.ShapeDtypeStruct((m, LANES), f32),
        ),
        grid=(m // tm,),
        in_specs=[
            row(d),
            pl.BlockSpec((1, d), lambda i: (0, 0)),
            pl.BlockSpec((d, D_IN_PAD), lambda i: (0, 0), pipeline_mode=pl.Buffered(1)),
        ],
        out_specs=(
            pl.BlockSpec((N_SLABS, tm, LANES), lambda i: (0, i, 0)),
            pl.BlockSpec((N_PAIRS, tm, LANES), win_block),
            pl.BlockSpec((N_PAIRS, tm, LANES), win_block),
            row(D_B_PAD), row(D_B_PAD), row(LANES),
        ),
        compiler_params=pltpu.CompilerParams(
            dimension_semantics=("arbitrary",),
            vmem_limit_bytes=VMEM_LIMIT_BYTES),
        name="norm_inproj",
    )(x, g, w_pad_bf16)


def _norm_inproj_step_kernel(x_ref, g_ref, wt_ref, w_ref, zt_ref, xb_ref, ob_ref, gt_ref):
    h = _rms(x_ref[...], g_ref[...]).astype(bf16)
    zt_ref[...] = lax.dot_general(wt_ref[...], h, (((1,), (1,)), ((), ())), preferred_element_type=f32)
    o = 3 * D_A
    xb_ref[...] = jnp.dot(h, w_ref[:, o:o + D_B_PAD], preferred_element_type=f32)
    o += D_B_PAD
    ob_ref[...] = jnp.dot(h, w_ref[:, o:o + D_B_PAD], preferred_element_type=f32)
    o += D_B_PAD
    gt_ref[...] = jnp.dot(h, w_ref[:, o:o + LANES], preferred_element_type=f32)


def _norm_inproj_step(x, g, wqkv_t_bf16, w_pad_bf16):
    nb = x.shape[0]
    wide = jax.ShapeDtypeStruct((nb, D_B_PAD), f32)
    return pl.pallas_call(
        _norm_inproj_step_kernel,
        out_shape=(jax.ShapeDtypeStruct((3 * D_A, nb), f32), wide, wide,
                   jax.ShapeDtypeStruct((nb, LANES), f32)),
        compiler_params=pltpu.CompilerParams(vmem_limit_bytes=VMEM_LIMIT_BYTES),
        name="norm_inproj_step",
    )(x, g, wqkv_t_bf16, w_pad_bf16)


def _pad_w_in(w_in):
    zeros = lambda n: jnp.zeros((w_in.shape[0], n), w_in.dtype)
    pieces = [w_in[:, :3 * D_A]]
    for base in (SPLITS[2], SPLITS[3]):
        for h in range(N_HEADS_B):
            pieces += [w_in[:, base + h * HEAD_DIM_B:base + (h + 1) * HEAD_DIM_B], zeros(HEAD_PAD - HEAD_DIM_B)]
    pieces += [w_in[:, SPLITS[4]:], zeros(LANES - N_GATES)]
    return jnp.concatenate(pieces, axis=1)


def _out_ffn_kernel(x_ref, oa_ref, ob_ref, wo_ref, g2_ref, w1_ref, w2_ref, gf_ref, y_ref, *, ff_chunk):
    o = jnp.concatenate([oa_ref[p].astype(bf16) for p in range(N_PAIRS)]
                        + [jnp.zeros((ob_ref.shape[0], LANES), bf16), ob_ref[...].astype(bf16)], axis=1)
    x1 = x_ref[...] + jnp.dot(o, wo_ref[...], preferred_element_type=f32)
    h2 = _rms(x1, g2_ref[...]).astype(bf16)
    acc = x1
    for c in range(D_FF // ff_chunk):
        u = jnp.dot(h2, w1_ref[:, c * ff_chunk:(c + 1) * ff_chunk], preferred_element_type=f32)
        u = jnp.square(jnp.maximum(u, 0.0)).astype(bf16)
        acc = acc + jnp.dot(u, w2_ref[c * ff_chunk:(c + 1) * ff_chunk, :],
                            preferred_element_type=f32)
    y_ref[...] = _rms(acc, gf_ref[...])


def _out_ffn(x, oa, ob, wo, g2, w1, w2, gf, tm, ff_chunk=1024):
    m, d = x.shape
    assert wo.shape[0] == D_O_PAD
    const = lambda i: (0, 0)
    single = dict(pipeline_mode=pl.Buffered(1))
    row = lambda w: pl.BlockSpec((tm, w), lambda i: (i, 0))
    return pl.pallas_call(
        functools.partial(_out_ffn_kernel, ff_chunk=ff_chunk),
        out_shape=jax.ShapeDtypeStruct((m, d), f32),
        grid=(m // tm,),
        in_specs=[
            row(d), pl.BlockSpec((N_PAIRS, tm, LANES), lambda i: (0, i, 0)), row(D_B_PAD),
            pl.BlockSpec((D_O_PAD, d), const, **single),
            pl.BlockSpec((1, d), const),
            pl.BlockSpec((d, D_FF), const, **single),
            pl.BlockSpec((D_FF, d), const, **single),
            pl.BlockSpec((1, d), const),
        ],
        out_specs=row(d),
        compiler_params=pltpu.CompilerParams(
            dimension_semantics=("arbitrary",),
            vmem_limit_bytes=VMEM_LIMIT_BYTES),
        name="outproj_ffn",
    )(x, oa, ob, wo, g2, w1, w2, gf)


def _t5_bucket(dist):
    max_exact = N_BUCKETS // 2
    df = jnp.maximum(dist, 1).astype(jnp.float32)
    large = max_exact + (jnp.log(df / max_exact) / math.log(REL_MAX_DIST / max_exact)
                         * (N_BUCKETS - max_exact)).astype(jnp.int32)
    large = jnp.minimum(large, N_BUCKETS - 1)
    return jnp.where(dist < max_exact, dist, large)


def _prompt_bias_table(rel_bias):
    period = 3 * BLK + 1
    r = jnp.arange(period)
    t = jnp.where(r < 2 * BLK, r, r - period)
    j = BLK - t
    band = (j >= 0) & (j <= BLK)
    tabs = []
    for (_, d) in DILATED_BRANCHES:
        vals = rel_bias[_t5_bucket(jnp.clip(j, 0, None) * d)].astype(f32)
        w = jnp.where(band[:, None], vals, NEG).T
        flat = jnp.tile(w, (1, BLK))[:, :BLK * (period - 1)]
        toe = flat.reshape(N_HEADS_A, BLK, period - 1)[:, :, :2 * BLK]
        sub = RES // d
        if sub > 1:
            pos = jnp.arange(BLK)
            nat = (pos % (BLK // sub)) * sub + pos // (BLK // sub)
            pq = jax.nn.one_hot(nat, BLK, dtype=f32)
            pk = jnp.kron(jnp.eye(2, dtype=f32), pq)
            toe = jnp.einsum('pq,hqk->hpk', pq, toe, precision=lax.Precision.HIGHEST)
            toe = jnp.einsum('hpk,jk->hpj', toe, pk, precision=lax.Precision.HIGHEST)
        tabs.append(toe.reshape(N_PAIRS, 2 * BLK, 2 * BLK))
    return jnp.stack(tabs)


def _attn_prompt_kernel(bias_ref, q_ref, k_ref, v_ref, g_ref, o_ref, q_scr, kv_scr, m_scr, l_scr, acc_scr):
    n = pl.program_id(1)
    step = pl.program_id(0) * pl.num_programs(1) + n
    cur = lax.rem(step, 2)
    prv = 1 - cur

    @pl.when(step == 0)
    def _init():
        kv_scr[1] = jnp.zeros(kv_scr.shape[1:], f32)

    def deinterleave(r, carry):
        rows = pl.ds(pl.multiple_of(r * BLK, BLK), BLK)
        for p in range(N_PAIRS):
            q_scr[p, rows, :] = q_ref.at[p][pl.ds(r, BLK, stride=RES), :]
            kv_scr[cur, p, rows, :] = k_ref.at[p][pl.ds(r, BLK, stride=RES), :]
            kv_scr[cur, N_PAIRS + p, rows, :] = v_ref.at[p][pl.ds(r, BLK, stride=RES), :]
        return carry

    lax.fori_loop(0, RES, deinterleave, 0, unroll=4)

    lane = lax.broadcasted_iota(jnp.int32, (BLK, LANES), 1)
    low = lane < HEAD_DIM_A
    kcol = lax.broadcasted_iota(jnp.int32, (2 * BLK, 2 * BLK), 1)
    first_extra = jnp.where((kcol < BLK) & (n == 0), NEG, 0.0).astype(f32)

    def job(g, d, buf_a, off_a, off_b, res, extra):
        sub = RES // d
        run = BLK // sub

        def starts(off):
            return [pl.multiple_of((c * d + res) * BLK + off, SUBLANES) for c in range(sub)]

        def gather(ref, lead, off):
            return jnp.concatenate([ref[lead + (pl.ds(s, run), slice(None))] for s in starts(off)], axis=0)

        def scatter(ref, p, val):
            for c, s in enumerate(starts(off_b)):
                ref[p, pl.ds(s, run), :] = val[c * run:(c + 1) * run]

        def finish(p, s):
            vv = jnp.concatenate([gather(kv_scr, (buf_a, N_PAIRS + p), off_a),
                                  gather(kv_scr, (cur, N_PAIRS + p), off_b)], axis=0)
            m = jnp.max(s, axis=-1, keepdims=True)
            e = jnp.exp(s - m)
            l = jnp.sum(e, axis=-1, keepdims=True)
            o = jnp.dot(e.astype(bf16), vv.astype(bf16), preferred_element_type=f32)
            o_t = jnp.where(low, o[:BLK], o[BLK:])
            m_t = jnp.where(low, m[:BLK], m[BLK:])
            l_t = jnp.where(low, l[:BLK], l[BLK:])
            if g == 0:
                scatter(acc_scr, p, o_t)
                scatter(m_scr, p, m_t)
                scatter(l_scr, p, l_t)
            else:
                m_o = gather(m_scr, (p,), off_b)
                m_n = jnp.maximum(m_o, m_t)
                al = jnp.exp(m_o - m_n)
                be = jnp.exp(m_t - m_n)
                scatter(acc_scr, p, al * gather(acc_scr, (p,), off_b) + be * o_t)
                scatter(l_scr, p, al * gather(l_scr, (p,), off_b) + be * l_t)
                scatter(m_scr, p, m_n)

        scores = []
        for p in range(N_PAIRS):
            q2 = gather(q_scr, (p,), off_b) * (1.0 / math.sqrt(HEAD_DIM_A))
            qcat = jnp.concatenate([jnp.where(low, q2, 0.0), jnp.where(low, 0.0, q2)], axis=0)
            kk = jnp.concatenate([gather(kv_scr, (buf_a, p), off_a), gather(kv_scr, (cur, p), off_b)], axis=0)
            s = lax.dot_general(qcat.astype(bf16), kk.astype(bf16), (((1,), (1,)), ((), ())),
                                preferred_element_type=f32)
            s = s + bias_ref[g, p]
            if extra is not None:
                s = s + extra
            if d == 1:
                finish(p, s)
            else:
                scores.append(s)
        for p, s in enumerate(scores):
            finish(p, s)

    for g, (_, d) in enumerate(DILATED_BRANCHES):
        nu = Q_SUPER // (BLK * d)
        run = BLK * d // RES
        log_d = d.bit_length() - 1

        def first(res, carry, g=g, d=d, nu=nu, run=run):
            job(g, d, prv, (nu - 1) * run, 0, res, first_extra)
            return carry

        def rest(idx, carry, g=g, d=d, run=run, log_d=log_d):
            u = 1 + lax.shift_right_logical(idx, log_d)
            res = jnp.bitwise_and(idx, d - 1)
            job(g, d, cur, (u - 1) * run, u * run, res, None)
            return carry

        if d == 1:
            first(0, 0)
        else:
            lax.fori_loop(0, d, first, 0, unroll=4)
        if nu > 1:
            lax.fori_loop(0, (nu - 1) * d, rest, 0, unroll=5 if d == 1 else 4)

    def fin(r, carry):
        rows = pl.ds(pl.multiple_of(r * BLK, BLK), BLK)
        os_ = [acc_scr[p, rows, :] / l_scr[p, rows, :] for p in range(N_PAIRS)]
        ss = sum(jnp.sum(o * o, axis=-1, keepdims=True) for o in os_)
        sc = lax.rsqrt(ss * (1.0 / D_A) + EPS)
        for p in range(N_PAIRS):
            o_ref.at[p][pl.ds(r, BLK, stride=RES), :] = os_[p] * sc * g_ref[:, p * LANES:(p + 1) * LANES]
        return carry

    lax.fori_loop(0, RES, fin, 0, unroll=4)


def _attn_prompt(slabs, rel_bias, attn_out_g, batch, seq):
    assert seq % Q_SUPER == 0
    nsb = seq // Q_SUPER
    nbr = len(DILATED_BRANCHES)
    slab = lambda grp: pl.BlockSpec((N_PAIRS, Q_SUPER, LANES), lambda bi, ni: (grp, bi * nsb + ni, 0))
    return pl.pallas_call(
        _attn_prompt_kernel,
        out_shape=jax.ShapeDtypeStruct((N_PAIRS, batch * seq, LANES), f32),
        grid=(batch, nsb),
        in_specs=[
            pl.BlockSpec((nbr, N_PAIRS, 2 * BLK, 2 * BLK), lambda bi, ni: (0, 0, 0, 0),
                         pipeline_mode=pl.Buffered(1)),
            slab(0), slab(1), slab(2),
            pl.BlockSpec((1, D_A), lambda bi, ni: (0, 0)),
        ],
        out_specs=pl.BlockSpec((N_PAIRS, Q_SUPER, LANES), lambda bi, ni: (0, bi * nsb + ni, 0)),
        scratch_shapes=[
            pltpu.VMEM((N_PAIRS, Q_SUPER, LANES), f32),
            pltpu.VMEM((2, 2 * N_PAIRS, Q_SUPER, LANES), f32),
            pltpu.VMEM((N_PAIRS, Q_SUPER, LANES), f32),
            pltpu.VMEM((N_PAIRS, Q_SUPER, LANES), f32),
            pltpu.VMEM((N_PAIRS, Q_SUPER, LANES), f32),
        ],
        compiler_params=pltpu.CompilerParams(
            dimension_semantics=("arbitrary", "arbitrary"),
            vmem_limit_bytes=VMEM_LIMIT_BYTES),
        name="attn_prompt",
    )(_prompt_bias_table(rel_bias), slabs, slabs, slabs, attn_out_g.reshape(1, D_A))


def _log_sigmoid(x):
    return -(jnp.maximum(-x, 0.0) + jnp.log1p(jnp.exp(-jnp.abs(x))))


def _mlstm_prompt_kernel(xb_ref, ob_ref, gt_ref, gbias_ref, cw_ref, cb_ref, wq_ref, wk_ref, mg_ref,
                         skip_ref, bias_ref, cnt_ref, bias0_ref, zt_ref, ga_ref, kt_ref, vt_ref,
                         out_ref, c_out_ref, n_out_ref, m_out_ref, conv_out_ref, ot_ref, nk_ref, nv_ref,
                         conv_scr, c_scr, n_scr, m_scr):
    c_idx = pl.program_id(1)
    L = MLSTM_CHUNK

    seq_idx = pl.program_id(0) * pl.num_programs(1) + c_idx

    @pl.when(seq_idx == 0)
    def _init_decode():
        ot_ref[...] = jnp.zeros_like(ot_ref)

    new_col = _new_token_column(seq_idx, zt_ref)
    _attn_step(seq_idx, new_col, bias_ref, cnt_ref, bias0_ref, ga_ref, kt_ref, vt_ref, ot_ref, nk_ref, nv_ref)

    @pl.when(c_idx == 0)
    def _init():
        conv_scr[0:SUBLANES, :] = jnp.zeros((SUBLANES, D_B_PAD), f32)
        c_scr[...] = jnp.zeros_like(c_scr)
        n_scr[...] = jnp.zeros_like(n_scr)
        m_scr[...] = jnp.zeros_like(m_scr)

    x = xb_ref[...]
    conv_scr[SUBLANES:SUBLANES + L, :] = x
    c = cb_ref[...] + x * cw_ref[CONV_W - 1:CONV_W, :]
    for i in range(CONV_W - 1):
        sh = CONV_W - 1 - i
        c = c + conv_scr[SUBLANES - sh:SUBLANES - sh + L, :] * cw_ref[i:i + 1, :]
    conv_scr[0:SUBLANES, :] = x[L - SUBLANES:, :]
    c_act = c * jax.nn.sigmoid(c)

    gts = gt_ref[...] + gbias_ref[...]
    logf = _log_sigmoid(gts)
    row = lax.broadcasted_iota(jnp.int32, (L, L), 0)
    col = lax.broadcasted_iota(jnp.int32, (L, L), 1)
    causal = row >= col
    a_all = jnp.dot(causal.astype(f32), logf, precision=lax.Precision.HIGHEST,
                    preferred_element_type=f32)
    gts_t = gts.T
    a_t = a_all.T

    for h in range(N_HEADS_B):
        sl = slice(h * HEAD_PAD, (h + 1) * HEAD_PAD)
        ch = c_act[:, sl]
        ch_bf = ch.astype(bf16)
        q = jnp.dot(ch_bf, wq_ref[h], preferred_element_type=f32)
        k = jnp.dot(ch_bf, wk_ref[h], preferred_element_type=f32) * (1.0 / math.sqrt(HEAD_DIM_B))
        v = x[:, sl]
        q_bf, k_bf = q.astype(bf16), k.astype(bf16)
        a_col = a_all[:, N_HEADS_B + h:N_HEADS_B + h + 1]
        i_col = gts[:, h:h + 1]
        a_row = a_t[N_HEADS_B + h:N_HEADS_B + h + 1, :]
        i_row = gts_t[h:h + 1, :]
        m_prev = m_scr[h]
        dmat = jnp.where(causal, a_col - a_row + i_row, NEG)
        g = a_col + m_prev
        m_t = jnp.maximum(g, jnp.max(dmat, axis=-1, keepdims=True))
        w_state = jnp.exp(g - m_t)
        s = lax.dot_general(q_bf, k_bf, (((1,), (1,)), ((), ())), preferred_element_type=f32)
        amat = s * jnp.exp(dmat - m_t)
        c_state = c_scr[h]
        inter = lax.dot_general(q_bf, c_state.astype(bf16), (((1,), (1,)), ((), ())),
                                preferred_element_type=f32)
        num = w_state * inter + jnp.dot(amat.astype(bf16), v.astype(bf16),
                                        preferred_element_type=f32)
        n_row = n_scr[h]
        den = (w_state * jnp.sum(q * n_row, axis=-1, keepdims=True)
               + jnp.sum(amat, axis=-1, keepdims=True))
        hh = num / jnp.maximum(jnp.abs(den), jnp.exp(-m_t))
        hn = hh * lax.rsqrt(jnp.sum(hh * hh, axis=-1, keepdims=True) * (1.0 / HEAD_DIM_B) + EPS)
        hn = hn * mg_ref[:, sl]
        out_ref[:, sl] = jax.nn.sigmoid(ob_ref[:, sl]) * (hn + skip_ref[:, sl] * ch)

        b_tot = a_col[L - 1:L, :]
        wl = b_tot - a_col + i_col
        m_new = jnp.maximum(b_tot + m_prev, jnp.max(wl, axis=0, keepdims=True))
        wk = jnp.exp(wl - m_new)
        decay = jnp.exp(b_tot + m_prev - m_new)
        upd = lax.dot_general((wk * v).astype(bf16), k_bf, (((0,), (0,)), ((), ())),
                              preferred_element_type=f32)
        c_scr[h] = decay * c_state + upd
        n_scr[h] = decay * n_row + jnp.sum(wk * k, axis=0, keepdims=True)
        m_scr[h] = m_new

    @pl.when(c_idx == pl.num_programs(1) - 1)
    def _final():
        c_out_ref[0] = c_scr[...]
        n_out_ref[0] = n_scr[...]
        m_out_ref[0] = m_scr[...]
        conv_out_ref[0] = x[L - SUBLANES:, :]


def _mlstm_prompt_attn_step(xb, ob, gates, gbias, cw, cb, wq, wk, mg, skip, batch, seq,
                            zt, cache_kt, cache_vt, rel_bias, attn_out_g):
    L = MLSTM_CHUNK
    assert seq % L == 0
    nc = seq // L
    nb, p = cache_kt.shape[0], cache_kt.shape[-1]
    assert nb == batch * nc, "one decode sequence per (prompt sequence, chunk) grid step"
    for (w, d) in DILATED_BRANCHES:
        assert w <= p and w // d == BLK
    bias, count, bias0 = _step_bias_rows(rel_bias, p)
    win = pl.BlockSpec((1, N_HEADS_A, HEAD_DIM_A, p), lambda bi, ci: (bi * nc + ci, 0, 0, 0))
    buf = jax.ShapeDtypeStruct(cache_kt.shape, f32)
    rows = lambda w: pl.BlockSpec((L, w), lambda bi, ci: (bi * nc + ci, 0))
    const2 = lambda shape: pl.BlockSpec(shape, lambda bi, ci: (0, 0))
    const3 = lambda shape: pl.BlockSpec(shape, lambda bi, ci: (0, 0, 0))
    state = lambda shape: pl.BlockSpec((1,) + shape, lambda bi, ci: (bi,) + (0,) * len(shape))
    return pl.pallas_call(
        _mlstm_prompt_kernel,
        out_shape=(
            jax.ShapeDtypeStruct((batch * seq, D_B_PAD), f32),
            jax.ShapeDtypeStruct((batch, N_HEADS_B, HEAD_PAD, HEAD_PAD), f32),
            jax.ShapeDtypeStruct((batch, N_HEADS_B, 1, HEAD_PAD), f32),
            jax.ShapeDtypeStruct((batch, N_HEADS_B, 1, 1), f32),
            jax.ShapeDtypeStruct((batch, SUBLANES, D_B_PAD), f32),
            jax.ShapeDtypeStruct((D_A, nb), f32), buf, buf,
        ),
        grid=(batch, nc),
        in_specs=[rows(D_B_PAD), rows(D_B_PAD), rows(LANES), const2((1, LANES)),
                  const2((CONV_W, D_B_PAD)), const2((1, D_B_PAD)),
                  const3((N_HEADS_B, HEAD_PAD, HEAD_PAD)), const3((N_HEADS_B, HEAD_PAD, HEAD_PAD)),
                  const2((1, D_B_PAD)), const2((1, D_B_PAD)),
                  const3((N_HEADS_A, 1, p)), const2((1, p)), const3((N_HEADS_A, 1, 1)),
                  const2((3 * D_A, nb)), const2((D_A, 1)), win, win],
        out_specs=(rows(D_B_PAD), state((N_HEADS_B, HEAD_PAD, HEAD_PAD)),
                   state((N_HEADS_B, 1, HEAD_PAD)), state((N_HEADS_B, 1, 1)),
                   state((SUBLANES, D_B_PAD)),
                   const2((D_A, nb)), win, win),
        scratch_shapes=[
            pltpu.VMEM((SUBLANES + L, D_B_PAD), f32),
            pltpu.VMEM((N_HEADS_B, HEAD_PAD, HEAD_PAD), f32),
            pltpu.VMEM((N_HEADS_B, 1, HEAD_PAD), f32),
            pltpu.VMEM((N_HEADS_B, 1, 1), f32),
        ],
        compiler_params=pltpu.CompilerParams(
            dimension_semantics=("arbitrary", "arbitrary"),
            vmem_limit_bytes=VMEM_LIMIT_BYTES),
        name="mlstm_prompt_attn_step",
    )(xb, ob, gates, gbias, cw, cb, wq, wk, mg, skip,
      bias, count, bias0, zt, attn_out_g.reshape(D_A, 1), cache_kt, cache_vt)


def _step_bias_rows(rel_bias, p):
    dist = p - jnp.arange(p)
    full = rel_bias[_t5_bucket(dist)].astype(f32).T
    count = sum(((dist % d == 0) & (dist <= w)).astype(f32) for (w, d) in DILATED_BRANCHES)
    zero = rel_bias[_t5_bucket(jnp.zeros((1,), jnp.int32))].astype(f32).T
    return jnp.where(count[None, :] > 0, full, NEG)[:, None, :], count[None, :], zero[:, None, :]


def _new_token_column(b, zt_ref):
    zt = zt_ref[...]
    is_b = lax.broadcasted_iota(jnp.int32, zt.shape, 1) == b
    return jnp.sum(jnp.where(is_b, zt, 0.0), axis=1, keepdims=True)


def _attn_step(b, col, bias_ref, cnt_ref, bias0_ref, g_ref, kt_ref, vt_ref, ot_ref, nk_ref, nv_ref):
    nbr = len(DILATED_BRANCHES)
    scale = 1.0 / math.sqrt(HEAD_DIM_A)
    cnt = cnt_ref[...]
    weights = []
    for h in range(N_HEADS_A):
        q_col = col[h * HEAD_DIM_A:(h + 1) * HEAD_DIM_A] * scale
        kn_col = col[D_A + h * HEAD_DIM_A:D_A + (h + 1) * HEAD_DIM_A]
        kt = kt_ref[0, h]
        s = jnp.sum(kt * q_col, axis=0, keepdims=True)
        s_new = jnp.sum(kn_col * q_col, axis=0, keepdims=True)
        sb = s + bias_ref[h]
        s0 = s_new + bias0_ref[h]
        m = jnp.maximum(jnp.max(sb, axis=1, keepdims=True), s0)
        pw = cnt * jnp.exp(sb - m)
        e0 = nbr * jnp.exp(s0 - m)
        weights.append((pw, e0, jnp.sum(pw, axis=1, keepdims=True) + e0))
    _window_roll(col, kt_ref, vt_ref, nk_ref, nv_ref)
    o_cols = []
    for h, (pw, e0, l) in enumerate(weights):
        vn_col = col[2 * D_A + h * HEAD_DIM_A:2 * D_A + (h + 1) * HEAD_DIM_A]
        vt = vt_ref[0, h]
        o_cols.append((jnp.sum(vt * pw, axis=1, keepdims=True) + e0 * vn_col) / l)
    o = jnp.concatenate(o_cols, axis=0)
    ssq = jnp.sum(o * o, axis=0, keepdims=True)
    o = o * lax.rsqrt(ssq * (1.0 / D_A) + EPS) * g_ref[...]
    sel = lax.broadcasted_iota(jnp.int32, ot_ref.shape, 1) == b
    ot_ref[...] = jnp.where(sel, o, ot_ref[...])


def _window_roll(col, kt_ref, vt_ref, nk_ref, nv_ref):
    p = kt_ref.shape[-1]
    is_last = lax.broadcasted_iota(jnp.int32, (HEAD_DIM_A, p), 1) == p - 1
    for base, src, dst in ((D_A, kt_ref, nk_ref), (2 * D_A, vt_ref, nv_ref)):
        for h in range(N_HEADS_A):
            new_col = col[base + h * HEAD_DIM_A:base + (h + 1) * HEAD_DIM_A]
            dst[0, h] = jnp.where(is_last, new_col, pltpu.roll(src[0, h], p - 1, axis=1))


N_ROWS = 3
ROW_W, ROW_A, ROW_R = range(N_ROWS)


def _mlstm_step_pre_kernel(xb_ref, gt_ref, gbias_ref, sc_ref, cw_ref, cb_ref, wq_ref, wk_ref, nt_ref, mt_ref,
                           cact_out, qt_out, kt_out, vt_out, vst_out, rows_out, nt_out, mt_out):
    E = HEAD_DIM_B
    x = xb_ref[...]
    c = cb_ref[...] + x * cw_ref[CONV_W - 1:CONV_W, :]
    for t in range(CONV_W - 1):
        c = c + sc_ref[t] * cw_ref[t:t + 1, :]
    c_act = c * jax.nn.sigmoid(c)
    cact_out[...] = c_act
    gts_t = (gt_ref[...] + gbias_ref[...]).T
    logf_t = _log_sigmoid(gts_t)
    for h in range(N_HEADS_B):
        sl = slice(h * HEAD_PAD, (h + 1) * HEAD_PAD)
        ch_bf = c_act[:, sl].astype(bf16)
        q = jnp.dot(ch_bf, wq_ref[h], preferred_element_type=f32)
        k = jnp.dot(ch_bf, wk_ref[h], preferred_element_type=f32) * (1.0 / math.sqrt(HEAD_DIM_B))
        qt = q.T[:E]
        kt = k.T[:E]
        vt = x[:, sl].T[:E]
        i_pre = gts_t[h:h + 1]
        a = logf_t[N_HEADS_B + h:N_HEADS_B + h + 1]
        m_old = mt_ref[h:h + 1]
        m_t = jnp.maximum(a + m_old, i_pre)
        w_state = jnp.exp(a + m_old - m_t)
        w_in = jnp.exp(i_pre - m_t)
        amat = jnp.sum(qt * kt, axis=0, keepdims=True) * w_in
        n_old = nt_ref[h]
        den = w_state * jnp.sum(n_old * qt, axis=0, keepdims=True) + amat
        nt_out[h] = w_state * n_old + w_in * kt
        mt_out[h:h + 1] = m_t
        qt_out[h] = qt
        kt_out[h] = kt
        vt_out[h] = vt
        vst_out[h] = w_in * vt
        rows_out[h, ROW_W:ROW_W + 1] = w_state
        rows_out[h, ROW_A:ROW_A + 1] = amat
        rows_out[h, ROW_R:ROW_R + 1] = 1.0 / jnp.maximum(jnp.abs(den), jnp.exp(-m_t))


def _mlstm_step_pre(xb, gts, gbias, sc_t, cw, cb, wq, wk, nt, mt):
    nb = xb.shape[0]
    hd = jax.ShapeDtypeStruct((N_HEADS_B, HEAD_DIM_B, nb), f32)
    return pl.pallas_call(
        _mlstm_step_pre_kernel,
        out_shape=(jax.ShapeDtypeStruct((nb, D_B_PAD), f32), hd, hd, hd, hd,
                   jax.ShapeDtypeStruct((N_HEADS_B, N_ROWS, nb), f32), hd,
                   jax.ShapeDtypeStruct((N_HEADS_B, nb), f32)),
        compiler_params=pltpu.CompilerParams(vmem_limit_bytes=VMEM_LIMIT_BYTES),
        name="mlstm_step_pre",
    )(xb, gts, gbias, sc_t, cw, cb, wq, wk, nt, mt)


ST_VT = 40


def _mlstm_step_state_kernel(c_ref, qt_ref, kt_ref, vt_ref, vst_ref, rows_ref, ob_ref, cact_ref, mg_ref,
                             skip_ref, c_out, out_ref, cq_scr):
    vb = pl.program_id(1)
    E = HEAD_DIM_B
    qt = qt_ref[0]
    kt = kt_ref[0]
    w_state = rows_ref[0, ROW_W:ROW_W + 1]

    def per_row(v, carry):
        c_old = c_ref[0, v]
        vs = vst_ref[0, pl.ds(vb * ST_VT + v, 1), :]
        c_out[0, v] = w_state * c_old + vs * kt
        cq_scr[pl.ds(vb * ST_VT + v, 1), :] = jnp.sum(c_old * qt, axis=0, keepdims=True)
        return carry

    lax.fori_loop(0, ST_VT, per_row, 0)

    @pl.when(vb == pl.num_programs(1) - 1)
    def _finish_head():
        ht = (w_state * cq_scr[...] + rows_ref[0, ROW_A:ROW_A + 1] * vt_ref[0]) * rows_ref[0, ROW_R:ROW_R + 1]
        hh = jnp.concatenate([ht, jnp.zeros((HEAD_PAD - E, ht.shape[1]), f32)], axis=0).T
        hn = hh * lax.rsqrt(jnp.sum(hh * hh, axis=-1, keepdims=True) * (1.0 / E) + EPS) * mg_ref[0]
        out_ref[0] = jax.nn.sigmoid(ob_ref[0]) * (hn + skip_ref[0] * cact_ref[0])


def _mlstm_step_state(ct, qt, kt, vt, vst, rows, ob3, cact3, mg3, skip3):
    nh, e, _, nb = ct.shape
    assert e % ST_VT == 0
    head = lambda shape: pl.BlockSpec((1,) + shape, lambda h, v: (h,) + (0,) * len(shape))
    cblk = pl.BlockSpec((1, ST_VT, e, nb), lambda h, v: (h, v, 0, 0))
    return pl.pallas_call(
        _mlstm_step_state_kernel,
        out_shape=(jax.ShapeDtypeStruct(ct.shape, f32), jax.ShapeDtypeStruct((nh, nb, HEAD_PAD), f32)),
        grid=(nh, e // ST_VT),
        in_specs=[cblk, head((e, nb)), head((e, nb)), head((e, nb)), head((e, nb)), head((N_ROWS, nb)),
                  head((nb, HEAD_PAD)), head((nb, HEAD_PAD)), head((1, HEAD_PAD)), head((1, HEAD_PAD))],
        out_specs=(cblk, head((nb, HEAD_PAD))),
        scratch_shapes=[pltpu.VMEM((e, nb), f32)],
        compiler_params=pltpu.CompilerParams(
            dimension_semantics=("arbitrary", "arbitrary"),
            vmem_limit_bytes=VMEM_LIMIT_BYTES),
        name="mlstm_step_state",
    )(ct, qt, kt, vt, vst, rows, ob3, cact3, mg3, skip3)


def kernel(x_prompt, x_sample, cache_win_k, cache_win_v, state_conv, state_C, state_n, state_m,
           rel_bias, norm1_g, w_in, gate_bias, conv_w, conv_b, wq_head, wk_head, attn_out_g,
           mh_norm_g, skip, w_out, norm2_g, w_ff1, w_ff2, final_g):
    Bp, Sp, _ = x_prompt.shape
    Bs, Ss, _ = x_sample.shape
    assert Ss == 1
    g1 = norm1_g[0].reshape(1, D_MODEL)
    g2 = norm2_g[0].reshape(1, D_MODEL)
    gf = final_g.reshape(1, D_MODEL)
    w_in_pad = _pad_w_in(w_in[0].astype(bf16))
    wo_bf = w_out[0].astype(bf16)
    wo_rows = [wo_bf[:D_A], jnp.zeros((LANES, D_MODEL), bf16)]
    for h in range(N_HEADS_B):
        wo_rows += [wo_bf[D_A + h * HEAD_DIM_B:D_A + (h + 1) * HEAD_DIM_B],
                    jnp.zeros((HEAD_PAD - HEAD_DIM_B, D_MODEL), bf16)]
    wo = jnp.concatenate(wo_rows, axis=0)
    w1 = w_ff1[0].astype(bf16)
    w2 = w_ff2[0].astype(bf16)
    hpad = HEAD_PAD - HEAD_DIM_B
    wq_p = jnp.pad(wq_head[0], ((0, 0), (0, hpad), (0, hpad))).astype(bf16)
    wk_p = jnp.pad(wk_head[0], ((0, 0), (0, hpad), (0, hpad))).astype(bf16)
    gbias = jnp.pad(gate_bias[0], (0, LANES - N_GATES)).reshape(1, LANES)
    cw_p = _pad_heads(conv_w[0])
    cb_p = _pad_heads(conv_b[0]).reshape(1, D_B_PAD)
    mg_p = _pad_heads(mh_norm_g[0]).reshape(1, D_B_PAD)
    skip_p = _pad_heads(skip[0]).reshape(1, D_B_PAD)

    xp2 = x_prompt.reshape(Bp * Sp, D_MODEL)
    P = min(WINDOW_MAX, Sp)
    slabs, k_win, v_win, xb, ob, gts = _norm_inproj(xp2, g1, w_in_pad, tm=512, seq=Sp, win=P)
    out_a = _attn_prompt(slabs, rel_bias, attn_out_g[0], Bp, Sp)
    xs2 = x_sample.reshape(Bs, D_MODEL)
    zt_s, xb_s, ob_s, gts_s = _norm_inproj_step(xs2, g1, w_in[0][:, :3 * D_A].T.astype(bf16), w_in_pad)
    out_b, c_p, n_p, m_p, tail_p, oat_s, nkt, nvt = _mlstm_prompt_attn_step(
        xb, ob, gts, gbias, cw_p, cb_p, wq_p, wk_p, mg_p, skip_p, Bp, Sp,
        zt_s, cache_win_k[0].transpose(0, 2, 3, 1), cache_win_v[0].transpose(0, 2, 3, 1), rel_bias,
        attn_out_g[0])
    y_p = _out_ffn(xp2, out_a, out_b, wo, g2, w1, w2, gf, tm=512).reshape(Bp, Sp, D_MODEL)
    win5 = lambda t: t.reshape(N_PAIRS, Bp, P, LANES).transpose(1, 2, 0, 3).reshape(
        1, Bp, P, N_HEADS_A, HEAD_DIM_A)
    st_p = (win5(k_win), win5(v_win), _unpad_heads(tail_p[:, SUBLANES - (CONV_W - 1):])[None],
            c_p[:, :, :HEAD_DIM_B, :HEAD_DIM_B][None], n_p[:, :, 0, :HEAD_DIM_B][None],
            m_p[:, :, 0, 0][None])

    oa_s = oat_s.reshape(N_PAIRS, LANES, Bs).transpose(0, 2, 1)
    new_k = nkt.transpose(0, 3, 1, 2)[None]
    new_v = nvt.transpose(0, 3, 1, 2)[None]
    sc_t = _pad_heads(state_conv[0].transpose(1, 0, 2))
    cact_s, qt_s, kt_s, vt_s, vst_s, rows_s, nt_s, mt_s = _mlstm_step_pre(
        xb_s, gts_s, gbias, sc_t, cw_p, cb_p, wq_p, wk_p,
        state_n[0].transpose(1, 2, 0), state_m[0].T)
    heads3 = lambda t: t.reshape(t.shape[0], N_HEADS_B, HEAD_PAD).transpose(1, 0, 2)
    ct_s, outb3 = _mlstm_step_state(state_C[0].transpose(1, 2, 3, 0), qt_s, kt_s, vt_s, vst_s, rows_s,
                                    heads3(ob_s), heads3(cact_s), heads3(mg_p), heads3(skip_p))
    outb_s = outb3.transpose(1, 0, 2).reshape(Bs, D_B_PAD)
    y_s = _out_ffn(xs2, oa_s, outb_s, wo, g2, w1, w2, gf, tm=Bs).reshape(Bs, Ss, D_MODEL)
    new_conv = jnp.concatenate([state_conv[0][:, 1:], _unpad_heads(xb_s)[:, None]], axis=1)
    st_s = (new_k, new_v, new_conv[None], ct_s.transpose(3, 0, 1, 2)[None],
            nt_s.transpose(2, 0, 1)[None], mt_s.T[None])
    return (y_p, y_s) + st_p + st_s
```

```python
import functools
import math

import jax
import jax.numpy as jnp
from jax import lax
from jax.experimental import pallas as pl
from jax.experimental.pallas import tpu as pltpu

D_MODEL = 1024
HEAD_DIM_A = 64
N_HEADS_A = 6
D_A = N_HEADS_A * HEAD_DIM_A
D_B = D_MODEL - D_A
N_HEADS_B = 4
HEAD_DIM_B = D_B // N_HEADS_B
DILATED_BRANCHES = ((128, 1), (512, 4), (2048, 16))
WINDOW_MAX = 2048
N_BUCKETS = 32
REL_MAX_DIST = 2048
CONV_W = 4
MLSTM_CHUNK = 128
D_FF = 4 * D_MODEL
N_GATES = 2 * N_HEADS_B
SPLITS = [D_A, 2 * D_A, 3 * D_A, 3 * D_A + D_B, 3 * D_A + 2 * D_B]
EPS = 1e-6
NEG = -1e30

LANES = 128
SUBLANES = 8
VMEM_LIMIT_BYTES = 56 * 1024 * 1024
ROW_TILE = 512

N_PAIRS = D_A // LANES
N_SLABS = 3 * N_PAIRS
HEAD_PAD = 2 * LANES
D_B_PAD = N_HEADS_B * HEAD_PAD
D_IN_PAD = 3 * D_A + 2 * D_B_PAD + LANES
D_O_PAD = D_A + LANES + D_B_PAD
BLK = 128
RES = DILATED_BRANCHES[-1][1]
Q_SUPER = BLK * RES

f32 = jnp.float32
bf16 = jnp.bfloat16


def _rms(xf, g):
    return xf * lax.rsqrt(jnp.mean(xf * xf, axis=-1, keepdims=True) + EPS) * g


def _pad_heads(t):
    t = t.reshape(t.shape[:-1] + (N_HEADS_B, HEAD_DIM_B))
    t = jnp.pad(t, [(0, 0)] * (t.ndim - 1) + [(0, HEAD_PAD - HEAD_DIM_B)])
    return t.reshape(t.shape[:-2] + (D_B_PAD,))


def _unpad_heads(t):
    t = t.reshape(t.shape[:-1] + (N_HEADS_B, HEAD_PAD))[..., :HEAD_DIM_B]
    return t.reshape(t.shape[:-2] + (D_B,))


def _norm_inproj_kernel(x_ref, g_ref, w_ref, slab_ref, kwin_ref, vwin_ref, xb_ref, ob_ref, gt_ref):
    h = _rms(x_ref[...], g_ref[...]).astype(bf16)
    qkv = jnp.dot(h, w_ref[:, 0:3 * D_A], preferred_element_type=f32)
    for j in range(N_SLABS):
        slab_ref[j] = qkv[:, j * LANES:(j + 1) * LANES]
    for p in range(N_PAIRS):
        kwin_ref[p] = qkv[:, (N_PAIRS + p) * LANES:(N_PAIRS + p + 1) * LANES]
        vwin_ref[p] = qkv[:, (2 * N_PAIRS + p) * LANES:(2 * N_PAIRS + p + 1) * LANES]
    o = 3 * D_A
    xb_ref[...] = jnp.dot(h, w_ref[:, o:o + D_B_PAD], preferred_element_type=f32)
    o += D_B_PAD
    ob_ref[...] = jnp.dot(h, w_ref[:, o:o + D_B_PAD], preferred_element_type=f32)
    o += D_B_PAD
    gt_ref[...] = jnp.dot(h, w_ref[:, o:o + LANES], preferred_element_type=f32)


def _norm_inproj(x, g, w_pad_bf16, tm, seq, win):
    m, d = x.shape
    assert m % seq == 0 and seq % tm == 0 and win % tm == 0
    tiles_seq, tiles_win = seq // tm, win // tm
    row = lambda w: pl.BlockSpec((tm, w), lambda i: (i, 0))

    def win_block(i):
        return (0, (i // tiles_seq) * tiles_win + jnp.maximum(i % tiles_seq - (tiles_seq - tiles_win), 0), 0)

    return pl.pallas_call(
        _norm_inproj_kernel,
        out_shape=(
            jax.ShapeDtypeStruct((N_SLABS, m, LANES), f32),
            jax.ShapeDtypeStruct((N_PAIRS, (m // seq) * win, LANES), f32),
            jax.ShapeDtypeStruct((N_PAIRS, (m // seq) * win, LANES), f32),
            jax.ShapeDtypeStruct((m, D_B_PAD), f32),
            jax.ShapeDtypeStruct((m, D_B_PAD), f32),
            jax.ShapeDtypeStruct((m, LANES), f32),
        ),
        grid=(m // tm,),
        in_specs=[
            row(d),
            pl.BlockSpec((1, d), lambda i: (0, 0)),
            pl.BlockSpec((d, D_IN_PAD), lambda i: (0, 0), pipeline_mode=pl.Buffered(1)),
        ],
        out_specs=(
            pl.BlockSpec((N_SLABS, tm, LANES), lambda i: (0, i, 0)),
            pl.BlockSpec((N_PAIRS, tm, LANES), win_block),
            pl.BlockSpec((N_PAIRS, tm, LANES), win_block),
            row(D_B_PAD), row(D_B_PAD), row(LANES),
        ),
        compiler_params=pltpu.CompilerParams(
            dimension_semantics=("arbitrary",),
            vmem_limit_bytes=VMEM_LIMIT_BYTES),
        name="norm_inproj",
    )(x, g, w_pad_bf16)


def _norm_inproj_step_kernel(x_ref, g_ref, wt_ref, w_ref, zt_ref, xb_ref, ob_ref, gt_ref):
    h = _rms(x_ref[...], g_ref[...]).astype(bf16)
    zt_ref[...] = lax.dot_general(wt_ref[...], h, (((1,), (1,)), ((), ())), preferred_element_type=f32)
    o = 3 * D_A
    xb_ref[...] = jnp.dot(h, w_ref[:, o:o + D_B_PAD], preferred_element_type=f32)
    o += D_B_PAD
    ob_ref[...] = jnp.dot(h, w_ref[:, o:o + D_B_PAD], preferred_element_type=f32)
    o += D_B_PAD
    gt_ref[...] = jnp.dot(h, w_ref[:, o:o + LANES], preferred_element_type=f32)


def _norm_inproj_step(x, g, wqkv_t_bf16, w_pad_bf16):
    nb = x.shape[0]
    wide = jax.ShapeDtypeStruct((nb, D_B_PAD), f32)
    return pl.pallas_call(
        _norm_inproj_step_kernel,
        out_shape=(jax.ShapeDtypeStruct((3 * D_A, nb), f32), wide, wide,
                   jax.ShapeDtypeStruct((nb, LANES), f32)),
        compiler_params=pltpu.CompilerParams(vmem_limit_bytes=VMEM_LIMIT_BYTES),
        name="norm_inproj_step",
    )(x, g, wqkv_t_bf16, w_pad_bf16)


def _pad_w_in(w_in):
    zeros = lambda n: jnp.zeros((w_in.shape[0], n), w_in.dtype)
    pieces = [w_in[:, :3 * D_A]]
    for base in (SPLITS[2], SPLITS[3]):
        for h in range(N_HEADS_B):
            pieces += [w_in[:, base + h * HEAD_DIM_B:base + (h + 1) * HEAD_DIM_B], zeros(HEAD_PAD - HEAD_DIM_B)]
    pieces += [w_in[:, SPLITS[4]:], zeros(LANES - N_GATES)]
    return jnp.concatenate(pieces, axis=1)


def _out_ffn_kernel(x_ref, oa_ref, ob_ref, wo_ref, g2_ref, w1_ref, w2_ref, gf_ref, y_ref, *, ff_chunk):
    o = jnp.concatenate([oa_ref[p].astype(bf16) for p in range(N_PAIRS)]
                        + [jnp.zeros((ob_ref.shape[0], LANES), bf16), ob_ref[...].astype(bf16)], axis=1)
    x1 = x_ref[...] + jnp.dot(o, wo_ref[...], preferred_element_type=f32)
    h2 = _rms(x1, g2_ref[...]).astype(bf16)
    acc = x1
    for c in range(D_FF // ff_chunk):
        u = jnp.dot(h2, w1_ref[:, c * ff_chunk:(c + 1) * ff_chunk], preferred_element_type=f32)
        u = jnp.square(jnp.maximum(u, 0.0)).astype(bf16)
        acc = acc + jnp.dot(u, w2_ref[c * ff_chunk:(c + 1) * ff_chunk, :],
                            preferred_element_type=f32)
    y_ref[...] = _rms(acc, gf_ref[...])


def _out_ffn(x, oa, ob, wo, g2, w1, w2, gf, tm, ff_chunk=1024):
    m, d = x.shape
    assert wo.shape[0] == D_O_PAD
    const = lambda i: (0, 0)
    single = dict(pipeline_mode=pl.Buffered(1))
    row = lambda w: pl.BlockSpec((tm, w), lambda i: (i, 0))
    return pl.pallas_call(
        functools.partial(_out_ffn_kernel, ff_chunk=ff_chunk),
        out_shape=jax.ShapeDtypeStruct((m, d), f32),
        grid=(m // tm,),
        in_specs=[
            row(d), pl.BlockSpec((N_PAIRS, tm, LANES), lambda i: (0, i, 0)), row(D_B_PAD),
            pl.BlockSpec((D_O_PAD, d), const, **single),
            pl.BlockSpec((1, d), const),
            pl.BlockSpec((d, D_FF), const, **single),
            pl.BlockSpec((D_FF, d), const, **single),
            pl.BlockSpec((1, d), const),
        ],
        out_specs=row(d),
        compiler_params=pltpu.CompilerParams(
            dimension_semantics=("arbitrary",),
            vmem_limit_bytes=VMEM_LIMIT_BYTES),
        name="outproj_ffn",
    )(x, oa, ob, wo, g2, w1, w2, gf)


def _t5_bucket(dist):
    max_exact = N_BUCKETS // 2
    df = jnp.maximum(dist, 1).astype(jnp.float32)
    large = max_exact + (jnp.log(df / max_exact) / math.log(REL_MAX_DIST / max_exact)
                         * (N_BUCKETS - max_exact)).astype(jnp.int32)
    large = jnp.minimum(large, N_BUCKETS - 1)
    return jnp.where(dist < max_exact, dist, large)


def _prompt_bias_table(rel_bias):
    period = 3 * BLK + 1
    r = jnp.arange(period)
    t = jnp.where(r < 2 * BLK, r, r - period)
    j = BLK - t
    band = (j >= 0) & (j <= BLK)
    tabs = []
    for (_, d) in DILATED_BRANCHES:
        vals = rel_bias[_t5_bucket(jnp.clip(j, 0, None) * d)].astype(f32)
        w = jnp.where(band[:, None], vals, NEG).T
        flat = jnp.tile(w, (1, BLK))[:, :BLK * (period - 1)]
        toe = flat.reshape(N_HEADS_A, BLK, period - 1)[:, :, :2 * BLK]
        sub = RES // d
        if sub > 1:
            pos = jnp.arange(BLK)
            nat = (pos % (BLK // sub)) * sub + pos // (BLK // sub)
            pq = jax.nn.one_hot(nat, BLK, dtype=f32)
            pk = jnp.kron(jnp.eye(2, dtype=f32), pq)
            toe = jnp.einsum('pq,hqk->hpk', pq, toe, precision=lax.Precision.HIGHEST)
            toe = jnp.einsum('hpk,jk->hpj', toe, pk, precision=lax.Precision.HIGHEST)
        tabs.append(toe.reshape(N_PAIRS, 2 * BLK, 2 * BLK))
    return jnp.stack(tabs)


def _attn_prompt_kernel(bias_ref, q_ref, k_ref, v_ref, g_ref, o_ref, q_scr, kv_scr, m_scr, l_scr, acc_scr):
    n = pl.program_id(1)
    step = pl.program_id(0) * pl.num_programs(1) + n
    cur = lax.rem(step, 2)
    prv = 1 - cur

    @pl.when(step == 0)
    def _init():
        kv_scr[1] = jnp.zeros(kv_scr.shape[1:], f32)

    def deinterleave(r, carry):
        rows = pl.ds(pl.multiple_of(r * BLK, BLK), BLK)
        for p in range(N_PAIRS):
            q_scr[p, rows, :] = q_ref.at[p][pl.ds(r, BLK, stride=RES), :]
            kv_scr[cur, p, rows, :] = k_ref.at[p][pl.ds(r, BLK, stride=RES), :]
            kv_scr[cur, N_PAIRS + p, rows, :] = v_ref.at[p][pl.ds(r, BLK, stride=RES), :]
        return carry

    lax.fori_loop(0, RES, deinterleave, 0, unroll=4)

    lane = lax.broadcasted_iota(jnp.int32, (BLK, LANES), 1)
    low = lane < HEAD_DIM_A
    kcol = lax.broadcasted_iota(jnp.int32, (2 * BLK, 2 * BLK), 1)
    first_extra = jnp.where((kcol < BLK) & (n == 0), NEG, 0.0).astype(f32)

    def job(g, d, buf_a, off_a, off_b, res, extra):
        sub = RES // d
        run = BLK // sub

        def starts(off):
            return [pl.multiple_of((c * d + res) * BLK + off, SUBLANES) for c in range(sub)]

        def gather(ref, lead, off):
            return jnp.concatenate([ref[lead + (pl.ds(s, run), slice(None))] for s in starts(off)], axis=0)

        def scatter(ref, p, val):
            for c, s in enumerate(starts(off_b)):
                ref[p, pl.ds(s, run), :] = val[c * run:(c + 1) * run]

        def finish(p, s):
            vv = jnp.concatenate([gather(kv_scr, (buf_a, N_PAIRS + p), off_a),
                                  gather(kv_scr, (cur, N_PAIRS + p), off_b)], axis=0)
            m = jnp.max(s, axis=-1, keepdims=True)
            e = jnp.exp(s - m)
            l = jnp.sum(e, axis=-1, keepdims=True)
            o = jnp.dot(e.astype(bf16), vv.astype(bf16), preferred_element_type=f32)
            o_t = jnp.where(low, o[:BLK], o[BLK:])
            m_t = jnp.where(low, m[:BLK], m[BLK:])
            l_t = jnp.where(low, l[:BLK], l[BLK:])
            if g == 0:
                scatter(acc_scr, p, o_t)
                scatter(m_scr, p, m_t)
                scatter(l_scr, p, l_t)
            else:
                m_o = gather(m_scr, (p,), off_b)
                m_n = jnp.maximum(m_o, m_t)
                al = jnp.exp(m_o - m_n)
                be = jnp.exp(m_t - m_n)
                scatter(acc_scr, p, al * gather(acc_scr, (p,), off_b) + be * o_t)
                scatter(l_scr, p, al * gather(l_scr, (p,), off_b) + be * l_t)
                scatter(m_scr, p, m_n)

        scores = []
        for p in range(N_PAIRS):
            q2 = gather(q_scr, (p,), off_b) * (1.0 / math.sqrt(HEAD_DIM_A))
            qcat = jnp.concatenate([jnp.where(low, q2, 0.0), jnp.where(low, 0.0, q2)], axis=0)
            kk = jnp.concatenate([gather(kv_scr, (buf_a, p), off_a), gather(kv_scr, (cur, p), off_b)], axis=0)
            s = lax.dot_general(qcat.astype(bf16), kk.astype(bf16), (((1,), (1,)), ((), ())),
                                preferred_element_type=f32)
            s = s + bias_ref[g, p]
            if extra is not None:
                s = s + extra
            if d == 1:
                finish(p, s)
            else:
                scores.append(s)
        for p, s in enumerate(scores):
            finish(p, s)

    for g, (_, d) in enumerate(DILATED_BRANCHES):
        nu = Q_SUPER // (BLK * d)
        run = BLK * d // RES
        log_d = d.bit_length() - 1

        def first(res, carry, g=g, d=d, nu=nu, run=run):
            job(g, d, prv, (nu - 1) * run, 0, res, first_extra)
            return carry

        def rest(idx, carry, g=g, d=d, run=run, log_d=log_d):
            u = 1 + lax.shift_right_logical(idx, log_d)
            res = jnp.bitwise_and(idx, d - 1)
            job(g, d, cur, (u - 1) * run, u * run, res, None)
            return carry

        if d == 1:
            first(0, 0)
        else:
            lax.fori_loop(0, d, first, 0, unroll=4)
        if nu > 1:
            lax.fori_loop(0, (nu - 1) * d, rest, 0, unroll=5 if d == 1 else 4)

    def fin(r, carry):
        rows = pl.ds(pl.multiple_of(r * BLK, BLK), BLK)
        os_ = [acc_scr[p, rows, :] / l_scr[p, rows, :] for p in range(N_PAIRS)]
        ss = sum(jnp.sum(o * o, axis=-1, keepdims=True) for o in os_)
        sc = lax.rsqrt(ss * (1.0 / D_A) + EPS)
        for p in range(N_PAIRS):
            o_ref.at[p][pl.ds(r, BLK, stride=RES), :] = os_[p] * sc * g_ref[:, p * LANES:(p + 1) * LANES]
        return carry

    lax.fori_loop(0, RES, fin, 0, unroll=4)


def _attn_prompt(slabs, rel_bias, attn_out_g, batch, seq):
    assert seq % Q_SUPER == 0
    nsb = seq // Q_SUPER
    nbr = len(DILATED_BRANCHES)
    slab = lambda grp: pl.BlockSpec((N_PAIRS, Q_SUPER, LANES), lambda bi, ni: (grp, bi * nsb + ni, 0))
    return pl.pallas_call(
        _attn_prompt_kernel,
        out_shape=jax.ShapeDtypeStruct((N_PAIRS, batch * seq, LANES), f32),
        grid=(batch, nsb),
        in_specs=[
            pl.BlockSpec((nbr, N_PAIRS, 2 * BLK, 2 * BLK), lambda bi, ni: (0, 0, 0, 0),
                         pipeline_mode=pl.Buffered(1)),
            slab(0), slab(1), slab(2),
            pl.BlockSpec((1, D_A), lambda bi, ni: (0, 0)),
        ],
        out_specs=pl.BlockSpec((N_PAIRS, Q_SUPER, LANES), lambda bi, ni: (0, bi * nsb + ni, 0)),
        scratch_shapes=[
            pltpu.VMEM((N_PAIRS, Q_SUPER, LANES), f32),
            pltpu.VMEM((2, 2 * N_PAIRS, Q_SUPER, LANES), f32),
            pltpu.VMEM((N_PAIRS, Q_SUPER, LANES), f32),
            pltpu.VMEM((N_PAIRS, Q_SUPER, LANES), f32),
            pltpu.VMEM((N_PAIRS, Q_SUPER, LANES), f32),
        ],
        compiler_params=pltpu.CompilerParams(
            dimension_semantics=("arbitrary", "arbitrary"),
            vmem_limit_bytes=VMEM_LIMIT_BYTES),
        name="attn_prompt",
    )(_prompt_bias_table(rel_bias), slabs, slabs, slabs, attn_out_g.reshape(1, D_A))


def _log_sigmoid(x):
    return -(jnp.maximum(-x, 0.0) + jnp.log1p(jnp.exp(-jnp.abs(x))))


def _mlstm_prompt_kernel(xb_ref, ob_ref, gt_ref, gbias_ref, cw_ref, cb_ref, wq_ref, wk_ref, mg_ref,
                         skip_ref, bias_ref, cnt_ref, bias0_ref, zt_ref, ga_ref, kt_ref, vt_ref,
                         out_ref, c_out_ref, n_out_ref, m_out_ref, conv_out_ref, ot_ref, nk_ref, nv_ref,
                         conv_scr, c_scr, n_scr, m_scr):
    c_idx = pl.program_id(1)
    L = MLSTM_CHUNK

    seq_idx = pl.program_id(0) * pl.num_programs(1) + c_idx

    @pl.when(seq_idx == 0)
    def _init_decode():
        ot_ref[...] = jnp.zeros_like(ot_ref)

    new_col = _new_token_column(seq_idx, zt_ref)
    _attn_step(seq_idx, new_col, bias_ref, cnt_ref, bias0_ref, ga_ref, kt_ref, vt_ref, ot_ref, nk_ref, nv_ref)

    @pl.when(c_idx == 0)
    def _init():
        conv_scr[0:SUBLANES, :] = jnp.zeros((SUBLANES, D_B_PAD), f32)
        c_scr[...] = jnp.zeros_like(c_scr)
        n_scr[...] = jnp.zeros_like(n_scr)
        m_scr[...] = jnp.zeros_like(m_scr)

    x = xb_ref[...]
    conv_scr[SUBLANES:SUBLANES + L, :] = x
    c = cb_ref[...] + x * cw_ref[CONV_W - 1:CONV_W, :]
    for i in range(CONV_W - 1):
        sh = CONV_W - 1 - i
        c = c + conv_scr[SUBLANES - sh:SUBLANES - sh + L, :] * cw_ref[i:i + 1, :]
    conv_scr[0:SUBLANES, :] = x[L - SUBLANES:, :]
    c_act = c * jax.nn.sigmoid(c)

    gts = gt_ref[...] + gbias_ref[...]
    logf = _log_sigmoid(gts)
    row = lax.broadcasted_iota(jnp.int32, (L, L), 0)
    col = lax.broadcasted_iota(jnp.int32, (L, L), 1)
    causal = row >= col
    a_all = jnp.dot(causal.astype(f32), logf, precision=lax.Precision.HIGHEST,
                    preferred_element_type=f32)
    gts_t = gts.T
    a_t = a_all.T

    for h in range(N_HEADS_B):
        sl = slice(h * HEAD_PAD, (h + 1) * HEAD_PAD)
        ch = c_act[:, sl]
        ch_bf = ch.astype(bf16)
        q = jnp.dot(ch_bf, wq_ref[h], preferred_element_type=f32)
        k = jnp.dot(ch_bf, wk_ref[h], preferred_element_type=f32) * (1.0 / math.sqrt(HEAD_DIM_B))
        v = x[:, sl]
        q_bf, k_bf = q.astype(bf16), k.astype(bf16)
        a_col = a_all[:, N_HEADS_B + h:N_HEADS_B + h + 1]
        i_col = gts[:, h:h + 1]
        a_row = a_t[N_HEADS_B + h:N_HEADS_B + h + 1, :]
        i_row = gts_t[h:h + 1, :]
        m_prev = m_scr[h]
        dmat = jnp.where(causal, a_col - a_row + i_row, NEG)
        g = a_col + m_prev
        m_t = jnp.maximum(g, jnp.max(dmat, axis=-1, keepdims=True))
        w_state = jnp.exp(g - m_t)
        s = lax.dot_general(q_bf, k_bf, (((1,), (1,)), ((), ())), preferred_element_type=f32)
        amat = s * jnp.exp(dmat - m_t)
        c_state = c_scr[h]
        inter = lax.dot_general(q_bf, c_state.astype(bf16), (((1,), (1,)), ((), ())),
                                preferred_element_type=f32)
        num = w_state * inter + jnp.dot(amat.astype(bf16), v.astype(bf16),
                                        preferred_element_type=f32)
        n_row = n_scr[h]
        den = (w_state * jnp.sum(q * n_row, axis=-1, keepdims=True)
               + jnp.sum(amat, axis=-1, keepdims=True))
        hh = num / jnp.maximum(jnp.abs(den), jnp.exp(-m_t))
        hn = hh * lax.rsqrt(jnp.sum(hh * hh, axis=-1, keepdims=True) * (1.0 / HEAD_DIM_B) + EPS)
        hn = hn * mg_ref[:, sl]
        out_ref[:, sl] = jax.nn.sigmoid(ob_ref[:, sl]) * (hn + skip_ref[:, sl] * ch)

        b_tot = a_col[L - 1:L, :]
        wl = b_tot - a_col + i_col
        m_new = jnp.maximum(b_tot + m_prev, jnp.max(wl, axis=0, keepdims=True))
        wk = jnp.exp(wl - m_new)
        decay = jnp.exp(b_tot + m_prev - m_new)
        upd = lax.dot_general((wk * v).astype(bf16), k_bf, (((0,), (0,)), ((), ())),
                              preferred_element_type=f32)
        c_scr[h] = decay * c_state + upd
        n_scr[h] = decay * n_row + jnp.sum(wk * k, axis=0, keepdims=True)
        m_scr[h] = m_new

    @pl.when(c_idx == pl.num_programs(1) - 1)
    def _final():
        c_out_ref[0] = c_scr[...]
        n_out_ref[0] = n_scr[...]
        m_out_ref[0] = m_scr[...]
        conv_out_ref[0] = x[L - SUBLANES:, :]


def _mlstm_prompt_attn_step(xb, ob, gates, gbias, cw, cb, wq, wk, mg, skip, batch, seq,
                            zt, cache_kt, cache_vt, rel_bias, attn_out_g):
    L = MLSTM_CHUNK
    assert seq % L == 0
    nc = seq // L
    nb, p = cache_kt.shape[0], cache_kt.shape[-1]
    assert nb == batch * nc, "one decode sequence per (prompt sequence, chunk) grid step"
    for (w, d) in DILATED_BRANCHES:
        assert w <= p and w // d == BLK
    bias, count, bias0 = _step_bias_rows(rel_bias, p)
    win = pl.BlockSpec((1, N_HEADS_A, HEAD_DIM_A, p), lambda bi, ci: (bi * nc + ci, 0, 0, 0))
    buf = jax.ShapeDtypeStruct(cache_kt.shape, f32)
    rows = lambda w: pl.BlockSpec((L, w), lambda bi, ci: (bi * nc + ci, 0))
    const2 = lambda shape: pl.BlockSpec(shape, lambda bi, ci: (0, 0))
    const3 = lambda shape: pl.BlockSpec(shape, lambda bi, ci: (0, 0, 0))
    state = lambda shape: pl.BlockSpec((1,) + shape, lambda bi, ci: (bi,) + (0,) * len(shape))
    return pl.pallas_call(
        _mlstm_prompt_kernel,
        out_shape=(
            jax.ShapeDtypeStruct((batch * seq, D_B_PAD), f32),
            jax.ShapeDtypeStruct((batch, N_HEADS_B, HEAD_PAD, HEAD_PAD), f32),
            jax.ShapeDtypeStruct((batch, N_HEADS_B, 1, HEAD_PAD), f32),
            jax.ShapeDtypeStruct((batch, N_HEADS_B, 1, 1), f32),
            jax.ShapeDtypeStruct((batch, SUBLANES, D_B_PAD), f32),
            jax.ShapeDtypeStruct((D_A, nb), f32), buf, buf,
        ),
        grid=(batch, nc),
        in_specs=[rows(D_B_PAD), rows(D_B_PAD), rows(LANES), const2((1, LANES)),
                  const2((CONV_W, D_B_PAD)), const2((1, D_B_PAD)),
                  const3((N_HEADS_B, HEAD_PAD, HEAD_PAD)), const3((N_HEADS_B, HEAD_PAD, HEAD_PAD)),
                  const2((1, D_B_PAD)), const2((1, D_B_PAD)),
                  const3((N_HEADS_A, 1, p)), const2((1, p)), const3((N_HEADS_A, 1, 1)),
                  const2((3 * D_A, nb)), const2((D_A, 1)), win, win],
        out_specs=(rows(D_B_PAD), state((N_HEADS_B, HEAD_PAD, HEAD_PAD)),
                   state((N_HEADS_B, 1, HEAD_PAD)), state((N_HEADS_B, 1, 1)),
                   state((SUBLANES, D_B_PAD)),
                   const2((D_A, nb)), win, win),
        scratch_shapes=[
            pltpu.VMEM((SUBLANES + L, D_B_PAD), f32),
            pltpu.VMEM((N_HEADS_B, HEAD_PAD, HEAD_PAD), f32),
            pltpu.VMEM((N_HEADS_B, 1, HEAD_PAD), f32),
            pltpu.VMEM((N_HEADS_B, 1, 1), f32),
        ],
        compiler_params=pltpu.CompilerParams(
            dimension_semantics=("arbitrary", "arbitrary"),
            vmem_limit_bytes=VMEM_LIMIT_BYTES),
        name="mlstm_prompt_attn_step",
    )(xb, ob, gates, gbias, cw, cb, wq, wk, mg, skip,
      bias, count, bias0, zt, attn_out_g.reshape(D_A, 1), cache_kt, cache_vt)


def _step_bias_rows(rel_bias, p):
    dist = p - jnp.arange(p)
    full = rel_bias[_t5_bucket(dist)].astype(f32).T
    count = sum(((dist % d == 0) & (dist <= w)).astype(f32) for (w, d) in DILATED_BRANCHES)
    zero = rel_bias[_t5_bucket(jnp.zeros((1,), jnp.int32))].astype(f32).T
    return jnp.where(count[None, :] > 0, full, NEG)[:, None, :], count[None, :], zero[:, None, :]


def _new_token_column(b, zt_ref):
    zt = zt_ref[...]
    is_b = lax.broadcasted_iota(jnp.int32, zt.shape, 1) == b
    return jnp.sum(jnp.where(is_b, zt, 0.0), axis=1, keepdims=True)


def _attn_step(b, col, bias_ref, cnt_ref, bias0_ref, g_ref, kt_ref, vt_ref, ot_ref, nk_ref, nv_ref):
    nbr = len(DILATED_BRANCHES)
    scale = 1.0 / math.sqrt(HEAD_DIM_A)
    cnt = cnt_ref[...]
    weights = []
    for h in range(N_HEADS_A):
        q_col = col[h * HEAD_DIM_A:(h + 1) * HEAD_DIM_A] * scale
        kn_col = col[D_A + h * HEAD_DIM_A:D_A + (h + 1) * HEAD_DIM_A]
        kt = kt_ref[0, h]
        s = jnp.sum(kt * q_col, axis=0, keepdims=True)
        s_new = jnp.sum(kn_col * q_col, axis=0, keepdims=True)
        sb = s + bias_ref[h]
        s0 = s_new + bias0_ref[h]
        m = jnp.maximum(jnp.max(sb, axis=1, keepdims=True), s0)
        pw = cnt * jnp.exp(sb - m)
        e0 = nbr * jnp.exp(s0 - m)
        weights.append((pw, e0, jnp.sum(pw, axis=1, keepdims=True) + e0))
    _window_roll(col, kt_ref, vt_ref, nk_ref, nv_ref)
    o_cols = []
    for h, (pw, e0, l) in enumerate(weights):
        vn_col = col[2 * D_A + h * HEAD_DIM_A:2 * D_A + (h + 1) * HEAD_DIM_A]
        vt = vt_ref[0, h]
        o_cols.append((jnp.sum(vt * pw, axis=1, keepdims=True) + e0 * vn_col) / l)
    o = jnp.concatenate(o_cols, axis=0)
    ssq = jnp.sum(o * o, axis=0, keepdims=True)
    o = o * lax.rsqrt(ssq * (1.0 / D_A) + EPS) * g_ref[...]
    sel = lax.broadcasted_iota(jnp.int32, ot_ref.shape, 1) == b
    ot_ref[...] = jnp.where(sel, o, ot_ref[...])


def _window_roll(col, kt_ref, vt_ref, nk_ref, nv_ref):
    p = kt_ref.shape[-1]
    is_last = lax.broadcasted_iota(jnp.int32, (HEAD_DIM_A, p), 1) == p - 1
    for base, src, dst in ((D_A, kt_ref, nk_ref), (2 * D_A, vt_ref, nv_ref)):
        for h in range(N_HEADS_A):
            new_col = col[base + h * HEAD_DIM_A:base + (h + 1) * HEAD_DIM_A]
            dst[0, h] = jnp.where(is_last, new_col, pltpu.roll(src[0, h], p - 1, axis=1))


N_ROWS = 3
ROW_W, ROW_A, ROW_R = range(N_ROWS)


def _mlstm_step_pre_kernel(xb_ref, gt_ref, gbias_ref, sc_ref, cw_ref, cb_ref, wq_ref, wk_ref, nt_ref, mt_ref,
                           cact_out, qt_out, kt_out, vt_out, vst_out, rows_out, nt_out, mt_out):
    E = HEAD_DIM_B
    x = xb_ref[...]
    c = cb_ref[...] + x * cw_ref[CONV_W - 1:CONV_W, :]
    for t in range(CONV_W - 1):
        c = c + sc_ref[t] * cw_ref[t:t + 1, :]
    c_act = c * jax.nn.sigmoid(c)
    cact_out[...] = c_act
    gts_t = (gt_ref[...] + gbias_ref[...]).T
    logf_t = _log_sigmoid(gts_t)
    for h in range(N_HEADS_B):
        sl = slice(h * HEAD_PAD, (h + 1) * HEAD_PAD)
        ch_bf = c_act[:, sl].astype(bf16)
        q = jnp.dot(ch_bf, wq_ref[h], preferred_element_type=f32)
        k = jnp.dot(ch_bf, wk_ref[h], preferred_element_type=f32) * (1.0 / math.sqrt(HEAD_DIM_B))
        qt = q.T[:E]
        kt = k.T[:E]
        vt = x[:, sl].T[:E]
        i_pre = gts_t[h:h + 1]
        a = logf_t[N_HEADS_B + h:N_HEADS_B + h + 1]
        m_old = mt_ref[h:h + 1]
        m_t = jnp.maximum(a + m_old, i_pre)
        w_state = jnp.exp(a + m_old - m_t)
        w_in = jnp.exp(i_pre - m_t)
        amat = jnp.sum(qt * kt, axis=0, keepdims=True) * w_in
        n_old = nt_ref[h]
        den = w_state * jnp.sum(n_old * qt, axis=0, keepdims=True) + amat
        nt_out[h] = w_state * n_old + w_in * kt
        mt_out[h:h + 1] = m_t
        qt_out[h] = qt
        kt_out[h] = kt
        vt_out[h] = vt
        vst_out[h] = w_in * vt
        rows_out[h, ROW_W:ROW_W + 1] = w_state
        rows_out[h, ROW_A:ROW_A + 1] = amat
        rows_out[h, ROW_R:ROW_R + 1] = 1.0 / jnp.maximum(jnp.abs(den), jnp.exp(-m_t))


def _mlstm_step_pre(xb, gts, gbias, sc_t, cw, cb, wq, wk, nt, mt):
    nb = xb.shape[0]
    hd = jax.ShapeDtypeStruct((N_HEADS_B, HEAD_DIM_B, nb), f32)
    return pl.pallas_call(
        _mlstm_step_pre_kernel,
        out_shape=(jax.ShapeDtypeStruct((nb, D_B_PAD), f32), hd, hd, hd, hd,
                   jax.ShapeDtypeStruct((N_HEADS_B, N_ROWS, nb), f32), hd,
                   jax.ShapeDtypeStruct((N_HEADS_B, nb), f32)),
        compiler_params=pltpu.CompilerParams(vmem_limit_bytes=VMEM_LIMIT_BYTES),
        name="mlstm_step_pre",
    )(xb, gts, gbias, sc_t, cw, cb, wq, wk, nt, mt)


ST_VT = 40


def _mlstm_step_state_kernel(c_ref, qt_ref, kt_ref, vt_ref, vst_ref, rows_ref, ob_ref, cact_ref, mg_ref,
                             skip_ref, c_out, out_ref, cq_scr):
    vb = pl.program_id(1)
    E = HEAD_DIM_B
    qt = qt_ref[0]
    kt = kt_ref[0]
    w_state = rows_ref[0, ROW_W:ROW_W + 1]

    def per_row(v, carry):
        c_old = c_ref[0, v]
        vs = vst_ref[0, pl.ds(vb * ST_VT + v, 1), :]
        c_out[0, v] = w_state * c_old + vs * kt
        cq_scr[pl.ds(vb * ST_VT + v, 1), :] = jnp.sum(c_old * qt, axis=0, keepdims=True)
        return carry

    lax.fori_loop(0, ST_VT, per_row, 0)

    @pl.when(vb == pl.num_programs(1) - 1)
    def _finish_head():
        ht = (w_state * cq_scr[...] + rows_ref[0, ROW_A:ROW_A + 1] * vt_ref[0]) * rows_ref[0, ROW_R:ROW_R + 1]
        hh = jnp.concatenate([ht, jnp.zeros((HEAD_PAD - E, ht.shape[1]), f32)], axis=0).T
        hn = hh * lax.rsqrt(jnp.sum(hh * hh, axis=-1, keepdims=True) * (1.0 / E) + EPS) * mg_ref[0]
        out_ref[0] = jax.nn.sigmoid(ob_ref[0]) * (hn + skip_ref[0] * cact_ref[0])


def _mlstm_step_state(ct, qt, kt, vt, vst, rows, ob3, cact3, mg3, skip3):
    nh, e, _, nb = ct.shape
    assert e % ST_VT == 0
    head = lambda shape: pl.BlockSpec((1,) + shape, lambda h, v: (h,) + (0,) * len(shape))
    cblk = pl.BlockSpec((1, ST_VT, e, nb), lambda h, v: (h, v, 0, 0))
    return pl.pallas_call(
        _mlstm_step_state_kernel,
        out_shape=(jax.ShapeDtypeStruct(ct.shape, f32), jax.ShapeDtypeStruct((nh, nb, HEAD_PAD), f32)),
        grid=(nh, e // ST_VT),
        in_specs=[cblk, head((e, nb)), head((e, nb)), head((e, nb)), head((e, nb)), head((N_ROWS, nb)),
                  head((nb, HEAD_PAD)), head((nb, HEAD_PAD)), head((1, HEAD_PAD)), head((1, HEAD_PAD))],
        out_specs=(cblk, head((nb, HEAD_PAD))),
        scratch_shapes=[pltpu.VMEM((e, nb), f32)],
        compiler_params=pltpu.CompilerParams(
            dimension_semantics=("arbitrary", "arbitrary"),
            vmem_limit_bytes=VMEM_LIMIT_BYTES),
        name="mlstm_step_state",
    )(ct, qt, kt, vt, vst, rows, ob3, cact3, mg3, skip3)


def kernel(x_prompt, x_sample, cache_win_k, cache_win_v, state_conv, state_C, state_n, state_m,
           rel_bias, norm1_g, w_in, gate_bias, conv_w, conv_b, wq_head, wk_head, attn_out_g,
           mh_norm_g, skip, w_out, norm2_g, w_ff1, w_ff2, final_g):
    Bp, Sp, _ = x_prompt.shape
    Bs, Ss, _ = x_sample.shape
    assert Ss == 1
    g1 = norm1_g[0].reshape(1, D_MODEL)
    g2 = norm2_g[0].reshape(1, D_MODEL)
    gf = final_g.reshape(1, D_MODEL)
    w_in_pad = _pad_w_in(w_in[0].astype(bf16))
    wo_bf = w_out[0].astype(bf16)
    wo_rows = [wo_bf[:D_A], jnp.zeros((LANES, D_MODEL), bf16)]
    for h in range(N_HEADS_B):
        wo_rows += [wo_bf[D_A + h * HEAD_DIM_B:D_A + (h + 1) * HEAD_DIM_B],
                    jnp.zeros((HEAD_PAD - HEAD_DIM_B, D_MODEL), bf16)]
    wo = jnp.concatenate(wo_rows, axis=0)
    w1 = w_ff1[0].astype(bf16)
    w2 = w_ff2[0].astype(bf16)
    hpad = HEAD_PAD - HEAD_DIM_B
    wq_p = jnp.pad(wq_head[0], ((0, 0), (0, hpad), (0, hpad))).astype(bf16)
    wk_p = jnp.pad(wk_head[0], ((0, 0), (0, hpad), (0, hpad))).astype(bf16)
    gbias = jnp.pad(gate_bias[0], (0, LANES - N_GATES)).reshape(1, LANES)
    cw_p = _pad_heads(conv_w[0])
    cb_p = _pad_heads(conv_b[0]).reshape(1, D_B_PAD)
    mg_p = _pad_heads(mh_norm_g[0]).reshape(1, D_B_PAD)
    skip_p = _pad_heads(skip[0]).reshape(1, D_B_PAD)

    xp2 = x_prompt.reshape(Bp * Sp, D_MODEL)
    P = min(WINDOW_MAX, Sp)
    slabs, k_win, v_win, xb, ob, gts = _norm_inproj(xp2, g1, w_in_pad, tm=ROW_TILE, seq=Sp, win=P)
    out_a = _attn_prompt(slabs, rel_bias, attn_out_g[0], Bp, Sp)
    xs2 = x_sample.reshape(Bs, D_MODEL)
    zt_s, xb_s, ob_s, gts_s = _norm_inproj_step(xs2, g1, w_in[0][:, :3 * D_A].T.astype(bf16), w_in_pad)
    out_b, c_p, n_p, m_p, tail_p, oat_s, nkt, nvt = _mlstm_prompt_attn_step(
        xb, ob, gts, gbias, cw_p, cb_p, wq_p, wk_p, mg_p, skip_p, Bp, Sp,
        zt_s, cache_win_k[0].transpose(0, 2, 3, 1), cache_win_v[0].transpose(0, 2, 3, 1), rel_bias,
        attn_out_g[0])
    y_p = _out_ffn(xp2, out_a, out_b, wo, g2, w1, w2, gf, tm=ROW_TILE).reshape(Bp, Sp, D_MODEL)
    win5 = lambda t: t.reshape(N_PAIRS, Bp, P, LANES).transpose(1, 2, 0, 3).reshape(
        1, Bp, P, N_HEADS_A, HEAD_DIM_A)
    st_p = (win5(k_win), win5(v_win), _unpad_heads(tail_p[:, SUBLANES - (CONV_W - 1):])[None],
            c_p[:, :, :HEAD_DIM_B, :HEAD_DIM_B][None], n_p[:, :, 0, :HEAD_DIM_B][None],
            m_p[:, :, 0, 0][None])

    oa_s = oat_s.reshape(N_PAIRS, LANES, Bs).transpose(0, 2, 1)
    new_k = nkt.transpose(0, 3, 1, 2)[None]
    new_v = nvt.transpose(0, 3, 1, 2)[None]
    sc_t = _pad_heads(state_conv[0].transpose(1, 0, 2))
    cact_s, qt_s, kt_s, vt_s, vst_s, rows_s, nt_s, mt_s = _mlstm_step_pre(
        xb_s, gts_s, gbias, sc_t, cw_p, cb_p, wq_p, wk_p,
        state_n[0].transpose(1, 2, 0), state_m[0].T)
    heads3 = lambda t: t.reshape(t.shape[0], N_HEADS_B, HEAD_PAD).transpose(1, 0, 2)
    ct_s, outb3 = _mlstm_step_state(state_C[0].transpose(1, 2, 3, 0), qt_s, kt_s, vt_s, vst_s, rows_s,
                                    heads3(ob_s), heads3(cact_s), heads3(mg_p), heads3(skip_p))
    outb_s = outb3.transpose(1, 0, 2).reshape(Bs, D_B_PAD)
    y_s = _out_ffn(xs2, oa_s, outb_s, wo, g2, w1, w2, gf, tm=Bs).reshape(Bs, Ss, D_MODEL)
    new_conv = jnp.concatenate([state_conv[0][:, 1:], _unpad_heads(xb_s)[:, None]], axis=1)
    st_s = (new_k, new_v, new_conv[None], ct_s.transpose(3, 0, 1, 2)[None],
            nt_s.transpose(2, 0, 1)[None], mt_s.T[None])
    return (y_p, y_s) + st_p + st_s
```

```python
import functools
import math

import jax
import jax.numpy as jnp
from jax import lax
from jax.experimental import pallas as pl
from jax.experimental.pallas import tpu as pltpu

D_MODEL = 1024
HEAD_DIM_A = 64
N_HEADS_A = 6
D_A = N_HEADS_A * HEAD_DIM_A
D_B = D_MODEL - D_A
N_HEADS_B = 4
HEAD_DIM_B = D_B // N_HEADS_B
DILATED_BRANCHES = ((128, 1), (512, 4), (2048, 16))
WINDOW_MAX = 2048
N_BUCKETS = 32
REL_MAX_DIST = 2048
CONV_W = 4
MLSTM_CHUNK = 128
D_FF = 4 * D_MODEL
N_GATES = 2 * N_HEADS_B
SPLITS = [D_A, 2 * D_A, 3 * D_A, 3 * D_A + D_B, 3 * D_A + 2 * D_B]
EPS = 1e-6
NEG = -1e30

LANES = 128
SUBLANES = 8
VMEM_LIMIT_BYTES = 56 * 1024 * 1024
ROW_TILE = 512

N_PAIRS = D_A // LANES
N_SLABS = 3 * N_PAIRS
HEAD_PAD = 2 * LANES
D_B_PAD = N_HEADS_B * HEAD_PAD
D_IN_PAD = 3 * D_A + 2 * D_B_PAD + LANES
D_O_PAD = D_A + LANES + D_B_PAD
BLK = 128
RES = DILATED_BRANCHES[-1][1]
Q_SUPER = BLK * RES

f32 = jnp.float32
bf16 = jnp.bfloat16


def _rms(xf, g):
    return xf * lax.rsqrt(jnp.mean(xf * xf, axis=-1, keepdims=True) + EPS) * g


def _pad_heads(t):
    t = t.reshape(t.shape[:-1] + (N_HEADS_B, HEAD_DIM_B))
    t = jnp.pad(t, [(0, 0)] * (t.ndim - 1) + [(0, HEAD_PAD - HEAD_DIM_B)])
    return t.reshape(t.shape[:-2] + (D_B_PAD,))


def _unpad_heads(t):
    t = t.reshape(t.shape[:-1] + (N_HEADS_B, HEAD_PAD))[..., :HEAD_DIM_B]
    return t.reshape(t.shape[:-2] + (D_B,))


def _norm_inproj_kernel(x_ref, g_ref, w_ref, slab_ref, kwin_ref, vwin_ref, xb_ref, ob_ref, gt_ref):
    h = _rms(x_ref[...], g_ref[...]).astype(bf16)
    qkv = jnp.dot(h, w_ref[:, 0:3 * D_A], preferred_element_type=f32)
    for j in range(N_SLABS):
        slab_ref[j] = qkv[:, j * LANES:(j + 1) * LANES]
    for p in range(N_PAIRS):
        kwin_ref[p] = qkv[:, (N_PAIRS + p) * LANES:(N_PAIRS + p + 1) * LANES]
        vwin_ref[p] = qkv[:, (2 * N_PAIRS + p) * LANES:(2 * N_PAIRS + p + 1) * LANES]
    o = 3 * D_A
    xb_ref[...] = jnp.dot(h, w_ref[:, o:o + D_B_PAD], preferred_element_type=f32)
    o += D_B_PAD
    ob_ref[...] = jnp.dot(h, w_ref[:, o:o + D_B_PAD], preferred_element_type=f32)
    o += D_B_PAD
    gt_ref[...] = jnp.dot(h, w_ref[:, o:o + LANES], preferred_element_type=f32)


def _norm_inproj(x, g, w_pad_bf16, tm, seq, win):
    m, d = x.shape
    assert m % seq == 0 and seq % tm == 0 and win % tm == 0
    tiles_seq, tiles_win = seq // tm, win // tm
    row = lambda w: pl.BlockSpec((tm, w), lambda i: (i, 0))

    def win_block(i):
        return (0, (i // tiles_seq) * tiles_win + jnp.maximum(i % tiles_seq - (tiles_seq - tiles_win), 0), 0)

    return pl.pallas_call(
        _norm_inproj_kernel,
        out_shape=(
            jax.ShapeDtypeStruct((N_SLABS, m, LANES), f32),
            jax.ShapeDtypeStruct((N_PAIRS, (m // seq) * win, LANES), f32),
            jax.ShapeDtypeStruct((N_PAIRS, (m // seq) * win, LANES), f32),
            jax.ShapeDtypeStruct((m, D_B_PAD), f32),
            jax.ShapeDtypeStruct((m, D_B_PAD), f32),
            jax.ShapeDtypeStruct((m, LANES), f32),
        ),
        grid=(m // tm,),
        in_specs=[
            row(d),
            pl.BlockSpec((1, d), lambda i: (0, 0)),
            pl.BlockSpec((d, D_IN_PAD), lambda i: (0, 0), pipeline_mode=pl.Buffered(1)),
        ],
        out_specs=(
            pl.BlockSpec((N_SLABS, tm, LANES), lambda i: (0, i, 0)),
            pl.BlockSpec((N_PAIRS, tm, LANES), win_block),
            pl.BlockSpec((N_PAIRS, tm, LANES), win_block),
            row(D_B_PAD), row(D_B_PAD), row(LANES),
        ),
        compiler_params=pltpu.CompilerParams(
            dimension_semantics=("arbitrary",),
            vmem_limit_bytes=VMEM_LIMIT_BYTES),
        name="norm_inproj",
    )(x, g, w_pad_bf16)


def _norm_inproj_step_kernel(x_ref, g_ref, wt_ref, w_ref, zt_ref, xb_ref, ob_ref, gt_ref):
    h = _rms(x_ref[...], g_ref[...]).astype(bf16)
    zt_ref[...] = lax.dot_general(wt_ref[...], h, (((1,), (1,)), ((), ())), preferred_element_type=f32)
    o = 3 * D_A
    xb_ref[...] = jnp.dot(h, w_ref[:, o:o + D_B_PAD], preferred_element_type=f32)
    o += D_B_PAD
    ob_ref[...] = jnp.dot(h, w_ref[:, o:o + D_B_PAD], preferred_element_type=f32)
    o += D_B_PAD
    gt_ref[...] = jnp.dot(h, w_ref[:, o:o + LANES], preferred_element_type=f32)


def _norm_inproj_step(x, g, wqkv_t_bf16, w_pad_bf16):
    nb = x.shape[0]
    wide = jax.ShapeDtypeStruct((nb, D_B_PAD), f32)
    return pl.pallas_call(
        _norm_inproj_step_kernel,
        out_shape=(jax.ShapeDtypeStruct((3 * D_A, nb), f32), wide, wide,
                   jax.ShapeDtypeStruct((nb, LANES), f32)),
        compiler_params=pltpu.CompilerParams(vmem_limit_bytes=VMEM_LIMIT_BYTES),
        name="norm_inproj_step",
    )(x, g, wqkv_t_bf16, w_pad_bf16)


def _pad_w_in(w_in):
    zeros = lambda n: jnp.zeros((w_in.shape[0], n), w_in.dtype)
    pieces = [w_in[:, :3 * D_A]]
    for base in (SPLITS[2], SPLITS[3]):
        for h in range(N_HEADS_B):
            pieces += [w_in[:, base + h * HEAD_DIM_B:base + (h + 1) * HEAD_DIM_B], zeros(HEAD_PAD - HEAD_DIM_B)]
    pieces += [w_in[:, SPLITS[4]:], zeros(LANES - N_GATES)]
    return jnp.concatenate(pieces, axis=1)


def _out_ffn_kernel(x_ref, oa_ref, ob_ref, wo_ref, g2_ref, w1_ref, w2_ref, gf_ref, y_ref, *, ff_chunk):
    o = jnp.concatenate([oa_ref[p].astype(bf16) for p in range(N_PAIRS)]
                        + [jnp.zeros((ob_ref.shape[0], LANES), bf16), ob_ref[...].astype(bf16)], axis=1)
    x1 = x_ref[...] + jnp.dot(o, wo_ref[...], preferred_element_type=f32)
    h2 = _rms(x1, g2_ref[...]).astype(bf16)
    acc = x1
    for c in range(D_FF // ff_chunk):
        u = jnp.dot(h2, w1_ref[:, c * ff_chunk:(c + 1) * ff_chunk], preferred_element_type=f32)
        u = jnp.square(jnp.maximum(u, 0.0)).astype(bf16)
        acc = acc + jnp.dot(u, w2_ref[c * ff_chunk:(c + 1) * ff_chunk, :],
                            preferred_element_type=f32)
    y_ref[...] = _rms(acc, gf_ref[...])


def _out_ffn(x, oa, ob, wo, g2, w1, w2, gf, tm, ff_chunk=1024):
    m, d = x.shape
    assert wo.shape[0] == D_O_PAD
    const = lambda i: (0, 0)
    single = dict(pipeline_mode=pl.Buffered(1))
    row = lambda w: pl.BlockSpec((tm, w), lambda i: (i, 0))
    return pl.pallas_call(
        functools.partial(_out_ffn_kernel, ff_chunk=ff_chunk),
        out_shape=jax.ShapeDtypeStruct((m, d), f32),
        grid=(m // tm,),
        in_specs=[
            row(d), pl.BlockSpec((N_PAIRS, tm, LANES), lambda i: (0, i, 0)), row(D_B_PAD),
            pl.BlockSpec((D_O_PAD, d), const, **single),
            pl.BlockSpec((1, d), const),
            pl.BlockSpec((d, D_FF), const, **single),
            pl.BlockSpec((D_FF, d), const, **single),
            pl.BlockSpec((1, d), const),
        ],
        out_specs=row(d),
        compiler_params=pltpu.CompilerParams(
            dimension_semantics=("arbitrary",),
            vmem_limit_bytes=VMEM_LIMIT_BYTES),
        name="outproj_ffn",
    )(x, oa, ob, wo, g2, w1, w2, gf)


def _t5_bucket(dist):
    max_exact = N_BUCKETS // 2
    df = jnp.maximum(dist, 1).astype(jnp.float32)
    large = max_exact + (jnp.log(df / max_exact) / math.log(REL_MAX_DIST / max_exact)
                         * (N_BUCKETS - max_exact)).astype(jnp.int32)
    large = jnp.minimum(large, N_BUCKETS - 1)
    return jnp.where(dist < max_exact, dist, large)


def _prompt_bias_table(rel_bias):
    period = 3 * BLK + 1
    r = jnp.arange(period)
    t = jnp.where(r < 2 * BLK, r, r - period)
    j = BLK - t
    band = (j >= 0) & (j <= BLK)
    tabs = []
    for (_, d) in DILATED_BRANCHES:
        vals = rel_bias[_t5_bucket(jnp.clip(j, 0, None) * d)].astype(f32)
        w = jnp.where(band[:, None], vals, NEG).T
        flat = jnp.tile(w, (1, BLK))[:, :BLK * (period - 1)]
        toe = flat.reshape(N_HEADS_A, BLK, period - 1)[:, :, :2 * BLK]
        sub = RES // d
        if sub > 1:
            pos = jnp.arange(BLK)
            nat = (pos % (BLK // sub)) * sub + pos // (BLK // sub)
            pq = jax.nn.one_hot(nat, BLK, dtype=f32)
            pk = jnp.kron(jnp.eye(2, dtype=f32), pq)
            toe = jnp.einsum('pq,hqk->hpk', pq, toe, precision=lax.Precision.HIGHEST)
            toe = jnp.einsum('hpk,jk->hpj', toe, pk, precision=lax.Precision.HIGHEST)
        tabs.append(toe.reshape(N_PAIRS, 2 * BLK, 2 * BLK))
    return jnp.stack(tabs)


def _attn_prompt_kernel(bias_ref, q_ref, k_ref, v_ref, g_ref, o_ref, q_scr, kv_scr, m_scr, l_scr, acc_scr):
    n = pl.program_id(1)
    step = pl.program_id(0) * pl.num_programs(1) + n
    cur = lax.rem(step, 2)
    prv = 1 - cur

    @pl.when(step == 0)
    def _init():
        kv_scr[1] = jnp.zeros(kv_scr.shape[1:], f32)

    def deinterleave(r, carry):
        rows = pl.ds(pl.multiple_of(r * BLK, BLK), BLK)
        for p in range(N_PAIRS):
            q_scr[p, rows, :] = q_ref.at[p][pl.ds(r, BLK, stride=RES), :]
            kv_scr[cur, p, rows, :] = k_ref.at[p][pl.ds(r, BLK, stride=RES), :]
            kv_scr[cur, N_PAIRS + p, rows, :] = v_ref.at[p][pl.ds(r, BLK, stride=RES), :]
        return carry

    lax.fori_loop(0, RES, deinterleave, 0, unroll=4)

    lane = lax.broadcasted_iota(jnp.int32, (BLK, LANES), 1)
    low = lane < HEAD_DIM_A
    kcol = lax.broadcasted_iota(jnp.int32, (2 * BLK, 2 * BLK), 1)
    first_extra = jnp.where((kcol < BLK) & (n == 0), NEG, 0.0).astype(f32)

    def job(g, d, buf_a, off_a, off_b, res, extra):
        sub = RES // d
        run = BLK // sub

        def starts(off):
            return [pl.multiple_of((c * d + res) * BLK + off, SUBLANES) for c in range(sub)]

        def gather(ref, lead, off):
            return jnp.concatenate([ref[lead + (pl.ds(s, run), slice(None))] for s in starts(off)], axis=0)

        def scatter(ref, p, val):
            for c, s in enumerate(starts(off_b)):
                ref[p, pl.ds(s, run), :] = val[c * run:(c + 1) * run]

        def finish(p, s):
            vv = jnp.concatenate([gather(kv_scr, (buf_a, N_PAIRS + p), off_a),
                                  gather(kv_scr, (cur, N_PAIRS + p), off_b)], axis=0)
            m = jnp.max(s, axis=-1, keepdims=True)
            e = jnp.exp(s - m)
            l = jnp.sum(e, axis=-1, keepdims=True)
            o = jnp.dot(e.astype(bf16), vv.astype(bf16), preferred_element_type=f32)
            o_t = jnp.where(low, o[:BLK], o[BLK:])
            m_t = jnp.where(low, m[:BLK], m[BLK:])
            l_t = jnp.where(low, l[:BLK], l[BLK:])
            if g == 0:
                scatter(acc_scr, p, o_t)
                scatter(m_scr, p, m_t)
                scatter(l_scr, p, l_t)
            else:
                m_o = gather(m_scr, (p,), off_b)
                m_n = jnp.maximum(m_o, m_t)
                al = jnp.exp(m_o - m_n)
                be = jnp.exp(m_t - m_n)
                scatter(acc_scr, p, al * gather(acc_scr, (p,), off_b) + be * o_t)
                scatter(l_scr, p, al * gather(l_scr, (p,), off_b) + be * l_t)
                scatter(m_scr, p, m_n)

        scores = []
        for p in range(N_PAIRS):
            q2 = gather(q_scr, (p,), off_b) * (1.0 / math.sqrt(HEAD_DIM_A))
            qcat = jnp.concatenate([jnp.where(low, q2, 0.0), jnp.where(low, 0.0, q2)], axis=0)
            kk = jnp.concatenate([gather(kv_scr, (buf_a, p), off_a), gather(kv_scr, (cur, p), off_b)], axis=0)
            s = lax.dot_general(qcat.astype(bf16), kk.astype(bf16), (((1,), (1,)), ((), ())),
                                preferred_element_type=f32)
            s = s + bias_ref[g, p]
            if extra is not None:
                s = s + extra
            if d == 1:
                finish(p, s)
            else:
                scores.append(s)
        for p, s in enumerate(scores):
            finish(p, s)

    for g, (_, d) in enumerate(DILATED_BRANCHES):
        nu = Q_SUPER // (BLK * d)
        run = BLK * d // RES
        log_d = d.bit_length() - 1

        def first(res, carry, g=g, d=d, nu=nu, run=run):
            job(g, d, prv, (nu - 1) * run, 0, res, first_extra)
            return carry

        def rest(idx, carry, g=g, d=d, run=run, log_d=log_d):
            u = 1 + lax.shift_right_logical(idx, log_d)
            res = jnp.bitwise_and(idx, d - 1)
            job(g, d, cur, (u - 1) * run, u * run, res, None)
            return carry

        if d == 1:
            first(0, 0)
        else:
            lax.fori_loop(0, d, first, 0, unroll=4)
        if nu > 1:
            lax.fori_loop(0, (nu - 1) * d, rest, 0, unroll=5 if d == 1 else 4)

    def fin(r, carry):
        rows = pl.ds(pl.multiple_of(r * BLK, BLK), BLK)
        os_ = [acc_scr[p, rows, :] / l_scr[p, rows, :] for p in range(N_PAIRS)]
        ss = sum(jnp.sum(o * o, axis=-1, keepdims=True) for o in os_)
        sc = lax.rsqrt(ss * (1.0 / D_A) + EPS)
        for p in range(N_PAIRS):
            o_ref.at[p][pl.ds(r, BLK, stride=RES), :] = os_[p] * sc * g_ref[:, p * LANES:(p + 1) * LANES]
        return carry

    lax.fori_loop(0, RES, fin, 0, unroll=4)


def _attn_prompt(slabs, rel_bias, attn_out_g, batch, seq):
    assert seq % Q_SUPER == 0
    nsb = seq // Q_SUPER
    nbr = len(DILATED_BRANCHES)
    slab = lambda grp: pl.BlockSpec((N_PAIRS, Q_SUPER, LANES), lambda bi, ni: (grp, bi * nsb + ni, 0))
    return pl.pallas_call(
        _attn_prompt_kernel,
        out_shape=jax.ShapeDtypeStruct((N_PAIRS, batch * seq, LANES), f32),
        grid=(batch, nsb),
        in_specs=[
            pl.BlockSpec((nbr, N_PAIRS, 2 * BLK, 2 * BLK), lambda bi, ni: (0, 0, 0, 0),
                         pipeline_mode=pl.Buffered(1)),
            slab(0), slab(1), slab(2),
            pl.BlockSpec((1, D_A), lambda bi, ni: (0, 0)),
        ],
        out_specs=pl.BlockSpec((N_PAIRS, Q_SUPER, LANES), lambda bi, ni: (0, bi * nsb + ni, 0)),
        scratch_shapes=[
            pltpu.VMEM((N_PAIRS, Q_SUPER, LANES), f32),
            pltpu.VMEM((2, 2 * N_PAIRS, Q_SUPER, LANES), f32),
            pltpu.VMEM((N_PAIRS, Q_SUPER, LANES), f32),
            pltpu.VMEM((N_PAIRS, Q_SUPER, LANES), f32),
            pltpu.VMEM((N_PAIRS, Q_SUPER, LANES), f32),
        ],
        compiler_params=pltpu.CompilerParams(
            dimension_semantics=("arbitrary", "arbitrary"),
            vmem_limit_bytes=VMEM_LIMIT_BYTES),
        name="attn_prompt",
    )(_prompt_bias_table(rel_bias), slabs, slabs, slabs, attn_out_g.reshape(1, D_A))


def _log_sigmoid(x):
    return -(jnp.maximum(-x, 0.0) + jnp.log1p(jnp.exp(-jnp.abs(x))))


def _mlstm_prompt_kernel(xb_ref, ob_ref, gt_ref, gbias_ref, cw_ref, cb_ref, wq_ref, wk_ref, mg_ref,
                         skip_ref, bias_ref, cnt_ref, bias0_ref, zt_ref, ga_ref, kt_ref, vt_ref,
                         cs_ref, qts_ref, kts_ref, vts_ref, vsts_ref, rows_ref, obs_ref, cacts_ref, mgs_ref,
                         skips_ref,
                         out_ref, c_out_ref, n_out_ref, m_out_ref, conv_out_ref, ot_ref, nk_ref, nv_ref,
                         cs_out_ref, outs_ref,
                         conv_scr, c_scr, n_scr, m_scr, cq_scr, *, n_vb):
    c_idx = pl.program_id(1)
    L = MLSTM_CHUNK

    seq_idx = pl.program_id(0) * pl.num_programs(1) + c_idx

    @pl.when(seq_idx == 0)
    def _init_decode():
        ot_ref[...] = jnp.zeros_like(ot_ref)

    new_col = _new_token_column(seq_idx, zt_ref)
    _attn_step(seq_idx, new_col, bias_ref, cnt_ref, bias0_ref, ga_ref, kt_ref, vt_ref, ot_ref, nk_ref, nv_ref)
    _mlstm_step_state_body(lax.rem(seq_idx, n_vb), n_vb, cs_ref, qts_ref, kts_ref, vts_ref, vsts_ref, rows_ref,
                           obs_ref, cacts_ref, mgs_ref, skips_ref, cs_out_ref, outs_ref, cq_scr)

    @pl.when(c_idx == 0)
    def _init():
        conv_scr[0:SUBLANES, :] = jnp.zeros((SUBLANES, D_B_PAD), f32)
        c_scr[...] = jnp.zeros_like(c_scr)
        n_scr[...] = jnp.zeros_like(n_scr)
        m_scr[...] = jnp.zeros_like(m_scr)

    x = xb_ref[...]
    conv_scr[SUBLANES:SUBLANES + L, :] = x
    c = cb_ref[...] + x * cw_ref[CONV_W - 1:CONV_W, :]
    for i in range(CONV_W - 1):
        sh = CONV_W - 1 - i
        c = c + conv_scr[SUBLANES - sh:SUBLANES - sh + L, :] * cw_ref[i:i + 1, :]
    conv_scr[0:SUBLANES, :] = x[L - SUBLANES:, :]
    c_act = c * jax.nn.sigmoid(c)

    gts = gt_ref[...] + gbias_ref[...]
    logf = _log_sigmoid(gts)
    row = lax.broadcasted_iota(jnp.int32, (L, L), 0)
    col = lax.broadcasted_iota(jnp.int32, (L, L), 1)
    causal = row >= col
    a_all = jnp.dot(causal.astype(f32), logf, precision=lax.Precision.HIGHEST,
                    preferred_element_type=f32)
    gts_t = gts.T
    a_t = a_all.T

    for h in range(N_HEADS_B):
        sl = slice(h * HEAD_PAD, (h + 1) * HEAD_PAD)
        ch = c_act[:, sl]
        ch_bf = ch.astype(bf16)
        q = jnp.dot(ch_bf, wq_ref[h], preferred_element_type=f32)
        k = jnp.dot(ch_bf, wk_ref[h], preferred_element_type=f32) * (1.0 / math.sqrt(HEAD_DIM_B))
        v = x[:, sl]
        q_bf, k_bf = q.astype(bf16), k.astype(bf16)
        a_col = a_all[:, N_HEADS_B + h:N_HEADS_B + h + 1]
        i_col = gts[:, h:h + 1]
        a_row = a_t[N_HEADS_B + h:N_HEADS_B + h + 1, :]
        i_row = gts_t[h:h + 1, :]
        m_prev = m_scr[h]
        dmat = jnp.where(causal, a_col - a_row + i_row, NEG)
        g = a_col + m_prev
        m_t = jnp.maximum(g, jnp.max(dmat, axis=-1, keepdims=True))
        w_state = jnp.exp(g - m_t)
        s = lax.dot_general(q_bf, k_bf, (((1,), (1,)), ((), ())), preferred_element_type=f32)
        amat = s * jnp.exp(dmat - m_t)
        c_state = c_scr[h]
        inter = lax.dot_general(q_bf, c_state.astype(bf16), (((1,), (1,)), ((), ())),
                                preferred_element_type=f32)
        num = w_state * inter + jnp.dot(amat.astype(bf16), v.astype(bf16),
                                        preferred_element_type=f32)
        n_row = n_scr[h]
        den = (w_state * jnp.sum(q * n_row, axis=-1, keepdims=True)
               + jnp.sum(amat, axis=-1, keepdims=True))
        hh = num / jnp.maximum(jnp.abs(den), jnp.exp(-m_t))
        hn = hh * lax.rsqrt(jnp.sum(hh * hh, axis=-1, keepdims=True) * (1.0 / HEAD_DIM_B) + EPS)
        hn = hn * mg_ref[:, sl]
        out_ref[:, sl] = jax.nn.sigmoid(ob_ref[:, sl]) * (hn + skip_ref[:, sl] * ch)

        b_tot = a_col[L - 1:L, :]
        wl = b_tot - a_col + i_col
        m_new = jnp.maximum(b_tot + m_prev, jnp.max(wl, axis=0, keepdims=True))
        wk = jnp.exp(wl - m_new)
        decay = jnp.exp(b_tot + m_prev - m_new)
        upd = lax.dot_general((wk * v).astype(bf16), k_bf, (((0,), (0,)), ((), ())),
                              preferred_element_type=f32)
        c_scr[h] = decay * c_state + upd
        n_scr[h] = decay * n_row + jnp.sum(wk * k, axis=0, keepdims=True)
        m_scr[h] = m_new

    @pl.when(c_idx == pl.num_programs(1) - 1)
    def _final():
        c_out_ref[0] = c_scr[...]
        n_out_ref[0] = n_scr[...]
        m_out_ref[0] = m_scr[...]
        conv_out_ref[0] = x[L - SUBLANES:, :]


def _mlstm_prompt_attn_step(xb, ob, gates, gbias, cw, cb, wq, wk, mg, skip, batch, seq,
                            zt, cache_kt, cache_vt, rel_bias, attn_out_g,
                            ct, qt, kt, vt, vst, srows, ob3, cact3, mg3, skip3):
    L = MLSTM_CHUNK
    assert seq % L == 0
    nc = seq // L
    nb, p = cache_kt.shape[0], cache_kt.shape[-1]
    assert nb == batch * nc, "one decode sequence per (prompt sequence, chunk) grid step"
    for (w, d) in DILATED_BRANCHES:
        assert w <= p and w // d == BLK
    nh, e = ct.shape[0], ct.shape[1]
    assert (batch * nc) % nh == 0
    n_vb = batch * nc // nh
    assert e % n_vb == 0
    head = lambda shape: pl.BlockSpec((1,) + shape,
                                      lambda bi, ci: ((bi * nc + ci) // n_vb,) + (0,) * len(shape))
    cblk = pl.BlockSpec((1, e // n_vb, e, nb),
                        lambda bi, ci: ((bi * nc + ci) // n_vb, (bi * nc + ci) % n_vb, 0, 0))
    bias, count, bias0 = _step_bias_rows(rel_bias, p)
    win = pl.BlockSpec((1, N_HEADS_A, HEAD_DIM_A, p), lambda bi, ci: (bi * nc + ci, 0, 0, 0))
    buf = jax.ShapeDtypeStruct(cache_kt.shape, f32)
    rows = lambda w: pl.BlockSpec((L, w), lambda bi, ci: (bi * nc + ci, 0))
    const2 = lambda shape: pl.BlockSpec(shape, lambda bi, ci: (0, 0))
    const3 = lambda shape: pl.BlockSpec(shape, lambda bi, ci: (0, 0, 0))
    state = lambda shape: pl.BlockSpec((1,) + shape, lambda bi, ci: (bi,) + (0,) * len(shape))
    return pl.pallas_call(
        functools.partial(_mlstm_prompt_kernel, n_vb=n_vb),
        out_shape=(
            jax.ShapeDtypeStruct((batch * seq, D_B_PAD), f32),
            jax.ShapeDtypeStruct((batch, N_HEADS_B, HEAD_PAD, HEAD_PAD), f32),
            jax.ShapeDtypeStruct((batch, N_HEADS_B, 1, HEAD_PAD), f32),
            jax.ShapeDtypeStruct((batch, N_HEADS_B, 1, 1), f32),
            jax.ShapeDtypeStruct((batch, SUBLANES, D_B_PAD), f32),
            jax.ShapeDtypeStruct((D_A, nb), f32), buf, buf,
            jax.ShapeDtypeStruct(ct.shape, f32), jax.ShapeDtypeStruct((nh, nb, HEAD_PAD), f32),
        ),
        grid=(batch, nc),
        in_specs=[rows(D_B_PAD), rows(D_B_PAD), rows(LANES), const2((1, LANES)),
                  const2((CONV_W, D_B_PAD)), const2((1, D_B_PAD)),
                  const3((N_HEADS_B, HEAD_PAD, HEAD_PAD)), const3((N_HEADS_B, HEAD_PAD, HEAD_PAD)),
                  const2((1, D_B_PAD)), const2((1, D_B_PAD)),
                  const3((N_HEADS_A, 1, p)), const2((1, p)), const3((N_HEADS_A, 1, 1)),
                  const2((3 * D_A, nb)), const2((D_A, 1)), win, win,
                  cblk, head((e, nb)), head((e, nb)), head((e, nb)), head((e, nb)), head((N_ROWS, nb)),
                  head((nb, HEAD_PAD)), head((nb, HEAD_PAD)), head((1, HEAD_PAD)), head((1, HEAD_PAD))],
        out_specs=(rows(D_B_PAD), state((N_HEADS_B, HEAD_PAD, HEAD_PAD)),
                   state((N_HEADS_B, 1, HEAD_PAD)), state((N_HEADS_B, 1, 1)),
                   state((SUBLANES, D_B_PAD)),
                   const2((D_A, nb)), win, win,
                   cblk, head((nb, HEAD_PAD))),
        scratch_shapes=[
            pltpu.VMEM((SUBLANES + L, D_B_PAD), f32),
            pltpu.VMEM((N_HEADS_B, HEAD_PAD, HEAD_PAD), f32),
            pltpu.VMEM((N_HEADS_B, 1, HEAD_PAD), f32),
            pltpu.VMEM((N_HEADS_B, 1, 1), f32),
            pltpu.VMEM((e, nb), f32),
        ],
        compiler_params=pltpu.CompilerParams(
            dimension_semantics=("arbitrary", "arbitrary"),
            vmem_limit_bytes=VMEM_LIMIT_BYTES),
        name="mlstm_prompt_attn_step",
    )(xb, ob, gates, gbias, cw, cb, wq, wk, mg, skip,
      bias, count, bias0, zt, attn_out_g.reshape(D_A, 1), cache_kt, cache_vt,
      ct, qt, kt, vt, vst, srows, ob3, cact3, mg3, skip3)


def _step_bias_rows(rel_bias, p):
    dist = p - jnp.arange(p)
    full = rel_bias[_t5_bucket(dist)].astype(f32).T
    count = sum(((dist % d == 0) & (dist <= w)).astype(f32) for (w, d) in DILATED_BRANCHES)
    zero = rel_bias[_t5_bucket(jnp.zeros((1,), jnp.int32))].astype(f32).T
    return jnp.where(count[None, :] > 0, full, NEG)[:, None, :], count[None, :], zero[:, None, :]


def _new_token_column(b, zt_ref):
    zt = zt_ref[...]
    is_b = lax.broadcasted_iota(jnp.int32, zt.shape, 1) == b
    return jnp.sum(jnp.where(is_b, zt, 0.0), axis=1, keepdims=True)


def _attn_step(b, col, bias_ref, cnt_ref, bias0_ref, g_ref, kt_ref, vt_ref, ot_ref, nk_ref, nv_ref):
    nbr = len(DILATED_BRANCHES)
    scale = 1.0 / math.sqrt(HEAD_DIM_A)
    cnt = cnt_ref[...]
    weights = []
    for h in range(N_HEADS_A):
        q_col = col[h * HEAD_DIM_A:(h + 1) * HEAD_DIM_A] * scale
        kn_col = col[D_A + h * HEAD_DIM_A:D_A + (h + 1) * HEAD_DIM_A]
        kt = kt_ref[0, h]
        s = jnp.sum(kt * q_col, axis=0, keepdims=True)
        s_new = jnp.sum(kn_col * q_col, axis=0, keepdims=True)
        sb = s + bias_ref[h]
        s0 = s_new + bias0_ref[h]
        m = jnp.maximum(jnp.max(sb, axis=1, keepdims=True), s0)
        pw = cnt * jnp.exp(sb - m)
        e0 = nbr * jnp.exp(s0 - m)
        weights.append((pw, e0, jnp.sum(pw, axis=1, keepdims=True) + e0))
    _window_roll(col, kt_ref, vt_ref, nk_ref, nv_ref)
    o_cols = []
    for h, (pw, e0, l) in enumerate(weights):
        vn_col = col[2 * D_A + h * HEAD_DIM_A:2 * D_A + (h + 1) * HEAD_DIM_A]
        vt = vt_ref[0, h]
        o_cols.append((jnp.sum(vt * pw, axis=1, keepdims=True) + e0 * vn_col) / l)
    o = jnp.concatenate(o_cols, axis=0)
    ssq = jnp.sum(o * o, axis=0, keepdims=True)
    o = o * lax.rsqrt(ssq * (1.0 / D_A) + EPS) * g_ref[...]
    sel = lax.broadcasted_iota(jnp.int32, ot_ref.shape, 1) == b
    ot_ref[...] = jnp.where(sel, o, ot_ref[...])


def _window_roll(col, kt_ref, vt_ref, nk_ref, nv_ref):
    p = kt_ref.shape[-1]
    is_last = lax.broadcasted_iota(jnp.int32, (HEAD_DIM_A, p), 1) == p - 1
    for base, src, dst in ((D_A, kt_ref, nk_ref), (2 * D_A, vt_ref, nv_ref)):
        for h in range(N_HEADS_A):
            new_col = col[base + h * HEAD_DIM_A:base + (h + 1) * HEAD_DIM_A]
            dst[0, h] = jnp.where(is_last, new_col, pltpu.roll(src[0, h], p - 1, axis=1))


N_ROWS = 3
ROW_W, ROW_A, ROW_R = range(N_ROWS)


def _mlstm_step_pre_kernel(xb_ref, gt_ref, gbias_ref, sc_ref, cw_ref, cb_ref, wq_ref, wk_ref, nt_ref, mt_ref,
                           cact_out, qt_out, kt_out, vt_out, vst_out, rows_out, nt_out, mt_out):
    E = HEAD_DIM_B
    x = xb_ref[...]
    c = cb_ref[...] + x * cw_ref[CONV_W - 1:CONV_W, :]
    for t in range(CONV_W - 1):
        c = c + sc_ref[t] * cw_ref[t:t + 1, :]
    c_act = c * jax.nn.sigmoid(c)
    cact_out[...] = c_act
    gts_t = (gt_ref[...] + gbias_ref[...]).T
    logf_t = _log_sigmoid(gts_t)
    for h in range(N_HEADS_B):
        sl = slice(h * HEAD_PAD, (h + 1) * HEAD_PAD)
        ch_bf = c_act[:, sl].astype(bf16)
        q = jnp.dot(ch_bf, wq_ref[h], preferred_element_type=f32)
        k = jnp.dot(ch_bf, wk_ref[h], preferred_element_type=f32) * (1.0 / math.sqrt(HEAD_DIM_B))
        qt = q.T[:E]
        kt = k.T[:E]
        vt = x[:, sl].T[:E]
        i_pre = gts_t[h:h + 1]
        a = logf_t[N_HEADS_B + h:N_HEADS_B + h + 1]
        m_old = mt_ref[h:h + 1]
        m_t = jnp.maximum(a + m_old, i_pre)
        w_state = jnp.exp(a + m_old - m_t)
        w_in = jnp.exp(i_pre - m_t)
        amat = jnp.sum(qt * kt, axis=0, keepdims=True) * w_in
        n_old = nt_ref[h]
        den = w_state * jnp.sum(n_old * qt, axis=0, keepdims=True) + amat
        nt_out[h] = w_state * n_old + w_in * kt
        mt_out[h:h + 1] = m_t
        qt_out[h] = qt
        kt_out[h] = kt
        vt_out[h] = vt
        vst_out[h] = w_in * vt
        rows_out[h, ROW_W:ROW_W + 1] = w_state
        rows_out[h, ROW_A:ROW_A + 1] = amat
        rows_out[h, ROW_R:ROW_R + 1] = 1.0 / jnp.maximum(jnp.abs(den), jnp.exp(-m_t))


def _mlstm_step_pre(xb, gts, gbias, sc_t, cw, cb, wq, wk, nt, mt):
    nb = xb.shape[0]
    hd = jax.ShapeDtypeStruct((N_HEADS_B, HEAD_DIM_B, nb), f32)
    return pl.pallas_call(
        _mlstm_step_pre_kernel,
        out_shape=(jax.ShapeDtypeStruct((nb, D_B_PAD), f32), hd, hd, hd, hd,
                   jax.ShapeDtypeStruct((N_HEADS_B, N_ROWS, nb), f32), hd,
                   jax.ShapeDtypeStruct((N_HEADS_B, nb), f32)),
        compiler_params=pltpu.CompilerParams(vmem_limit_bytes=VMEM_LIMIT_BYTES),
        name="mlstm_step_pre",
    )(xb, gts, gbias, sc_t, cw, cb, wq, wk, nt, mt)


def _mlstm_step_state_body(vb, n_vb, c_ref, qt_ref, kt_ref, vt_ref, vst_ref, rows_ref, ob_ref, cact_ref,
                           mg_ref, skip_ref, c_out, out_ref, cq_scr):
    E = HEAD_DIM_B
    n_rows = c_ref.shape[1]
    qt = qt_ref[0]
    kt = kt_ref[0]
    w_state = rows_ref[0, ROW_W:ROW_W + 1]

    def per_row(v, carry):
        c_old = c_ref[0, v]
        vs = vst_ref[0, pl.ds(vb * n_rows + v, 1), :]
        c_out[0, v] = w_state * c_old + vs * kt
        cq_scr[pl.ds(vb * n_rows + v, 1), :] = jnp.sum(c_old * qt, axis=0, keepdims=True)
        return carry

    lax.fori_loop(0, n_rows, per_row, 0)

    @pl.when(vb == n_vb - 1)
    def _finish_head():
        ht = (w_state * cq_scr[...] + rows_ref[0, ROW_A:ROW_A + 1] * vt_ref[0]) * rows_ref[0, ROW_R:ROW_R + 1]
        hh = jnp.concatenate([ht, jnp.zeros((HEAD_PAD - E, ht.shape[1]), f32)], axis=0).T
        hn = hh * lax.rsqrt(jnp.sum(hh * hh, axis=-1, keepdims=True) * (1.0 / E) + EPS) * mg_ref[0]
        out_ref[0] = jax.nn.sigmoid(ob_ref[0]) * (hn + skip_ref[0] * cact_ref[0])


def kernel(x_prompt, x_sample, cache_win_k, cache_win_v, state_conv, state_C, state_n, state_m,
           rel_bias, norm1_g, w_in, gate_bias, conv_w, conv_b, wq_head, wk_head, attn_out_g,
           mh_norm_g, skip, w_out, norm2_g, w_ff1, w_ff2, final_g):
    Bp, Sp, _ = x_prompt.shape
    Bs, Ss, _ = x_sample.shape
    assert Ss == 1
    g1 = norm1_g[0].reshape(1, D_MODEL)
    g2 = norm2_g[0].reshape(1, D_MODEL)
    gf = final_g.reshape(1, D_MODEL)
    w_in_pad = _pad_w_in(w_in[0].astype(bf16))
    wo_bf = w_out[0].astype(bf16)
    wo_rows = [wo_bf[:D_A], jnp.zeros((LANES, D_MODEL), bf16)]
    for h in range(N_HEADS_B):
        wo_rows += [wo_bf[D_A + h * HEAD_DIM_B:D_A + (h + 1) * HEAD_DIM_B],
                    jnp.zeros((HEAD_PAD - HEAD_DIM_B, D_MODEL), bf16)]
    wo = jnp.concatenate(wo_rows, axis=0)
    w1 = w_ff1[0].astype(bf16)
    w2 = w_ff2[0].astype(bf16)
    hpad = HEAD_PAD - HEAD_DIM_B
    wq_p = jnp.pad(wq_head[0], ((0, 0), (0, hpad), (0, hpad))).astype(bf16)
    wk_p = jnp.pad(wk_head[0], ((0, 0), (0, hpad), (0, hpad))).astype(bf16)
    gbias = jnp.pad(gate_bias[0], (0, LANES - N_GATES)).reshape(1, LANES)
    cw_p = _pad_heads(conv_w[0])
    cb_p = _pad_heads(conv_b[0]).reshape(1, D_B_PAD)
    mg_p = _pad_heads(mh_norm_g[0]).reshape(1, D_B_PAD)
    skip_p = _pad_heads(skip[0]).reshape(1, D_B_PAD)

    xp2 = x_prompt.reshape(Bp * Sp, D_MODEL)
    P = min(WINDOW_MAX, Sp)
    slabs, k_win, v_win, xb, ob, gts = _norm_inproj(xp2, g1, w_in_pad, tm=ROW_TILE, seq=Sp, win=P)
    out_a = _attn_prompt(slabs, rel_bias, attn_out_g[0], Bp, Sp)
    xs2 = x_sample.reshape(Bs, D_MODEL)
    zt_s, xb_s, ob_s, gts_s = _norm_inproj_step(xs2, g1, w_in[0][:, :3 * D_A].T.astype(bf16), w_in_pad)
    sc_t = _pad_heads(state_conv[0].transpose(1, 0, 2))
    cact_s, qt_s, kt_s, vt_s, vst_s, rows_s, nt_s, mt_s = _mlstm_step_pre(
        xb_s, gts_s, gbias, sc_t, cw_p, cb_p, wq_p, wk_p,
        state_n[0].transpose(1, 2, 0), state_m[0].T)
    heads3 = lambda t: t.reshape(t.shape[0], N_HEADS_B, HEAD_PAD).transpose(1, 0, 2)
    out_b, c_p, n_p, m_p, tail_p, oat_s, nkt, nvt, ct_s, outb3 = _mlstm_prompt_attn_step(
        xb, ob, gts, gbias, cw_p, cb_p, wq_p, wk_p, mg_p, skip_p, Bp, Sp,
        zt_s, cache_win_k[0].transpose(0, 2, 3, 1), cache_win_v[0].transpose(0, 2, 3, 1), rel_bias,
        attn_out_g[0],
        state_C[0].transpose(1, 2, 3, 0), qt_s, kt_s, vt_s, vst_s, rows_s,
        heads3(ob_s), heads3(cact_s), heads3(mg_p), heads3(skip_p))
    y_p = _out_ffn(xp2, out_a, out_b, wo, g2, w1, w2, gf, tm=ROW_TILE).reshape(Bp, Sp, D_MODEL)
    win5 = lambda t: t.reshape(N_PAIRS, Bp, P, LANES).transpose(1, 2, 0, 3).reshape(
        1, Bp, P, N_HEADS_A, HEAD_DIM_A)
    st_p = (win5(k_win), win5(v_win), _unpad_heads(tail_p[:, SUBLANES - (CONV_W - 1):])[None],
            c_p[:, :, :HEAD_DIM_B, :HEAD_DIM_B][None], n_p[:, :, 0, :HEAD_DIM_B][None],
            m_p[:, :, 0, 0][None])

    oa_s = oat_s.reshape(N_PAIRS, LANES, Bs).transpose(0, 2, 1)
    new_k = nkt.transpose(0, 3, 1, 2)[None]
    new_v = nvt.transpose(0, 3, 1, 2)[None]
    outb_s = outb3.transpose(1, 0, 2).reshape(Bs, D_B_PAD)
    y_s = _out_ffn(xs2, oa_s, outb_s, wo, g2, w1, w2, gf, tm=Bs).reshape(Bs, Ss, D_MODEL)
    new_conv = jnp.concatenate([state_conv[0][:, 1:], _unpad_heads(xb_s)[:, None]], axis=1)
    st_s = (new_k, new_v, new_conv[None], ct_s.transpose(3, 0, 1, 2)[None],
            nt_s.transpose(2, 0, 1)[None], mt_s.T[None])
    return (y_p, y_s) + st_p + st_s
```

```python
import functools
import math

import jax
import jax.numpy as jnp
from jax import lax
from jax.experimental import pallas as pl
from jax.experimental.pallas import tpu as pltpu

D_MODEL = 1024
HEAD_DIM_A = 64
N_HEADS_A = 6
D_A = N_HEADS_A * HEAD_DIM_A
D_B = D_MODEL - D_A
N_HEADS_B = 4
HEAD_DIM_B = D_B // N_HEADS_B
DILATED_BRANCHES = ((128, 1), (512, 4), (2048, 16))
WINDOW_MAX = 2048
N_BUCKETS = 32
REL_MAX_DIST = 2048
CONV_W = 4
MLSTM_CHUNK = 128
D_FF = 4 * D_MODEL
N_GATES = 2 * N_HEADS_B
SPLITS = [D_A, 2 * D_A, 3 * D_A, 3 * D_A + D_B, 3 * D_A + 2 * D_B]
EPS = 1e-6
NEG = -1e30

LANES = 128
SUBLANES = 8
VMEM_LIMIT_BYTES = 56 * 1024 * 1024
ROW_TILE = 512

N_PAIRS = D_A // LANES
N_SLABS = 3 * N_PAIRS
HEAD_PAD = 2 * LANES
D_B_PAD = N_HEADS_B * HEAD_PAD
D_IN_PAD = 3 * D_A + 2 * D_B_PAD + LANES
D_O_PAD = D_A + LANES + D_B_PAD
BLK = 128
RES = DILATED_BRANCHES[-1][1]
Q_SUPER = BLK * RES

f32 = jnp.float32
bf16 = jnp.bfloat16


def _rms(xf, g):
    return xf * lax.rsqrt(jnp.mean(xf * xf, axis=-1, keepdims=True) + EPS) * g


def _pad_heads(t):
    t = t.reshape(t.shape[:-1] + (N_HEADS_B, HEAD_DIM_B))
    t = jnp.pad(t, [(0, 0)] * (t.ndim - 1) + [(0, HEAD_PAD - HEAD_DIM_B)])
    return t.reshape(t.shape[:-2] + (D_B_PAD,))


def _unpad_heads(t):
    t = t.reshape(t.shape[:-1] + (N_HEADS_B, HEAD_PAD))[..., :HEAD_DIM_B]
    return t.reshape(t.shape[:-2] + (D_B,))


def _norm_inproj_kernel(x_ref, g_ref, w_ref, slab_ref, kwin_ref, vwin_ref, xb_ref, ob_ref, gt_ref):
    h = _rms(x_ref[...], g_ref[...]).astype(bf16)
    qkv = jnp.dot(h, w_ref[:, 0:3 * D_A], preferred_element_type=f32)
    for j in range(N_SLABS):
        slab_ref[j] = qkv[:, j * LANES:(j + 1) * LANES]
    for p in range(N_PAIRS):
        kwin_ref[p] = qkv[:, (N_PAIRS + p) * LANES:(N_PAIRS + p + 1) * LANES]
        vwin_ref[p] = qkv[:, (2 * N_PAIRS + p) * LANES:(2 * N_PAIRS + p + 1) * LANES]
    o = 3 * D_A
    xb_ref[...] = jnp.dot(h, w_ref[:, o:o + D_B_PAD], preferred_element_type=f32)
    o += D_B_PAD
    ob_ref[...] = jnp.dot(h, w_ref[:, o:o + D_B_PAD], preferred_element_type=f32)
    o += D_B_PAD
    gt_ref[...] = jnp.dot(h, w_ref[:, o:o + LANES], preferred_element_type=f32)


def _norm_inproj(x, g, w_pad_bf16, tm, seq, win):
    m, d = x.shape
    assert m % seq == 0 and seq % tm == 0 and win % tm == 0
    tiles_seq, tiles_win = seq // tm, win // tm
    row = lambda w: pl.BlockSpec((tm, w), lambda i: (i, 0))

    def win_block(i):
        return (0, (i // tiles_seq) * tiles_win + jnp.maximum(i % tiles_seq - (tiles_seq - tiles_win), 0), 0)

    return pl.pallas_call(
        _norm_inproj_kernel,
        out_shape=(
            jax.ShapeDtypeStruct((N_SLABS, m, LANES), f32),
            jax.ShapeDtypeStruct((N_PAIRS, (m // seq) * win, LANES), f32),
            jax.ShapeDtypeStruct((N_PAIRS, (m // seq) * win, LANES), f32),
            jax.ShapeDtypeStruct((m, D_B_PAD), f32),
            jax.ShapeDtypeStruct((m, D_B_PAD), f32),
            jax.ShapeDtypeStruct((m, LANES), f32),
        ),
        grid=(m // tm,),
        in_specs=[
            row(d),
            pl.BlockSpec((1, d), lambda i: (0, 0)),
            pl.BlockSpec((d, D_IN_PAD), lambda i: (0, 0), pipeline_mode=pl.Buffered(1)),
        ],
        out_specs=(
            pl.BlockSpec((N_SLABS, tm, LANES), lambda i: (0, i, 0)),
            pl.BlockSpec((N_PAIRS, tm, LANES), win_block),
            pl.BlockSpec((N_PAIRS, tm, LANES), win_block),
            row(D_B_PAD), row(D_B_PAD), row(LANES),
        ),
        compiler_params=pltpu.CompilerParams(
            dimension_semantics=("arbitrary",),
            vmem_limit_bytes=VMEM_LIMIT_BYTES),
        name="norm_inproj",
    )(x, g, w_pad_bf16)


def _norm_inproj_step_kernel(x_ref, g_ref, wt_ref, w_ref, zt_ref, xb_ref, ob_ref, gt_ref):
    h = _rms(x_ref[...], g_ref[...]).astype(bf16)
    zt_ref[...] = lax.dot_general(wt_ref[...], h, (((1,), (1,)), ((), ())), preferred_element_type=f32)
    o = 3 * D_A
    xb_ref[...] = jnp.dot(h, w_ref[:, o:o + D_B_PAD], preferred_element_type=f32)
    o += D_B_PAD
    ob_ref[...] = jnp.dot(h, w_ref[:, o:o + D_B_PAD], preferred_element_type=f32)
    o += D_B_PAD
    gt_ref[...] = jnp.dot(h, w_ref[:, o:o + LANES], preferred_element_type=f32)


def _norm_inproj_step(x, g, wqkv_t_bf16, w_pad_bf16):
    nb = x.shape[0]
    wide = jax.ShapeDtypeStruct((nb, D_B_PAD), f32)
    return pl.pallas_call(
        _norm_inproj_step_kernel,
        out_shape=(jax.ShapeDtypeStruct((3 * D_A, nb), f32), wide, wide,
                   jax.ShapeDtypeStruct((nb, LANES), f32)),
        compiler_params=pltpu.CompilerParams(vmem_limit_bytes=VMEM_LIMIT_BYTES),
        name="norm_inproj_step",
    )(x, g, wqkv_t_bf16, w_pad_bf16)


def _pad_w_in(w_in):
    zeros = lambda n: jnp.zeros((w_in.shape[0], n), w_in.dtype)
    pieces = [w_in[:, :3 * D_A]]
    for base in (SPLITS[2], SPLITS[3]):
        for h in range(N_HEADS_B):
            pieces += [w_in[:, base + h * HEAD_DIM_B:base + (h + 1) * HEAD_DIM_B], zeros(HEAD_PAD - HEAD_DIM_B)]
    pieces += [w_in[:, SPLITS[4]:], zeros(LANES - N_GATES)]
    return jnp.concatenate(pieces, axis=1)


def _out_ffn_kernel(x_ref, oa_ref, ob_ref, wo_ref, g2_ref, w1_ref, w2_ref, gf_ref, y_ref, *, ff_chunk):
    o = jnp.concatenate([oa_ref[p].astype(bf16) for p in range(N_PAIRS)]
                        + [jnp.zeros((ob_ref.shape[0], LANES), bf16), ob_ref[...].astype(bf16)], axis=1)
    x1 = x_ref[...] + jnp.dot(o, wo_ref[...], preferred_element_type=f32)
    h2 = _rms(x1, g2_ref[...]).astype(bf16)
    acc = x1
    for c in range(D_FF // ff_chunk):
        u = jnp.dot(h2, w1_ref[:, c * ff_chunk:(c + 1) * ff_chunk], preferred_element_type=f32)
        u = jnp.square(jnp.maximum(u, 0.0)).astype(bf16)
        acc = acc + jnp.dot(u, w2_ref[c * ff_chunk:(c + 1) * ff_chunk, :],
                            preferred_element_type=f32)
    y_ref[...] = _rms(acc, gf_ref[...])


def _out_ffn(x, oa, ob, wo, g2, w1, w2, gf, tm, ff_chunk=1024):
    m, d = x.shape
    assert wo.shape[0] == D_O_PAD
    const = lambda i: (0, 0)
    single = dict(pipeline_mode=pl.Buffered(1))
    row = lambda w: pl.BlockSpec((tm, w), lambda i: (i, 0))
    return pl.pallas_call(
        functools.partial(_out_ffn_kernel, ff_chunk=ff_chunk),
        out_shape=jax.ShapeDtypeStruct((m, d), f32),
        grid=(m // tm,),
        in_specs=[
            row(d), pl.BlockSpec((N_PAIRS, tm, LANES), lambda i: (0, i, 0)), row(D_B_PAD),
            pl.BlockSpec((D_O_PAD, d), const, **single),
            pl.BlockSpec((1, d), const),
            pl.BlockSpec((d, D_FF), const, **single),
            pl.BlockSpec((D_FF, d), const, **single),
            pl.BlockSpec((1, d), const),
        ],
        out_specs=row(d),
        compiler_params=pltpu.CompilerParams(
            dimension_semantics=("arbitrary",),
            vmem_limit_bytes=VMEM_LIMIT_BYTES),
        name="outproj_ffn",
    )(x, oa, ob, wo, g2, w1, w2, gf)


def _t5_bucket(dist):
    max_exact = N_BUCKETS // 2
    df = jnp.maximum(dist, 1).astype(jnp.float32)
    large = max_exact + (jnp.log(df / max_exact) / math.log(REL_MAX_DIST / max_exact)
                         * (N_BUCKETS - max_exact)).astype(jnp.int32)
    large = jnp.minimum(large, N_BUCKETS - 1)
    return jnp.where(dist < max_exact, dist, large)


def _prompt_bias_table(rel_bias):
    period = 3 * BLK + 1
    r = jnp.arange(period)
    t = jnp.where(r < 2 * BLK, r, r - period)
    j = BLK - t
    band = (j >= 0) & (j <= BLK)
    tabs = []
    for (_, d) in DILATED_BRANCHES:
        vals = rel_bias[_t5_bucket(jnp.clip(j, 0, None) * d)].astype(f32)
        w = jnp.where(band[:, None], vals, NEG).T
        flat = jnp.tile(w, (1, BLK))[:, :BLK * (period - 1)]
        toe = flat.reshape(N_HEADS_A, BLK, period - 1)[:, :, :2 * BLK]
        sub = RES // d
        if sub > 1:
            pos = jnp.arange(BLK)
            nat = (pos % (BLK // sub)) * sub + pos // (BLK // sub)
            pq = jax.nn.one_hot(nat, BLK, dtype=f32)
            pk = jnp.kron(jnp.eye(2, dtype=f32), pq)
            toe = jnp.einsum('pq,hqk->hpk', pq, toe, precision=lax.Precision.HIGHEST)
            toe = jnp.einsum('hpk,jk->hpj', toe, pk, precision=lax.Precision.HIGHEST)
        tabs.append(toe.reshape(N_PAIRS, 2 * BLK, 2 * BLK))
    return jnp.stack(tabs)


def _attn_prompt_kernel(bias_ref, q_ref, k_ref, v_ref, g_ref, o_ref, q_scr, kv_scr, m_scr, l_scr, acc_scr):
    n = pl.program_id(1)
    step = pl.program_id(0) * pl.num_programs(1) + n
    cur = lax.rem(step, 2)
    prv = 1 - cur

    @pl.when(step == 0)
    def _init():
        kv_scr[1] = jnp.zeros(kv_scr.shape[1:], f32)

    def deinterleave(r, carry):
        rows = pl.ds(pl.multiple_of(r * BLK, BLK), BLK)
        for p in range(N_PAIRS):
            q_scr[p, rows, :] = q_ref.at[p][pl.ds(r, BLK, stride=RES), :]
            kv_scr[cur, p, rows, :] = k_ref.at[p][pl.ds(r, BLK, stride=RES), :]
            kv_scr[cur, N_PAIRS + p, rows, :] = v_ref.at[p][pl.ds(r, BLK, stride=RES), :]
        return carry

    lax.fori_loop(0, RES, deinterleave, 0, unroll=4)

    lane = lax.broadcasted_iota(jnp.int32, (BLK, LANES), 1)
    low = lane < HEAD_DIM_A
    kcol = lax.broadcasted_iota(jnp.int32, (2 * BLK, 2 * BLK), 1)
    first_extra = jnp.where((kcol < BLK) & (n == 0), NEG, 0.0).astype(f32)

    def job(g, d, buf_a, off_a, off_b, res, extra):
        sub = RES // d
        run = BLK // sub

        def starts(off):
            return [pl.multiple_of((c * d + res) * BLK + off, SUBLANES) for c in range(sub)]

        def gather(ref, lead, off):
            return jnp.concatenate([ref[lead + (pl.ds(s, run), slice(None))] for s in starts(off)], axis=0)

        def scatter(ref, p, val):
            for c, s in enumerate(starts(off_b)):
                ref[p, pl.ds(s, run), :] = val[c * run:(c + 1) * run]

        def finish(p, s):
            vv = jnp.concatenate([gather(kv_scr, (buf_a, N_PAIRS + p), off_a),
                                  gather(kv_scr, (cur, N_PAIRS + p), off_b)], axis=0)
            m = jnp.max(s, axis=-1, keepdims=True)
            e = jnp.exp(s - m)
            l = jnp.sum(e, axis=-1, keepdims=True)
            o = jnp.dot(e.astype(bf16), vv.astype(bf16), preferred_element_type=f32)
            o_t = jnp.where(low, o[:BLK], o[BLK:])
            m_t = jnp.where(low, m[:BLK], m[BLK:])
            l_t = jnp.where(low, l[:BLK], l[BLK:])
            if g == 0:
                scatter(acc_scr, p, o_t)
                scatter(m_scr, p, m_t)
                scatter(l_scr, p, l_t)
            else:
                m_o = gather(m_scr, (p,), off_b)
                m_n = jnp.maximum(m_o, m_t)
                al = jnp.exp(m_o - m_n)
                be = jnp.exp(m_t - m_n)
                scatter(acc_scr, p, al * gather(acc_scr, (p,), off_b) + be * o_t)
                scatter(l_scr, p, al * gather(l_scr, (p,), off_b) + be * l_t)
                scatter(m_scr, p, m_n)

        scores = []
        for p in range(N_PAIRS):
            q2 = gather(q_scr, (p,), off_b) * (1.0 / math.sqrt(HEAD_DIM_A))
            qcat = jnp.concatenate([jnp.where(low, q2, 0.0), jnp.where(low, 0.0, q2)], axis=0)
            kk = jnp.concatenate([gather(kv_scr, (buf_a, p), off_a), gather(kv_scr, (cur, p), off_b)], axis=0)
            s = lax.dot_general(qcat.astype(bf16), kk.astype(bf16), (((1,), (1,)), ((), ())),
                                preferred_element_type=f32)
            s = s + bias_ref[g, p]
            if extra is not None:
                s = s + extra
            if d == 1:
                finish(p, s)
            else:
                scores.append(s)
        for p, s in enumerate(scores):
            finish(p, s)

    for g, (_, d) in enumerate(DILATED_BRANCHES):
        nu = Q_SUPER // (BLK * d)
        run = BLK * d // RES
        log_d = d.bit_length() - 1

        def first(res, carry, g=g, d=d, nu=nu, run=run):
            job(g, d, prv, (nu - 1) * run, 0, res, first_extra)
            return carry

        def rest(idx, carry, g=g, d=d, run=run, log_d=log_d):
            u = 1 + lax.shift_right_logical(idx, log_d)
            res = jnp.bitwise_and(idx, d - 1)
            job(g, d, cur, (u - 1) * run, u * run, res, None)
            return carry

        if d == 1:
            first(0, 0)
        else:
            lax.fori_loop(0, d, first, 0, unroll=4)
        if nu > 1:
            lax.fori_loop(0, (nu - 1) * d, rest, 0, unroll=5 if d == 1 else 4)

    def fin(r, carry):
        rows = pl.ds(pl.multiple_of(r * BLK, BLK), BLK)
        os_ = [acc_scr[p, rows, :] / l_scr[p, rows, :] for p in range(N_PAIRS)]
        ss = sum(jnp.sum(o * o, axis=-1, keepdims=True) for o in os_)
        sc = lax.rsqrt(ss * (1.0 / D_A) + EPS)
        for p in range(N_PAIRS):
            o_ref.at[p][pl.ds(r, BLK, stride=RES), :] = os_[p] * sc * g_ref[:, p * LANES:(p + 1) * LANES]
        return carry

    lax.fori_loop(0, RES, fin, 0, unroll=4)


def _attn_prompt(slabs, rel_bias, attn_out_g, batch, seq):
    assert seq % Q_SUPER == 0
    nsb = seq // Q_SUPER
    nbr = len(DILATED_BRANCHES)
    slab = lambda grp: pl.BlockSpec((N_PAIRS, Q_SUPER, LANES), lambda bi, ni: (grp, bi * nsb + ni, 0))
    return pl.pallas_call(
        _attn_prompt_kernel,
        out_shape=jax.ShapeDtypeStruct((N_PAIRS, batch * seq, LANES), f32),
        grid=(batch, nsb),
        in_specs=[
            pl.BlockSpec((nbr, N_PAIRS, 2 * BLK, 2 * BLK), lambda bi, ni: (0, 0, 0, 0),
                         pipeline_mode=pl.Buffered(1)),
            slab(0), slab(1), slab(2),
            pl.BlockSpec((1, D_A), lambda bi, ni: (0, 0)),
        ],
        out_specs=pl.BlockSpec((N_PAIRS, Q_SUPER, LANES), lambda bi, ni: (0, bi * nsb + ni, 0)),
        scratch_shapes=[
            pltpu.VMEM((N_PAIRS, Q_SUPER, LANES), f32),
            pltpu.VMEM((2, 2 * N_PAIRS, Q_SUPER, LANES), f32),
            pltpu.VMEM((N_PAIRS, Q_SUPER, LANES), f32),
            pltpu.VMEM((N_PAIRS, Q_SUPER, LANES), f32),
            pltpu.VMEM((N_PAIRS, Q_SUPER, LANES), f32),
        ],
        compiler_params=pltpu.CompilerParams(
            dimension_semantics=("arbitrary", "arbitrary"),
            vmem_limit_bytes=VMEM_LIMIT_BYTES),
        name="attn_prompt",
    )(_prompt_bias_table(rel_bias), slabs, slabs, slabs, attn_out_g.reshape(1, D_A))


def _log_sigmoid(x):
    return -(jnp.maximum(-x, 0.0) + jnp.log1p(jnp.exp(-jnp.abs(x))))


def _mlstm_prompt_kernel(xb_ref, ob_ref, gt_ref, gbias_ref, cw_ref, cb_ref, wq_ref, wk_ref, mg_ref,
                         skip_ref, bias_ref, cnt_ref, bias0_ref, zt_ref, ga_ref, kt_ref, vt_ref,
                         cs_ref, qts_ref, kts_ref, vts_ref, vsts_ref, rows_ref, obs_ref, cacts_ref, mgs_ref,
                         skips_ref,
                         out_ref, c_out_ref, n_out_ref, m_out_ref, conv_out_ref, ot_ref, nk_ref, nv_ref,
                         cs_out_ref, outs_ref,
                         conv_scr, c_scr, n_scr, m_scr, cq_scr, *, n_vb):
    c_idx = pl.program_id(1)
    L = MLSTM_CHUNK

    seq_idx = pl.program_id(0) * pl.num_programs(1) + c_idx

    @pl.when(seq_idx == 0)
    def _init_decode():
        ot_ref[...] = jnp.zeros_like(ot_ref)

    new_col = _new_token_column(seq_idx, zt_ref)
    _attn_step(seq_idx, new_col, bias_ref, cnt_ref, bias0_ref, ga_ref, kt_ref, vt_ref, ot_ref, nk_ref, nv_ref)
    _mlstm_step_state_body(lax.rem(seq_idx, n_vb), n_vb, cs_ref, qts_ref, kts_ref, vts_ref, vsts_ref, rows_ref,
                           obs_ref, cacts_ref, mgs_ref, skips_ref, cs_out_ref, outs_ref, cq_scr)

    @pl.when(c_idx == 0)
    def _init():
        conv_scr[0:SUBLANES, :] = jnp.zeros((SUBLANES, D_B_PAD), f32)
        c_scr[...] = jnp.zeros_like(c_scr)
        n_scr[...] = jnp.zeros_like(n_scr)
        m_scr[...] = jnp.zeros_like(m_scr)

    x = xb_ref[...]
    conv_scr[SUBLANES:SUBLANES + L, :] = x
    c = cb_ref[...] + x * cw_ref[CONV_W - 1:CONV_W, :]
    for i in range(CONV_W - 1):
        sh = CONV_W - 1 - i
        c = c + conv_scr[SUBLANES - sh:SUBLANES - sh + L, :] * cw_ref[i:i + 1, :]
    conv_scr[0:SUBLANES, :] = x[L - SUBLANES:, :]
    c_act = c * jax.nn.sigmoid(c)

    gts = gt_ref[...] + gbias_ref[...]
    logf = _log_sigmoid(gts)
    row = lax.broadcasted_iota(jnp.int32, (L, L), 0)
    col = lax.broadcasted_iota(jnp.int32, (L, L), 1)
    causal = row >= col
    a_all = jnp.dot(causal.astype(f32), logf, precision=lax.Precision.HIGHEST,
                    preferred_element_type=f32)
    gts_t = gts.T
    a_t = a_all.T

    for h in range(N_HEADS_B):
        sl = slice(h * HEAD_PAD, (h + 1) * HEAD_PAD)
        ch = c_act[:, sl]
        ch_bf = ch.astype(bf16)
        q = jnp.dot(ch_bf, wq_ref[h], preferred_element_type=f32)
        k = jnp.dot(ch_bf, wk_ref[h], preferred_element_type=f32) * (1.0 / math.sqrt(HEAD_DIM_B))
        v = x[:, sl]
        q_bf, k_bf = q.astype(bf16), k.astype(bf16)
        a_col = a_all[:, N_HEADS_B + h:N_HEADS_B + h + 1]
        i_col = gts[:, h:h + 1]
        a_row = a_t[N_HEADS_B + h:N_HEADS_B + h + 1, :]
        i_row = gts_t[h:h + 1, :]
        m_prev = m_scr[h]
        dmat = jnp.where(causal, a_col - a_row + i_row, NEG)
        g = a_col + m_prev
        m_t = jnp.maximum(g, jnp.max(dmat, axis=-1, keepdims=True))
        w_state = jnp.exp(g - m_t)
        s = lax.dot_general(q_bf, k_bf, (((1,), (1,)), ((), ())), preferred_element_type=f32)
        amat = s * jnp.exp(dmat - m_t)
        c_state = c_scr[h]
        inter = lax.dot_general(q_bf, c_state.astype(bf16), (((1,), (1,)), ((), ())),
                                preferred_element_type=f32)
        num = w_state * inter + jnp.dot(amat.astype(bf16), v.astype(bf16),
                                        preferred_element_type=f32)
        n_row = n_scr[h]
        den = (w_state * jnp.sum(q * n_row, axis=-1, keepdims=True)
               + jnp.sum(amat, axis=-1, keepdims=True))
        hh = num / jnp.maximum(jnp.abs(den), jnp.exp(-m_t))
        hn = hh * lax.rsqrt(jnp.sum(hh * hh, axis=-1, keepdims=True) * (1.0 / HEAD_DIM_B) + EPS)
        hn = hn * mg_ref[:, sl]
        out_ref[:, sl] = jax.nn.sigmoid(ob_ref[:, sl]) * (hn + skip_ref[:, sl] * ch)

        b_tot = a_col[L - 1:L, :]
        wl = b_tot - a_col + i_col
        m_new = jnp.maximum(b_tot + m_prev, jnp.max(wl, axis=0, keepdims=True))
        wk = jnp.exp(wl - m_new)
        decay = jnp.exp(b_tot + m_prev - m_new)
        upd = lax.dot_general((wk * v).astype(bf16), k_bf, (((0,), (0,)), ((), ())),
                              preferred_element_type=f32)
        c_scr[h] = decay * c_state + upd
        n_scr[h] = decay * n_row + jnp.sum(wk * k, axis=0, keepdims=True)
        m_scr[h] = m_new

    @pl.when(c_idx == pl.num_programs(1) - 1)
    def _final():
        c_out_ref[0] = c_scr[...]
        n_out_ref[0] = n_scr[...]
        m_out_ref[0] = m_scr[...]
        conv_out_ref[0] = x[L - SUBLANES:, :]


def _mlstm_prompt_attn_step(xb, ob, gates, gbias, cw, cb, wq, wk, mg, skip, batch, seq,
                            zt, cache_kt, cache_vt, rel_bias, attn_out_g,
                            ct, qt, kt, vt, vst, srows, ob3, cact3, mg3, skip3):
    L = MLSTM_CHUNK
    assert seq % L == 0
    nc = seq // L
    nb, p = cache_kt.shape[0], cache_kt.shape[-1]
    assert nb == batch * nc, "one decode sequence per (prompt sequence, chunk) grid step"
    for (w, d) in DILATED_BRANCHES:
        assert w <= p and w // d == BLK
    nh, e = ct.shape[0], ct.shape[1]
    assert (batch * nc) % nh == 0
    n_vb = batch * nc // nh
    assert e % n_vb == 0
    head = lambda shape: pl.BlockSpec((1,) + shape,
                                      lambda bi, ci: ((bi * nc + ci) // n_vb,) + (0,) * len(shape))
    cblk = pl.BlockSpec((1, e // n_vb, e, nb),
                        lambda bi, ci: ((bi * nc + ci) // n_vb, (bi * nc + ci) % n_vb, 0, 0))
    bias, count, bias0 = _step_bias_rows(rel_bias, p)
    win = pl.BlockSpec((1, N_HEADS_A, HEAD_DIM_A, p), lambda bi, ci: (bi * nc + ci, 0, 0, 0))
    buf = jax.ShapeDtypeStruct(cache_kt.shape, f32)
    rows = lambda w: pl.BlockSpec((L, w), lambda bi, ci: (bi * nc + ci, 0))
    const2 = lambda shape: pl.BlockSpec(shape, lambda bi, ci: (0, 0))
    const3 = lambda shape: pl.BlockSpec(shape, lambda bi, ci: (0, 0, 0))
    state = lambda shape: pl.BlockSpec((1,) + shape, lambda bi, ci: (bi,) + (0,) * len(shape))
    return pl.pallas_call(
        functools.partial(_mlstm_prompt_kernel, n_vb=n_vb),
        out_shape=(
            jax.ShapeDtypeStruct((batch * seq, D_B_PAD), f32),
            jax.ShapeDtypeStruct((batch, N_HEADS_B, HEAD_PAD, HEAD_PAD), f32),
            jax.ShapeDtypeStruct((batch, N_HEADS_B, 1, HEAD_PAD), f32),
            jax.ShapeDtypeStruct((batch, N_HEADS_B, 1, 1), f32),
            jax.ShapeDtypeStruct((batch, SUBLANES, D_B_PAD), f32),
            jax.ShapeDtypeStruct((D_A, nb), f32), buf, buf,
            jax.ShapeDtypeStruct(ct.shape, f32), jax.ShapeDtypeStruct((nh, nb, HEAD_PAD), f32),
        ),
        grid=(batch, nc),
        in_specs=[rows(D_B_PAD), rows(D_B_PAD), rows(LANES), const2((1, LANES)),
                  const2((CONV_W, D_B_PAD)), const2((1, D_B_PAD)),
                  const3((N_HEADS_B, HEAD_PAD, HEAD_PAD)), const3((N_HEADS_B, HEAD_PAD, HEAD_PAD)),
                  const2((1, D_B_PAD)), const2((1, D_B_PAD)),
                  const3((N_HEADS_A, 1, p)), const2((1, p)), const3((N_HEADS_A, 1, 1)),
                  const2((3 * D_A, nb)), const2((D_A, 1)), win, win,
                  cblk, head((e, nb)), head((e, nb)), head((e, nb)), head((e, nb)), head((N_ROWS, nb)),
                  head((nb, HEAD_PAD)), head((nb, HEAD_PAD)), head((1, HEAD_PAD)), head((1, HEAD_PAD))],
        out_specs=(rows(D_B_PAD), state((N_HEADS_B, HEAD_PAD, HEAD_PAD)),
                   state((N_HEADS_B, 1, HEAD_PAD)), state((N_HEADS_B, 1, 1)),
                   state((SUBLANES, D_B_PAD)),
                   const2((D_A, nb)), win, win,
                   cblk, head((nb, HEAD_PAD))),
        scratch_shapes=[
            pltpu.VMEM((SUBLANES + L, D_B_PAD), f32),
            pltpu.VMEM((N_HEADS_B, HEAD_PAD, HEAD_PAD), f32),
            pltpu.VMEM((N_HEADS_B, 1, HEAD_PAD), f32),
            pltpu.VMEM((N_HEADS_B, 1, 1), f32),
            pltpu.VMEM((e, nb), f32),
        ],
        compiler_params=pltpu.CompilerParams(
            dimension_semantics=("arbitrary", "arbitrary"),
            vmem_limit_bytes=VMEM_LIMIT_BYTES),
        name="mlstm_prompt_attn_step",
    )(xb, ob, gates, gbias, cw, cb, wq, wk, mg, skip,
      bias, count, bias0, zt, attn_out_g.reshape(D_A, 1), cache_kt, cache_vt,
      ct, qt, kt, vt, vst, srows, ob3, cact3, mg3, skip3)


def _step_bias_rows(rel_bias, p):
    dist = p - jnp.arange(p)
    full = rel_bias[_t5_bucket(dist)].astype(f32).T
    count = sum(((dist % d == 0) & (dist <= w)).astype(f32) for (w, d) in DILATED_BRANCHES)
    zero = rel_bias[_t5_bucket(jnp.zeros((1,), jnp.int32))].astype(f32).T
    return jnp.where(count[None, :] > 0, full, NEG)[:, None, :], count[None, :], zero[:, None, :]


def _new_token_column(b, zt_ref):
    zt = zt_ref[...]
    is_b = lax.broadcasted_iota(jnp.int32, zt.shape, 1) == b
    return jnp.sum(jnp.where(is_b, zt, 0.0), axis=1, keepdims=True)


def _attn_step(b, col, bias_ref, cnt_ref, bias0_ref, g_ref, kt_ref, vt_ref, ot_ref, nk_ref, nv_ref):
    nbr = len(DILATED_BRANCHES)
    scale = 1.0 / math.sqrt(HEAD_DIM_A)
    cnt = cnt_ref[...]
    weights = []
    for h in range(N_HEADS_A):
        q_col = col[h * HEAD_DIM_A:(h + 1) * HEAD_DIM_A] * scale
        kn_col = col[D_A + h * HEAD_DIM_A:D_A + (h + 1) * HEAD_DIM_A]
        kt = kt_ref[0, h]
        s = jnp.sum(kt * q_col, axis=0, keepdims=True)
        s_new = jnp.sum(kn_col * q_col, axis=0, keepdims=True)
        sb = s + bias_ref[h]
        s0 = s_new + bias0_ref[h]
        m = jnp.maximum(jnp.max(sb, axis=1, keepdims=True), s0)
        pw = cnt * jnp.exp(sb - m)
        e0 = nbr * jnp.exp(s0 - m)
        weights.append((pw, e0, jnp.sum(pw, axis=1, keepdims=True) + e0))
    _window_roll(col, kt_ref, vt_ref, nk_ref, nv_ref)
    o_cols = []
    for h, (pw, e0, l) in enumerate(weights):
        vn_col = col[2 * D_A + h * HEAD_DIM_A:2 * D_A + (h + 1) * HEAD_DIM_A]
        vt = vt_ref[0, h]
        o_cols.append((jnp.sum(vt * pw, axis=1, keepdims=True) + e0 * vn_col) / l)
    o = jnp.concatenate(o_cols, axis=0)
    ssq = jnp.sum(o * o, axis=0, keepdims=True)
    o = o * lax.rsqrt(ssq * (1.0 / D_A) + EPS) * g_ref[...]
    sel = lax.broadcasted_iota(jnp.int32, ot_ref.shape, 1) == b
    ot_ref[...] = jnp.where(sel, o, ot_ref[...])


def _window_roll(col, kt_ref, vt_ref, nk_ref, nv_ref):
    p = kt_ref.shape[-1]
    is_last = lax.broadcasted_iota(jnp.int32, (HEAD_DIM_A, p), 1) == p - 1
    for base, src, dst in ((D_A, kt_ref, nk_ref), (2 * D_A, vt_ref, nv_ref)):
        for h in range(N_HEADS_A):
            new_col = col[base + h * HEAD_DIM_A:base + (h + 1) * HEAD_DIM_A]
            dst[0, h] = jnp.where(is_last, new_col, pltpu.roll(src[0, h], p - 1, axis=1))


N_ROWS = 3
ROW_W, ROW_A, ROW_R = range(N_ROWS)


def _mlstm_step_pre_kernel(xb_ref, gt_ref, gbias_ref, sc_ref, cw_ref, cb_ref, wq_ref, wk_ref, nt_ref, mt_ref,
                           cact_out, qt_out, kt_out, vt_out, vst_out, rows_out, nt_out, mt_out):
    E = HEAD_DIM_B
    x = xb_ref[...]
    c = cb_ref[...] + x * cw_ref[CONV_W - 1:CONV_W, :]
    for t in range(CONV_W - 1):
        c = c + sc_ref[t] * cw_ref[t:t + 1, :]
    c_act = c * jax.nn.sigmoid(c)
    cact_out[...] = c_act
    gts_t = (gt_ref[...] + gbias_ref[...]).T
    logf_t = _log_sigmoid(gts_t)
    for h in range(N_HEADS_B):
        sl = slice(h * HEAD_PAD, (h + 1) * HEAD_PAD)
        ch_bf = c_act[:, sl].astype(bf16)
        q = jnp.dot(ch_bf, wq_ref[h], preferred_element_type=f32)
        k = jnp.dot(ch_bf, wk_ref[h], preferred_element_type=f32) * (1.0 / math.sqrt(HEAD_DIM_B))
        qt = q.T[:E]
        kt = k.T[:E]
        vt = x[:, sl].T[:E]
        i_pre = gts_t[h:h + 1]
        a = logf_t[N_HEADS_B + h:N_HEADS_B + h + 1]
        m_old = mt_ref[h:h + 1]
        m_t = jnp.maximum(a + m_old, i_pre)
        w_state = jnp.exp(a + m_old - m_t)
        w_in = jnp.exp(i_pre - m_t)
        amat = jnp.sum(qt * kt, axis=0, keepdims=True) * w_in
        n_old = nt_ref[h]
        den = w_state * jnp.sum(n_old * qt, axis=0, keepdims=True) + amat
        nt_out[h] = w_state * n_old + w_in * kt
        mt_out[h:h + 1] = m_t
        qt_out[h] = qt
        kt_out[h] = kt
        vt_out[h] = vt
        vst_out[h] = w_in * vt
        rows_out[h, ROW_W:ROW_W + 1] = w_state
        rows_out[h, ROW_A:ROW_A + 1] = amat
        rows_out[h, ROW_R:ROW_R + 1] = 1.0 / jnp.maximum(jnp.abs(den), jnp.exp(-m_t))


def _mlstm_step_pre(xb, gts, gbias, sc_t, cw, cb, wq, wk, nt, mt):
    nb = xb.shape[0]
    hd = jax.ShapeDtypeStruct((N_HEADS_B, HEAD_DIM_B, nb), f32)
    return pl.pallas_call(
        _mlstm_step_pre_kernel,
        out_shape=(jax.ShapeDtypeStruct((nb, D_B_PAD), f32), hd, hd, hd, hd,
                   jax.ShapeDtypeStruct((N_HEADS_B, N_ROWS, nb), f32), hd,
                   jax.ShapeDtypeStruct((N_HEADS_B, nb), f32)),
        compiler_params=pltpu.CompilerParams(vmem_limit_bytes=VMEM_LIMIT_BYTES),
        name="mlstm_step_pre",
    )(xb, gts, gbias, sc_t, cw, cb, wq, wk, nt, mt)


def _mlstm_step_state_body(vb, n_vb, c_ref, qt_ref, kt_ref, vt_ref, vst_ref, rows_ref, ob_ref, cact_ref,
                           mg_ref, skip_ref, c_out, out_ref, cq_scr):
    E = HEAD_DIM_B
    n_rows = c_ref.shape[1]
    qt = qt_ref[0]
    kt = kt_ref[0]
    w_state = rows_ref[0, ROW_W:ROW_W + 1]

    def per_row(v, carry):
        c_old = c_ref[0, v]
        vs = vst_ref[0, pl.ds(vb * n_rows + v, 1), :]
        c_out[0, v] = w_state * c_old + vs * kt
        cq_scr[pl.ds(vb * n_rows + v, 1), :] = jnp.sum(c_old * qt, axis=0, keepdims=True)
        return carry

    lax.fori_loop(0, n_rows, per_row, 0, unroll=True)

    @pl.when(vb == n_vb - 1)
    def _finish_head():
        ht = (w_state * cq_scr[...] + rows_ref[0, ROW_A:ROW_A + 1] * vt_ref[0]) * rows_ref[0, ROW_R:ROW_R + 1]
        hh = jnp.concatenate([ht, jnp.zeros((HEAD_PAD - E, ht.shape[1]), f32)], axis=0).T
        hn = hh * lax.rsqrt(jnp.sum(hh * hh, axis=-1, keepdims=True) * (1.0 / E) + EPS) * mg_ref[0]
        out_ref[0] = jax.nn.sigmoid(ob_ref[0]) * (hn + skip_ref[0] * cact_ref[0])


def kernel(x_prompt, x_sample, cache_win_k, cache_win_v, state_conv, state_C, state_n, state_m,
           rel_bias, norm1_g, w_in, gate_bias, conv_w, conv_b, wq_head, wk_head, attn_out_g,
           mh_norm_g, skip, w_out, norm2_g, w_ff1, w_ff2, final_g):
    Bp, Sp, _ = x_prompt.shape
    Bs, Ss, _ = x_sample.shape
    assert Ss == 1
    g1 = norm1_g[0].reshape(1, D_MODEL)
    g2 = norm2_g[0].reshape(1, D_MODEL)
    gf = final_g.reshape(1, D_MODEL)
    w_in_pad = _pad_w_in(w_in[0].astype(bf16))
    wo_bf = w_out[0].astype(bf16)
    wo_rows = [wo_bf[:D_A], jnp.zeros((LANES, D_MODEL), bf16)]
    for h in range(N_HEADS_B):
        wo_rows += [wo_bf[D_A + h * HEAD_DIM_B:D_A + (h + 1) * HEAD_DIM_B],
                    jnp.zeros((HEAD_PAD - HEAD_DIM_B, D_MODEL), bf16)]
    wo = jnp.concatenate(wo_rows, axis=0)
    w1 = w_ff1[0].astype(bf16)
    w2 = w_ff2[0].astype(bf16)
    hpad = HEAD_PAD - HEAD_DIM_B
    wq_p = jnp.pad(wq_head[0], ((0, 0), (0, hpad), (0, hpad))).astype(bf16)
    wk_p = jnp.pad(wk_head[0], ((0, 0), (0, hpad), (0, hpad))).astype(bf16)
    gbias = jnp.pad(gate_bias[0], (0, LANES - N_GATES)).reshape(1, LANES)
    cw_p = _pad_heads(conv_w[0])
    cb_p = _pad_heads(conv_b[0]).reshape(1, D_B_PAD)
    mg_p = _pad_heads(mh_norm_g[0]).reshape(1, D_B_PAD)
    skip_p = _pad_heads(skip[0]).reshape(1, D_B_PAD)

    xp2 = x_prompt.reshape(Bp * Sp, D_MODEL)
    P = min(WINDOW_MAX, Sp)
    slabs, k_win, v_win, xb, ob, gts = _norm_inproj(xp2, g1, w_in_pad, tm=ROW_TILE, seq=Sp, win=P)
    out_a = _attn_prompt(slabs, rel_bias, attn_out_g[0], Bp, Sp)
    xs2 = x_sample.reshape(Bs, D_MODEL)
    zt_s, xb_s, ob_s, gts_s = _norm_inproj_step(xs2, g1, w_in[0][:, :3 * D_A].T.astype(bf16), w_in_pad)
    sc_t = _pad_heads(state_conv[0].transpose(1, 0, 2))
    cact_s, qt_s, kt_s, vt_s, vst_s, rows_s, nt_s, mt_s = _mlstm_step_pre(
        xb_s, gts_s, gbias, sc_t, cw_p, cb_p, wq_p, wk_p,
        state_n[0].transpose(1, 2, 0), state_m[0].T)
    heads3 = lambda t: t.reshape(t.shape[0], N_HEADS_B, HEAD_PAD).transpose(1, 0, 2)
    out_b, c_p, n_p, m_p, tail_p, oat_s, nkt, nvt, ct_s, outb3 = _mlstm_prompt_attn_step(
        xb, ob, gts, gbias, cw_p, cb_p, wq_p, wk_p, mg_p, skip_p, Bp, Sp,
        zt_s, cache_win_k[0].transpose(0, 2, 3, 1), cache_win_v[0].transpose(0, 2, 3, 1), rel_bias,
        attn_out_g[0],
        state_C[0].transpose(1, 2, 3, 0), qt_s, kt_s, vt_s, vst_s, rows_s,
        heads3(ob_s), heads3(cact_s), heads3(mg_p), heads3(skip_p))
    y_p = _out_ffn(xp2, out_a, out_b, wo, g2, w1, w2, gf, tm=ROW_TILE).reshape(Bp, Sp, D_MODEL)
    win5 = lambda t: t.reshape(N_PAIRS, Bp, P, LANES).transpose(1, 2, 0, 3).reshape(
        1, Bp, P, N_HEADS_A, HEAD_DIM_A)
    st_p = (win5(k_win), win5(v_win), _unpad_heads(tail_p[:, SUBLANES - (CONV_W - 1):])[None],
            c_p[:, :, :HEAD_DIM_B, :HEAD_DIM_B][None], n_p[:, :, 0, :HEAD_DIM_B][None],
            m_p[:, :, 0, 0][None])

    oa_s = oat_s.reshape(N_PAIRS, LANES, Bs).transpose(0, 2, 1)
    new_k = nkt.transpose(0, 3, 1, 2)[None]
    new_v = nvt.transpose(0, 3, 1, 2)[None]
    outb_s = outb3.transpose(1, 0, 2).reshape(Bs, D_B_PAD)
    y_s = _out_ffn(xs2, oa_s, outb_s, wo, g2, w1, w2, gf, tm=Bs).reshape(Bs, Ss, D_MODEL)
    new_conv = jnp.concatenate([state_conv[0][:, 1:], _unpad_heads(xb_s)[:, None]], axis=1)
    st_s = (new_k, new_v, new_conv[None], ct_s.transpose(3, 0, 1, 2)[None],
            nt_s.transpose(2, 0, 1)[None], mt_s.T[None])
    return (y_p, y_s) + st_p + st_s
```

```python
import functools
import math

import jax
import jax.numpy as jnp
from jax import lax
from jax.experimental import pallas as pl
from jax.experimental.pallas import tpu as pltpu

D_MODEL = 1024
HEAD_DIM_A = 64
N_HEADS_A = 6
D_A = N_HEADS_A * HEAD_DIM_A
D_B = D_MODEL - D_A
N_HEADS_B = 4
HEAD_DIM_B = D_B // N_HEADS_B
DILATED_BRANCHES = ((128, 1), (512, 4), (2048, 16))
WINDOW_MAX = 2048
N_BUCKETS = 32
REL_MAX_DIST = 2048
CONV_W = 4
MLSTM_CHUNK = 128
D_FF = 4 * D_MODEL
N_GATES = 2 * N_HEADS_B
SPLITS = [D_A, 2 * D_A, 3 * D_A, 3 * D_A + D_B, 3 * D_A + 2 * D_B]
EPS = 1e-6
NEG = -1e30

LANES = 128
SUBLANES = 8
VMEM_LIMIT_BYTES = 56 * 1024 * 1024
ROW_TILE = 512

N_PAIRS = D_A // LANES
N_SLABS = 3 * N_PAIRS
HEAD_PAD = 2 * LANES
D_B_PAD = N_HEADS_B * HEAD_PAD
D_IN_PAD = 3 * D_A + 2 * D_B_PAD + LANES
D_O_PAD = D_A + LANES + D_B_PAD
BLK = 128
RES = DILATED_BRANCHES[-1][1]
Q_SUPER = BLK * RES

f32 = jnp.float32
bf16 = jnp.bfloat16


def _rms(xf, g):
    return xf * lax.rsqrt(jnp.mean(xf * xf, axis=-1, keepdims=True) + EPS) * g


def _pad_heads(t):
    t = t.reshape(t.shape[:-1] + (N_HEADS_B, HEAD_DIM_B))
    t = jnp.pad(t, [(0, 0)] * (t.ndim - 1) + [(0, HEAD_PAD - HEAD_DIM_B)])
    return t.reshape(t.shape[:-2] + (D_B_PAD,))


def _unpad_heads(t):
    t = t.reshape(t.shape[:-1] + (N_HEADS_B, HEAD_PAD))[..., :HEAD_DIM_B]
    return t.reshape(t.shape[:-2] + (D_B,))


def _norm_inproj_kernel(x_ref, g_ref, w_ref, slab_ref, kwin_ref, vwin_ref, xb_ref, ob_ref, gt_ref):
    h = _rms(x_ref[...], g_ref[...]).astype(bf16)
    qkv = jnp.dot(h, w_ref[:, 0:3 * D_A], preferred_element_type=f32)
    for j in range(N_SLABS):
        slab_ref[j] = qkv[:, j * LANES:(j + 1) * LANES]
    for p in range(N_PAIRS):
        kwin_ref[p] = qkv[:, (N_PAIRS + p) * LANES:(N_PAIRS + p + 1) * LANES]
        vwin_ref[p] = qkv[:, (2 * N_PAIRS + p) * LANES:(2 * N_PAIRS + p + 1) * LANES]
    o = 3 * D_A
    xb_ref[...] = jnp.dot(h, w_ref[:, o:o + D_B_PAD], preferred_element_type=f32)
    o += D_B_PAD
    ob_ref[...] = jnp.dot(h, w_ref[:, o:o + D_B_PAD], preferred_element_type=f32)
    o += D_B_PAD
    gt_ref[...] = jnp.dot(h, w_ref[:, o:o + LANES], preferred_element_type=f32)


def _norm_inproj(x, g, w_pad_bf16, tm, seq, win):
    m, d = x.shape
    assert m % seq == 0 and seq % tm == 0 and win % tm == 0
    tiles_seq, tiles_win = seq // tm, win // tm
    row = lambda w: pl.BlockSpec((tm, w), lambda i: (i, 0))

    def win_block(i):
        return (0, (i // tiles_seq) * tiles_win + jnp.maximum(i % tiles_seq - (tiles_seq - tiles_win), 0), 0)

    return pl.pallas_call(
        _norm_inproj_kernel,
        out_shape=(
            jax.ShapeDtypeStruct((N_SLABS, m, LANES), f32),
            jax.ShapeDtypeStruct((N_PAIRS, (m // seq) * win, LANES), f32),
            jax.ShapeDtypeStruct((N_PAIRS, (m // seq) * win, LANES), f32),
            jax.ShapeDtypeStruct((m, D_B_PAD), f32),
            jax.ShapeDtypeStruct((m, D_B_PAD), f32),
            jax.ShapeDtypeStruct((m, LANES), f32),
        ),
        grid=(m // tm,),
        in_specs=[
            row(d),
            pl.BlockSpec((1, d), lambda i: (0, 0)),
            pl.BlockSpec((d, D_IN_PAD), lambda i: (0, 0), pipeline_mode=pl.Buffered(1)),
        ],
        out_specs=(
            pl.BlockSpec((N_SLABS, tm, LANES), lambda i: (0, i, 0)),
            pl.BlockSpec((N_PAIRS, tm, LANES), win_block),
            pl.BlockSpec((N_PAIRS, tm, LANES), win_block),
            row(D_B_PAD), row(D_B_PAD), row(LANES),
        ),
        compiler_params=pltpu.CompilerParams(
            dimension_semantics=("arbitrary",),
            vmem_limit_bytes=VMEM_LIMIT_BYTES),
        name="norm_inproj",
    )(x, g, w_pad_bf16)


def _norm_inproj_step_kernel(x_ref, g_ref, wt_ref, w_ref, zt_ref, xb_ref, ob_ref, gt_ref):
    h = _rms(x_ref[...], g_ref[...]).astype(bf16)
    zt_ref[...] = lax.dot_general(wt_ref[...], h, (((1,), (1,)), ((), ())), preferred_element_type=f32)
    o = 3 * D_A
    xb_ref[...] = jnp.dot(h, w_ref[:, o:o + D_B_PAD], preferred_element_type=f32)
    o += D_B_PAD
    ob_ref[...] = jnp.dot(h, w_ref[:, o:o + D_B_PAD], preferred_element_type=f32)
    o += D_B_PAD
    gt_ref[...] = jnp.dot(h, w_ref[:, o:o + LANES], preferred_element_type=f32)


def _norm_inproj_step(x, g, wqkv_t_bf16, w_pad_bf16):
    nb = x.shape[0]
    wide = jax.ShapeDtypeStruct((nb, D_B_PAD), f32)
    return pl.pallas_call(
        _norm_inproj_step_kernel,
        out_shape=(jax.ShapeDtypeStruct((3 * D_A, nb), f32), wide, wide,
                   jax.ShapeDtypeStruct((nb, LANES), f32)),
        compiler_params=pltpu.CompilerParams(vmem_limit_bytes=VMEM_LIMIT_BYTES),
        name="norm_inproj_step",
    )(x, g, wqkv_t_bf16, w_pad_bf16)


def _pad_w_in(w_in):
    zeros = lambda n: jnp.zeros((w_in.shape[0], n), w_in.dtype)
    pieces = [w_in[:, :3 * D_A]]
    for base in (SPLITS[2], SPLITS[3]):
        for h in range(N_HEADS_B):
            pieces += [w_in[:, base + h * HEAD_DIM_B:base + (h + 1) * HEAD_DIM_B], zeros(HEAD_PAD - HEAD_DIM_B)]
    pieces += [w_in[:, SPLITS[4]:], zeros(LANES - N_GATES)]
    return jnp.concatenate(pieces, axis=1)


def _out_ffn_kernel(x_ref, oa_ref, ob_ref, wo_ref, g2_ref, w1_ref, w2_ref, gf_ref, y_ref, *, ff_chunk):
    o = jnp.concatenate([oa_ref[p].astype(bf16) for p in range(N_PAIRS)]
                        + [jnp.zeros((ob_ref.shape[0], LANES), bf16), ob_ref[...].astype(bf16)], axis=1)
    x1 = x_ref[...] + jnp.dot(o, wo_ref[...], preferred_element_type=f32)
    h2 = _rms(x1, g2_ref[...]).astype(bf16)
    acc = x1
    for c in range(D_FF // ff_chunk):
        u = jnp.dot(h2, w1_ref[:, c * ff_chunk:(c + 1) * ff_chunk], preferred_element_type=f32)
        u = jnp.square(jnp.maximum(u, 0.0)).astype(bf16)
        acc = acc + jnp.dot(u, w2_ref[c * ff_chunk:(c + 1) * ff_chunk, :],
                            preferred_element_type=f32)
    y_ref[...] = _rms(acc, gf_ref[...])


def _out_ffn(x, oa, ob, wo, g2, w1, w2, gf, tm, ff_chunk=1024):
    m, d = x.shape
    assert wo.shape[0] == D_O_PAD
    const = lambda i: (0, 0)
    single = dict(pipeline_mode=pl.Buffered(1))
    row = lambda w: pl.BlockSpec((tm, w), lambda i: (i, 0))
    return pl.pallas_call(
        functools.partial(_out_ffn_kernel, ff_chunk=ff_chunk),
        out_shape=jax.ShapeDtypeStruct((m, d), f32),
        grid=(m // tm,),
        in_specs=[
            row(d), pl.BlockSpec((N_PAIRS, tm, LANES), lambda i: (0, i, 0)), row(D_B_PAD),
            pl.BlockSpec((D_O_PAD, d), const, **single),
            pl.BlockSpec((1, d), const),
            pl.BlockSpec((d, D_FF), const, **single),
            pl.BlockSpec((D_FF, d), const, **single),
            pl.BlockSpec((1, d), const),
        ],
        out_specs=row(d),
        compiler_params=pltpu.CompilerParams(
            dimension_semantics=("arbitrary",),
            vmem_limit_bytes=VMEM_LIMIT_BYTES),
        name="outproj_ffn",
    )(x, oa, ob, wo, g2, w1, w2, gf)


def _t5_bucket(dist):
    max_exact = N_BUCKETS // 2
    df = jnp.maximum(dist, 1).astype(jnp.float32)
    large = max_exact + (jnp.log(df / max_exact) / math.log(REL_MAX_DIST / max_exact)
                         * (N_BUCKETS - max_exact)).astype(jnp.int32)
    large = jnp.minimum(large, N_BUCKETS - 1)
    return jnp.where(dist < max_exact, dist, large)


def _prompt_bias_table(rel_bias):
    period = 3 * BLK + 1
    r = jnp.arange(period)
    t = jnp.where(r < 2 * BLK, r, r - period)
    j = BLK - t
    band = (j >= 0) & (j <= BLK)
    tabs = []
    for (_, d) in DILATED_BRANCHES:
        vals = rel_bias[_t5_bucket(jnp.clip(j, 0, None) * d)].astype(f32)
        w = jnp.where(band[:, None], vals, NEG).T
        flat = jnp.tile(w, (1, BLK))[:, :BLK * (period - 1)]
        toe = flat.reshape(N_HEADS_A, BLK, period - 1)[:, :, :2 * BLK]
        sub = RES // d
        if sub > 1:
            pos = jnp.arange(BLK)
            nat = (pos % (BLK // sub)) * sub + pos // (BLK // sub)
            pq = jax.nn.one_hot(nat, BLK, dtype=f32)
            pk = jnp.kron(jnp.eye(2, dtype=f32), pq)
            toe = jnp.einsum('pq,hqk->hpk', pq, toe, precision=lax.Precision.HIGHEST)
            toe = jnp.einsum('hpk,jk->hpj', toe, pk, precision=lax.Precision.HIGHEST)
        tabs.append(toe.reshape(N_PAIRS, 2 * BLK, 2 * BLK))
    return jnp.stack(tabs)


def _attn_prompt_kernel(bias_ref, q_ref, k_ref, v_ref, g_ref, o_ref, q_scr, kv_scr, m_scr, l_scr, acc_scr):
    n = pl.program_id(1)
    step = pl.program_id(0) * pl.num_programs(1) + n
    cur = lax.rem(step, 2)
    prv = 1 - cur

    @pl.when(step == 0)
    def _init():
        kv_scr[1] = jnp.zeros(kv_scr.shape[1:], f32)

    def deinterleave(r, carry):
        rows = pl.ds(pl.multiple_of(r * BLK, BLK), BLK)
        for p in range(N_PAIRS):
            q_scr[p, rows, :] = q_ref.at[p][pl.ds(r, BLK, stride=RES), :]
            kv_scr[cur, p, rows, :] = k_ref.at[p][pl.ds(r, BLK, stride=RES), :]
            kv_scr[cur, N_PAIRS + p, rows, :] = v_ref.at[p][pl.ds(r, BLK, stride=RES), :]
        return carry

    lax.fori_loop(0, RES, deinterleave, 0, unroll=4)

    lane = lax.broadcasted_iota(jnp.int32, (BLK, LANES), 1)
    low = lane < HEAD_DIM_A
    kcol = lax.broadcasted_iota(jnp.int32, (2 * BLK, 2 * BLK), 1)
    first_extra = jnp.where((kcol < BLK) & (n == 0), NEG, 0.0).astype(f32)

    def job(g, d, buf_a, off_a, off_b, res, extra):
        sub = RES // d
        run = BLK // sub

        def starts(off):
            return [pl.multiple_of((c * d + res) * BLK + off, SUBLANES) for c in range(sub)]

        def gather(ref, lead, off):
            return jnp.concatenate([ref[lead + (pl.ds(s, run), slice(None))] for s in starts(off)], axis=0)

        def scatter(ref, p, val):
            for c, s in enumerate(starts(off_b)):
                ref[p, pl.ds(s, run), :] = val[c * run:(c + 1) * run]

        def finish(p, s):
            vv = jnp.concatenate([gather(kv_scr, (buf_a, N_PAIRS + p), off_a),
                                  gather(kv_scr, (cur, N_PAIRS + p), off_b)], axis=0)
            m = jnp.max(s, axis=-1, keepdims=True)
            e = jnp.exp(s - m)
            l = jnp.sum(e, axis=-1, keepdims=True)
            o = jnp.dot(e.astype(bf16), vv.astype(bf16), preferred_element_type=f32)
            o_t = jnp.where(low, o[:BLK], o[BLK:])
            m_t = jnp.where(low, m[:BLK], m[BLK:])
            l_t = jnp.where(low, l[:BLK], l[BLK:])
            if g == 0:
                scatter(acc_scr, p, o_t)
                scatter(m_scr, p, m_t)
                scatter(l_scr, p, l_t)
            else:
                m_o = gather(m_scr, (p,), off_b)
                m_n = jnp.maximum(m_o, m_t)
                al = jnp.exp(m_o - m_n)
                be = jnp.exp(m_t - m_n)
                scatter(acc_scr, p, al * gather(acc_scr, (p,), off_b) + be * o_t)
                scatter(l_scr, p, al * gather(l_scr, (p,), off_b) + be * l_t)
                scatter(m_scr, p, m_n)

        scores = []
        for p in range(N_PAIRS):
            q2 = gather(q_scr, (p,), off_b) * (1.0 / math.sqrt(HEAD_DIM_A))
            qcat = jnp.concatenate([jnp.where(low, q2, 0.0), jnp.where(low, 0.0, q2)], axis=0)
            kk = jnp.concatenate([gather(kv_scr, (buf_a, p), off_a), gather(kv_scr, (cur, p), off_b)], axis=0)
            s = lax.dot_general(qcat.astype(bf16), kk.astype(bf16), (((1,), (1,)), ((), ())),
                                preferred_element_type=f32)
            s = s + bias_ref[g, p]
            if extra is not None:
                s = s + extra
            if d == 1:
                finish(p, s)
            else:
                scores.append(s)
        for p, s in enumerate(scores):
            finish(p, s)

    for g, (_, d) in enumerate(DILATED_BRANCHES):
        nu = Q_SUPER // (BLK * d)
        run = BLK * d // RES
        log_d = d.bit_length() - 1

        def first(res, carry, g=g, d=d, nu=nu, run=run):
            job(g, d, prv, (nu - 1) * run, 0, res, first_extra)
            return carry

        def rest(idx, carry, g=g, d=d, run=run, log_d=log_d):
            u = 1 + lax.shift_right_logical(idx, log_d)
            res = jnp.bitwise_and(idx, d - 1)
            job(g, d, cur, (u - 1) * run, u * run, res, None)
            return carry

        if d == 1:
            first(0, 0)
        else:
            lax.fori_loop(0, d, first, 0, unroll=4 if d == 4 else 8)
        if nu > 1:
            lax.fori_loop(0, (nu - 1) * d, rest, 0, unroll=5 if d == 1 else 4)

    def fin(r, carry):
        rows = pl.ds(pl.multiple_of(r * BLK, BLK), BLK)
        os_ = [acc_scr[p, rows, :] / l_scr[p, rows, :] for p in range(N_PAIRS)]
        ss = sum(jnp.sum(o * o, axis=-1, keepdims=True) for o in os_)
        sc = lax.rsqrt(ss * (1.0 / D_A) + EPS)
        for p in range(N_PAIRS):
            o_ref.at[p][pl.ds(r, BLK, stride=RES), :] = os_[p] * sc * g_ref[:, p * LANES:(p + 1) * LANES]
        return carry

    lax.fori_loop(0, RES, fin, 0, unroll=4)


def _attn_prompt(slabs, rel_bias, attn_out_g, batch, seq):
    assert seq % Q_SUPER == 0
    nsb = seq // Q_SUPER
    nbr = len(DILATED_BRANCHES)
    slab = lambda grp: pl.BlockSpec((N_PAIRS, Q_SUPER, LANES), lambda bi, ni: (grp, bi * nsb + ni, 0))
    return pl.pallas_call(
        _attn_prompt_kernel,
        out_shape=jax.ShapeDtypeStruct((N_PAIRS, batch * seq, LANES), f32),
        grid=(batch, nsb),
        in_specs=[
            pl.BlockSpec((nbr, N_PAIRS, 2 * BLK, 2 * BLK), lambda bi, ni: (0, 0, 0, 0),
                         pipeline_mode=pl.Buffered(1)),
            slab(0), slab(1), slab(2),
            pl.BlockSpec((1, D_A), lambda bi, ni: (0, 0)),
        ],
        out_specs=pl.BlockSpec((N_PAIRS, Q_SUPER, LANES), lambda bi, ni: (0, bi * nsb + ni, 0)),
        scratch_shapes=[
            pltpu.VMEM((N_PAIRS, Q_SUPER, LANES), f32),
            pltpu.VMEM((2, 2 * N_PAIRS, Q_SUPER, LANES), f32),
            pltpu.VMEM((N_PAIRS, Q_SUPER, LANES), f32),
            pltpu.VMEM((N_PAIRS, Q_SUPER, LANES), f32),
            pltpu.VMEM((N_PAIRS, Q_SUPER, LANES), f32),
        ],
        compiler_params=pltpu.CompilerParams(
            dimension_semantics=("arbitrary", "arbitrary"),
            vmem_limit_bytes=VMEM_LIMIT_BYTES),
        name="attn_prompt",
    )(_prompt_bias_table(rel_bias), slabs, slabs, slabs, attn_out_g.reshape(1, D_A))


def _log_sigmoid(x):
    return -(jnp.maximum(-x, 0.0) + jnp.log1p(jnp.exp(-jnp.abs(x))))


def _mlstm_prompt_kernel(xb_ref, ob_ref, gt_ref, gbias_ref, cw_ref, cb_ref, wq_ref, wk_ref, mg_ref,
                         skip_ref, bias_ref, cnt_ref, bias0_ref, zt_ref, ga_ref, kt_ref, vt_ref,
                         cs_ref, qts_ref, kts_ref, vts_ref, vsts_ref, rows_ref, obs_ref, cacts_ref, mgs_ref,
                         skips_ref,
                         out_ref, c_out_ref, n_out_ref, m_out_ref, conv_out_ref, ot_ref, nk_ref, nv_ref,
                         cs_out_ref, outs_ref,
                         conv_scr, c_scr, n_scr, m_scr, cq_scr, *, n_vb):
    c_idx = pl.program_id(1)
    L = MLSTM_CHUNK

    seq_idx = pl.program_id(0) * pl.num_programs(1) + c_idx

    @pl.when(seq_idx == 0)
    def _init_decode():
        ot_ref[...] = jnp.zeros_like(ot_ref)

    new_col = _new_token_column(seq_idx, zt_ref)
    _attn_step(seq_idx, new_col, bias_ref, cnt_ref, bias0_ref, ga_ref, kt_ref, vt_ref, ot_ref, nk_ref, nv_ref)
    _mlstm_step_state_body(lax.rem(seq_idx, n_vb), n_vb, cs_ref, qts_ref, kts_ref, vts_ref, vsts_ref, rows_ref,
                           obs_ref, cacts_ref, mgs_ref, skips_ref, cs_out_ref, outs_ref, cq_scr)

    @pl.when(c_idx == 0)
    def _init():
        conv_scr[0:SUBLANES, :] = jnp.zeros((SUBLANES, D_B_PAD), f32)
        c_scr[...] = jnp.zeros_like(c_scr)
        n_scr[...] = jnp.zeros_like(n_scr)
        m_scr[...] = jnp.zeros_like(m_scr)

    x = xb_ref[...]
    conv_scr[SUBLANES:SUBLANES + L, :] = x
    c = cb_ref[...] + x * cw_ref[CONV_W - 1:CONV_W, :]
    for i in range(CONV_W - 1):
        sh = CONV_W - 1 - i
        c = c + conv_scr[SUBLANES - sh:SUBLANES - sh + L, :] * cw_ref[i:i + 1, :]
    conv_scr[0:SUBLANES, :] = x[L - SUBLANES:, :]
    c_act = c * jax.nn.sigmoid(c)

    gts = gt_ref[...] + gbias_ref[...]
    logf = _log_sigmoid(gts)
    row = lax.broadcasted_iota(jnp.int32, (L, L), 0)
    col = lax.broadcasted_iota(jnp.int32, (L, L), 1)
    causal = row >= col
    a_all = jnp.dot(causal.astype(f32), logf, precision=lax.Precision.HIGHEST,
                    preferred_element_type=f32)
    gts_t = gts.T
    a_t = a_all.T

    for h in range(N_HEADS_B):
        sl = slice(h * HEAD_PAD, (h + 1) * HEAD_PAD)
        ch = c_act[:, sl]
        ch_bf = ch.astype(bf16)
        q = jnp.dot(ch_bf, wq_ref[h], preferred_element_type=f32)
        k = jnp.dot(ch_bf, wk_ref[h], preferred_element_type=f32) * (1.0 / math.sqrt(HEAD_DIM_B))
        v = x[:, sl]
        q_bf, k_bf = q.astype(bf16), k.astype(bf16)
        a_col = a_all[:, N_HEADS_B + h:N_HEADS_B + h + 1]
        i_col = gts[:, h:h + 1]
        a_row = a_t[N_HEADS_B + h:N_HEADS_B + h + 1, :]
        i_row = gts_t[h:h + 1, :]
        m_prev = m_scr[h]
        dmat = jnp.where(causal, a_col - a_row + i_row, NEG)
        g = a_col + m_prev
        m_t = jnp.maximum(g, jnp.max(dmat, axis=-1, keepdims=True))
        w_state = jnp.exp(g - m_t)
        s = lax.dot_general(q_bf, k_bf, (((1,), (1,)), ((), ())), preferred_element_type=f32)
        amat = s * jnp.exp(dmat - m_t)
        c_state = c_scr[h]
        inter = lax.dot_general(q_bf, c_state.astype(bf16), (((1,), (1,)), ((), ())),
                                preferred_element_type=f32)
        num = w_state * inter + jnp.dot(amat.astype(bf16), v.astype(bf16),
                                        preferred_element_type=f32)
        n_row = n_scr[h]
        den = (w_state * jnp.sum(q * n_row, axis=-1, keepdims=True)
               + jnp.sum(amat, axis=-1, keepdims=True))
        hh = num / jnp.maximum(jnp.abs(den), jnp.exp(-m_t))
        hn = hh * lax.rsqrt(jnp.sum(hh * hh, axis=-1, keepdims=True) * (1.0 / HEAD_DIM_B) + EPS)
        hn = hn * mg_ref[:, sl]
        out_ref[:, sl] = jax.nn.sigmoid(ob_ref[:, sl]) * (hn + skip_ref[:, sl] * ch)

        b_tot = a_col[L - 1:L, :]
        wl = b_tot - a_col + i_col
        m_new = jnp.maximum(b_tot + m_prev, jnp.max(wl, axis=0, keepdims=True))
        wk = jnp.exp(wl - m_new)
        decay = jnp.exp(b_tot + m_prev - m_new)
        upd = lax.dot_general((wk * v).astype(bf16), k_bf, (((0,), (0,)), ((), ())),
                              preferred_element_type=f32)
        c_scr[h] = decay * c_state + upd
        n_scr[h] = decay * n_row + jnp.sum(wk * k, axis=0, keepdims=True)
        m_scr[h] = m_new

    @pl.when(c_idx == pl.num_programs(1) - 1)
    def _final():
        c_out_ref[0] = c_scr[...]
        n_out_ref[0] = n_scr[...]
        m_out_ref[0] = m_scr[...]
        conv_out_ref[0] = x[L - SUBLANES:, :]


def _mlstm_prompt_attn_step(xb, ob, gates, gbias, cw, cb, wq, wk, mg, skip, batch, seq,
                            zt, cache_kt, cache_vt, rel_bias, attn_out_g,
                            ct, qt, kt, vt, vst, srows, ob3, cact3, mg3, skip3):
    L = MLSTM_CHUNK
    assert seq % L == 0
    nc = seq // L
    nb, p = cache_kt.shape[0], cache_kt.shape[-1]
    assert nb == batch * nc, "one decode sequence per (prompt sequence, chunk) grid step"
    for (w, d) in DILATED_BRANCHES:
        assert w <= p and w // d == BLK
    nh, e = ct.shape[0], ct.shape[1]
    assert (batch * nc) % nh == 0
    n_vb = batch * nc // nh
    assert e % n_vb == 0
    head = lambda shape: pl.BlockSpec((1,) + shape,
                                      lambda bi, ci: ((bi * nc + ci) // n_vb,) + (0,) * len(shape))
    cblk = pl.BlockSpec((1, e // n_vb, e, nb),
                        lambda bi, ci: ((bi * nc + ci) // n_vb, (bi * nc + ci) % n_vb, 0, 0))
    bias, count, bias0 = _step_bias_rows(rel_bias, p)
    win = pl.BlockSpec((1, N_HEADS_A, HEAD_DIM_A, p), lambda bi, ci: (bi * nc + ci, 0, 0, 0))
    buf = jax.ShapeDtypeStruct(cache_kt.shape, f32)
    rows = lambda w: pl.BlockSpec((L, w), lambda bi, ci: (bi * nc + ci, 0))
    const2 = lambda shape: pl.BlockSpec(shape, lambda bi, ci: (0, 0))
    const3 = lambda shape: pl.BlockSpec(shape, lambda bi, ci: (0, 0, 0))
    state = lambda shape: pl.BlockSpec((1,) + shape, lambda bi, ci: (bi,) + (0,) * len(shape))
    return pl.pallas_call(
        functools.partial(_mlstm_prompt_kernel, n_vb=n_vb),
        out_shape=(
            jax.ShapeDtypeStruct((batch * seq, D_B_PAD), f32),
            jax.ShapeDtypeStruct((batch, N_HEADS_B, HEAD_PAD, HEAD_PAD), f32),
            jax.ShapeDtypeStruct((batch, N_HEADS_B, 1, HEAD_PAD), f32),
            jax.ShapeDtypeStruct((batch, N_HEADS_B, 1, 1), f32),
            jax.ShapeDtypeStruct((batch, SUBLANES, D_B_PAD), f32),
            jax.ShapeDtypeStruct((D_A, nb), f32), buf, buf,
            jax.ShapeDtypeStruct(ct.shape, f32), jax.ShapeDtypeStruct((nh, nb, HEAD_PAD), f32),
        ),
        grid=(batch, nc),
        in_specs=[rows(D_B_PAD), rows(D_B_PAD), rows(LANES), const2((1, LANES)),
                  const2((CONV_W, D_B_PAD)), const2((1, D_B_PAD)),
                  const3((N_HEADS_B, HEAD_PAD, HEAD_PAD)), const3((N_HEADS_B, HEAD_PAD, HEAD_PAD)),
                  const2((1, D_B_PAD)), const2((1, D_B_PAD)),
                  const3((N_HEADS_A, 1, p)), const2((1, p)), const3((N_HEADS_A, 1, 1)),
                  const2((3 * D_A, nb)), const2((D_A, 1)), win, win,
                  cblk, head((e, nb)), head((e, nb)), head((e, nb)), head((e, nb)), head((N_ROWS, nb)),
                  head((nb, HEAD_PAD)), head((nb, HEAD_PAD)), head((1, HEAD_PAD)), head((1, HEAD_PAD))],
        out_specs=(rows(D_B_PAD), state((N_HEADS_B, HEAD_PAD, HEAD_PAD)),
                   state((N_HEADS_B, 1, HEAD_PAD)), state((N_HEADS_B, 1, 1)),
                   state((SUBLANES, D_B_PAD)),
                   const2((D_A, nb)), win, win,
                   cblk, head((nb, HEAD_PAD))),
        scratch_shapes=[
            pltpu.VMEM((SUBLANES + L, D_B_PAD), f32),
            pltpu.VMEM((N_HEADS_B, HEAD_PAD, HEAD_PAD), f32),
            pltpu.VMEM((N_HEADS_B, 1, HEAD_PAD), f32),
            pltpu.VMEM((N_HEADS_B, 1, 1), f32),
            pltpu.VMEM((e, nb), f32),
        ],
        compiler_params=pltpu.CompilerParams(
            dimension_semantics=("arbitrary", "arbitrary"),
            vmem_limit_bytes=VMEM_LIMIT_BYTES),
        name="mlstm_prompt_attn_step",
    )(xb, ob, gates, gbias, cw, cb, wq, wk, mg, skip,
      bias, count, bias0, zt, attn_out_g.reshape(D_A, 1), cache_kt, cache_vt,
      ct, qt, kt, vt, vst, srows, ob3, cact3, mg3, skip3)


def _step_bias_rows(rel_bias, p):
    dist = p - jnp.arange(p)
    full = rel_bias[_t5_bucket(dist)].astype(f32).T
    count = sum(((dist % d == 0) & (dist <= w)).astype(f32) for (w, d) in DILATED_BRANCHES)
    zero = rel_bias[_t5_bucket(jnp.zeros((1,), jnp.int32))].astype(f32).T
    return jnp.where(count[None, :] > 0, full, NEG)[:, None, :], count[None, :], zero[:, None, :]


def _new_token_column(b, zt_ref):
    zt = zt_ref[...]
    is_b = lax.broadcasted_iota(jnp.int32, zt.shape, 1) == b
    return jnp.sum(jnp.where(is_b, zt, 0.0), axis=1, keepdims=True)


def _attn_step(b, col, bias_ref, cnt_ref, bias0_ref, g_ref, kt_ref, vt_ref, ot_ref, nk_ref, nv_ref):
    nbr = len(DILATED_BRANCHES)
    scale = 1.0 / math.sqrt(HEAD_DIM_A)
    cnt = cnt_ref[...]
    weights = []
    for h in range(N_HEADS_A):
        q_col = col[h * HEAD_DIM_A:(h + 1) * HEAD_DIM_A] * scale
        kn_col = col[D_A + h * HEAD_DIM_A:D_A + (h + 1) * HEAD_DIM_A]
        kt = kt_ref[0, h]
        s = jnp.sum(kt * q_col, axis=0, keepdims=True)
        s_new = jnp.sum(kn_col * q_col, axis=0, keepdims=True)
        sb = s + bias_ref[h]
        s0 = s_new + bias0_ref[h]
        m = jnp.maximum(jnp.max(sb, axis=1, keepdims=True), s0)
        pw = cnt * jnp.exp(sb - m)
        e0 = nbr * jnp.exp(s0 - m)
        weights.append((pw, e0, jnp.sum(pw, axis=1, keepdims=True) + e0))
    _window_roll(col, kt_ref, vt_ref, nk_ref, nv_ref)
    o_cols = []
    for h, (pw, e0, l) in enumerate(weights):
        vn_col = col[2 * D_A + h * HEAD_DIM_A:2 * D_A + (h + 1) * HEAD_DIM_A]
        vt = vt_ref[0, h]
        o_cols.append((jnp.sum(vt * pw, axis=1, keepdims=True) + e0 * vn_col) / l)
    o = jnp.concatenate(o_cols, axis=0)
    ssq = jnp.sum(o * o, axis=0, keepdims=True)
    o = o * lax.rsqrt(ssq * (1.0 / D_A) + EPS) * g_ref[...]
    sel = lax.broadcasted_iota(jnp.int32, ot_ref.shape, 1) == b
    ot_ref[...] = jnp.where(sel, o, ot_ref[...])


def _window_roll(col, kt_ref, vt_ref, nk_ref, nv_ref):
    p = kt_ref.shape[-1]
    is_last = lax.broadcasted_iota(jnp.int32, (HEAD_DIM_A, p), 1) == p - 1
    for base, src, dst in ((D_A, kt_ref, nk_ref), (2 * D_A, vt_ref, nv_ref)):
        for h in range(N_HEADS_A):
            new_col = col[base + h * HEAD_DIM_A:base + (h + 1) * HEAD_DIM_A]
            dst[0, h] = jnp.where(is_last, new_col, pltpu.roll(src[0, h], p - 1, axis=1))


N_ROWS = 3
ROW_W, ROW_A, ROW_R = range(N_ROWS)


def _mlstm_step_pre_kernel(xb_ref, gt_ref, gbias_ref, sc_ref, cw_ref, cb_ref, wq_ref, wk_ref, nt_ref, mt_ref,
                           cact_out, qt_out, kt_out, vt_out, vst_out, rows_out, nt_out, mt_out):
    E = HEAD_DIM_B
    x = xb_ref[...]
    c = cb_ref[...] + x * cw_ref[CONV_W - 1:CONV_W, :]
    for t in range(CONV_W - 1):
        c = c + sc_ref[t] * cw_ref[t:t + 1, :]
    c_act = c * jax.nn.sigmoid(c)
    cact_out[...] = c_act
    gts_t = (gt_ref[...] + gbias_ref[...]).T
    logf_t = _log_sigmoid(gts_t)
    for h in range(N_HEADS_B):
        sl = slice(h * HEAD_PAD, (h + 1) * HEAD_PAD)
        ch_bf = c_act[:, sl].astype(bf16)
        q = jnp.dot(ch_bf, wq_ref[h], preferred_element_type=f32)
        k = jnp.dot(ch_bf, wk_ref[h], preferred_element_type=f32) * (1.0 / math.sqrt(HEAD_DIM_B))
        qt = q.T[:E]
        kt = k.T[:E]
        vt = x[:, sl].T[:E]
        i_pre = gts_t[h:h + 1]
        a = logf_t[N_HEADS_B + h:N_HEADS_B + h + 1]
        m_old = mt_ref[h:h + 1]
        m_t = jnp.maximum(a + m_old, i_pre)
        w_state = jnp.exp(a + m_old - m_t)
        w_in = jnp.exp(i_pre - m_t)
        amat = jnp.sum(qt * kt, axis=0, keepdims=True) * w_in
        n_old = nt_ref[h]
        den = w_state * jnp.sum(n_old * qt, axis=0, keepdims=True) + amat
        nt_out[h] = w_state * n_old + w_in * kt
        mt_out[h:h + 1] = m_t
        qt_out[h] = qt
        kt_out[h] = kt
        vt_out[h] = vt
        vst_out[h] = w_in * vt
        rows_out[h, ROW_W:ROW_W + 1] = w_state
        rows_out[h, ROW_A:ROW_A + 1] = amat
        rows_out[h, ROW_R:ROW_R + 1] = 1.0 / jnp.maximum(jnp.abs(den), jnp.exp(-m_t))


def _mlstm_step_pre(xb, gts, gbias, sc_t, cw, cb, wq, wk, nt, mt):
    nb = xb.shape[0]
    hd = jax.ShapeDtypeStruct((N_HEADS_B, HEAD_DIM_B, nb), f32)
    return pl.pallas_call(
        _mlstm_step_pre_kernel,
        out_shape=(jax.ShapeDtypeStruct((nb, D_B_PAD), f32), hd, hd, hd, hd,
                   jax.ShapeDtypeStruct((N_HEADS_B, N_ROWS, nb), f32), hd,
                   jax.ShapeDtypeStruct((N_HEADS_B, nb), f32)),
        compiler_params=pltpu.CompilerParams(vmem_limit_bytes=VMEM_LIMIT_BYTES),
        name="mlstm_step_pre",
    )(xb, gts, gbias, sc_t, cw, cb, wq, wk, nt, mt)


def _mlstm_step_state_body(vb, n_vb, c_ref, qt_ref, kt_ref, vt_ref, vst_ref, rows_ref, ob_ref, cact_ref,
                           mg_ref, skip_ref, c_out, out_ref, cq_scr):
    E = HEAD_DIM_B
    n_rows = c_ref.shape[1]
    qt = qt_ref[0]
    kt = kt_ref[0]
    w_state = rows_ref[0, ROW_W:ROW_W + 1]

    def per_row(v, carry):
        c_old = c_ref[0, v]
        vs = vst_ref[0, pl.ds(vb * n_rows + v, 1), :]
        c_out[0, v] = w_state * c_old + vs * kt
        cq_scr[pl.ds(vb * n_rows + v, 1), :] = jnp.sum(c_old * qt, axis=0, keepdims=True)
        return carry

    lax.fori_loop(0, n_rows, per_row, 0, unroll=True)

    @pl.when(vb == n_vb - 1)
    def _finish_head():
        ht = (w_state * cq_scr[...] + rows_ref[0, ROW_A:ROW_A + 1] * vt_ref[0]) * rows_ref[0, ROW_R:ROW_R + 1]
        hh = jnp.concatenate([ht, jnp.zeros((HEAD_PAD - E, ht.shape[1]), f32)], axis=0).T
        hn = hh * lax.rsqrt(jnp.sum(hh * hh, axis=-1, keepdims=True) * (1.0 / E) + EPS) * mg_ref[0]
        out_ref[0] = jax.nn.sigmoid(ob_ref[0]) * (hn + skip_ref[0] * cact_ref[0])


def kernel(x_prompt, x_sample, cache_win_k, cache_win_v, state_conv, state_C, state_n, state_m,
           rel_bias, norm1_g, w_in, gate_bias, conv_w, conv_b, wq_head, wk_head, attn_out_g,
           mh_norm_g, skip, w_out, norm2_g, w_ff1, w_ff2, final_g):
    Bp, Sp, _ = x_prompt.shape
    Bs, Ss, _ = x_sample.shape
    assert Ss == 1
    g1 = norm1_g[0].reshape(1, D_MODEL)
    g2 = norm2_g[0].reshape(1, D_MODEL)
    gf = final_g.reshape(1, D_MODEL)
    w_in_pad = _pad_w_in(w_in[0].astype(bf16))
    wo_bf = w_out[0].astype(bf16)
    wo_rows = [wo_bf[:D_A], jnp.zeros((LANES, D_MODEL), bf16)]
    for h in range(N_HEADS_B):
        wo_rows += [wo_bf[D_A + h * HEAD_DIM_B:D_A + (h + 1) * HEAD_DIM_B],
                    jnp.zeros((HEAD_PAD - HEAD_DIM_B, D_MODEL), bf16)]
    wo = jnp.concatenate(wo_rows, axis=0)
    w1 = w_ff1[0].astype(bf16)
    w2 = w_ff2[0].astype(bf16)
    hpad = HEAD_PAD - HEAD_DIM_B
    wq_p = jnp.pad(wq_head[0], ((0, 0), (0, hpad), (0, hpad))).astype(bf16)
    wk_p = jnp.pad(wk_head[0], ((0, 0), (0, hpad), (0, hpad))).astype(bf16)
    gbias = jnp.pad(gate_bias[0], (0, LANES - N_GATES)).reshape(1, LANES)
    cw_p = _pad_heads(conv_w[0])
    cb_p = _pad_heads(conv_b[0]).reshape(1, D_B_PAD)
    mg_p = _pad_heads(mh_norm_g[0]).reshape(1, D_B_PAD)
    skip_p = _pad_heads(skip[0]).reshape(1, D_B_PAD)

    xp2 = x_prompt.reshape(Bp * Sp, D_MODEL)
    P = min(WINDOW_MAX, Sp)
    slabs, k_win, v_win, xb, ob, gts = _norm_inproj(xp2, g1, w_in_pad, tm=ROW_TILE, seq=Sp, win=P)
    out_a = _attn_prompt(slabs, rel_bias, attn_out_g[0], Bp, Sp)
    xs2 = x_sample.reshape(Bs, D_MODEL)
    zt_s, xb_s, ob_s, gts_s = _norm_inproj_step(xs2, g1, w_in[0][:, :3 * D_A].T.astype(bf16), w_in_pad)
    sc_t = _pad_heads(state_conv[0].transpose(1, 0, 2))
    cact_s, qt_s, kt_s, vt_s, vst_s, rows_s, nt_s, mt_s = _mlstm_step_pre(
        xb_s, gts_s, gbias, sc_t, cw_p, cb_p, wq_p, wk_p,
        state_n[0].transpose(1, 2, 0), state_m[0].T)
    heads3 = lambda t: t.reshape(t.shape[0], N_HEADS_B, HEAD_PAD).transpose(1, 0, 2)
    out_b, c_p, n_p, m_p, tail_p, oat_s, nkt, nvt, ct_s, outb3 = _mlstm_prompt_attn_step(
        xb, ob, gts, gbias, cw_p, cb_p, wq_p, wk_p, mg_p, skip_p, Bp, Sp,
        zt_s, cache_win_k[0].transpose(0, 2, 3, 1), cache_win_v[0].transpose(0, 2, 3, 1), rel_bias,
        attn_out_g[0],
        state_C[0].transpose(1, 2, 3, 0), qt_s, kt_s, vt_s, vst_s, rows_s,
        heads3(ob_s), heads3(cact_s), heads3(mg_p), heads3(skip_p))
    y_p = _out_ffn(xp2, out_a, out_b, wo, g2, w1, w2, gf, tm=ROW_TILE).reshape(Bp, Sp, D_MODEL)
    win5 = lambda t: t.reshape(N_PAIRS, Bp, P, LANES).transpose(1, 2, 0, 3).reshape(
        1, Bp, P, N_HEADS_A, HEAD_DIM_A)
    st_p = (win5(k_win), win5(v_win), _unpad_heads(tail_p[:, SUBLANES - (CONV_W - 1):])[None],
            c_p[:, :, :HEAD_DIM_B, :HEAD_DIM_B][None], n_p[:, :, 0, :HEAD_DIM_B][None],
            m_p[:, :, 0, 0][None])

    oa_s = oat_s.reshape(N_PAIRS, LANES, Bs).transpose(0, 2, 1)
    new_k = nkt.transpose(0, 3, 1, 2)[None]
    new_v = nvt.transpose(0, 3, 1, 2)[None]
    outb_s = outb3.transpose(1, 0, 2).reshape(Bs, D_B_PAD)
    y_s = _out_ffn(xs2, oa_s, outb_s, wo, g2, w1, w2, gf, tm=Bs).reshape(Bs, Ss, D_MODEL)
    new_conv = jnp.concatenate([state_conv[0][:, 1:], _unpad_heads(xb_s)[:, None]], axis=1)
    st_s = (new_k, new_v, new_conv[None], ct_s.transpose(3, 0, 1, 2)[None],
            nt_s.transpose(2, 0, 1)[None], mt_s.T[None])
    return (y_p, y_s) + st_p + st_s
```
